```python
import jax
import jax.numpy as jnp
from jax import lax
import numpy as np

D_MODEL = 2048
BATCH = 8
SEQ = 2048
DEPTH = 2
DEC_BATCH = 32
DEC_SEQ = 4
PAST_LEN = 8192
PAGE_SIZE = 128

DIL_GROUPS = ((128, 1), (512, 4), (2048, 16))
N_GROUPS = 3
A_HEADS = 8
A_HEAD_DIM = 128
A_WIDTH = A_HEADS * A_HEAD_DIM
ROPE_DIM = A_HEAD_DIM // 4
ROPE_THETA = 500000.0
GLA_HEADS = 4
GLA_DK = D_MODEL // 2 // GLA_HEADS
GLA_DV = D_MODEL // GLA_HEADS
GLA_RANK = 16
GLA_TAU = 16.0
GLA_CHUNK = 64
N_EXPERTS = 64
TOP_K = 6
N_EXPERT_GROUPS = 8
TOPK_GROUPS = 4
D_EXPERT = D_MODEL * 11 // 16
ROUTED_SCALE = 2.5
MOE_BLOCK = 128
EPS = 1e-6
NEG_INF = -1e30
F32 = jnp.float32

kernel_name = 'hybrid_dilated_gla_moe_adaln_step'


def rmsnorm(x, g):
    xf = x.astype(F32)
    y = xf * lax.rsqrt(jnp.mean(xf * xf, axis=-1, keepdims=True) + EPS)
    return (y * g.astype(F32)).astype(x.dtype)


def rope(x, pos):
    half = ROPE_DIM // 2
    inv = ROPE_THETA ** (-jnp.arange(half, dtype=F32) * 2.0 / ROPE_DIM)
    ang = pos.astype(F32)[:, None] * inv[None, :]
    cos = jnp.cos(ang)[None, :, None, :]
    sin = jnp.sin(ang)[None, :, None, :]
    xr = x[..., :ROPE_DIM].astype(F32)
    x1, x2 = xr[..., :half], xr[..., half:]
    rot = jnp.concatenate([x1 * cos - x2 * sin, x2 * cos + x1 * sin], axis=-1).astype(x.dtype)
    return jnp.concatenate([rot, x[..., ROPE_DIM:]], axis=-1)


def dilated_prompt(q, k, v, dil, span):
    B, S, H, E = q.shape
    L = S // dil
    nb = -(-L // span)
    Lp = nb * span

    def to_sub(a):
        a = a.reshape(B, L, dil, H, E).transpose(0, 2, 1, 3, 4)
        return jnp.pad(a, ((0, 0), (0, 0), (0, Lp - L), (0, 0), (0, 0)))

    def key_blocks(a):
        cur = to_sub(a)
        prev = jnp.pad(cur, ((0, 0), (0, 0), (span, 0), (0, 0), (0, 0)))[:, :, :Lp]
        shp = (B, dil, nb, span, H, E)
        return jnp.concatenate([prev.reshape(shp), cur.reshape(shp)], axis=3).astype(F32)

    qs = to_sub(q).reshape(B, dil, nb, span, H, E).astype(F32)
    kb = key_blocks(k)
    vb = key_blocks(v)
    s = jnp.einsum('bdnqhe,bdnkhe->bdnhqk', qs, kb) * (E ** -0.5)
    qi = jnp.arange(span)[:, None]
    ki = jnp.arange(2 * span)[None, :]
    band = (ki >= qi) & (ki <= qi + span)
    first = (jnp.arange(nb)[:, None] > 0) | (jnp.arange(2 * span)[None, :] >= span)
    mask = band[None, :, :] & first[:, None, :]
    s = jnp.where(mask[None, None, :, None], s, NEG_INF)
    lse = jax.nn.logsumexp(s, axis=-1)
    p = jnp.exp(s - lse[..., None])
    o = jnp.einsum('bdnhqk,bdnkhe->bdnqhe', p, vb)
    o = o.reshape(B, dil, Lp, H, E)[:, :, :L].transpose(0, 2, 1, 3, 4).reshape(B, S, H, E)
    lse = lse.transpose(0, 1, 2, 4, 3).reshape(B, dil, Lp, H)[:, :, :L]
    lse = lse.transpose(0, 2, 1, 3).reshape(B, S, H)
    return o, lse


def dilated_sample(q, k, v, ck, cv, dil, span):
    T = q.shape[1]
    E = q.shape[-1]
    Wc = ck.shape[1]
    kc = jnp.concatenate([ck, k.astype(ck.dtype)], axis=1)
    vc = jnp.concatenate([cv, v.astype(cv.dtype)], axis=1)
    idx = Wc + jnp.arange(T)[:, None] - dil * jnp.arange(span + 1)[None, :]
    valid = idx >= 0
    idx = jnp.maximum(idx, 0)
    kg = kc[:, idx].astype(F32)
    vg = vc[:, idx].astype(F32)
    s = jnp.einsum('bthe,btmhe->bhtm', q.astype(F32), kg) * (E ** -0.5)
    s = jnp.where(valid[None, None], s, NEG_INF)
    lse = jax.nn.logsumexp(s, axis=-1)
    p = jnp.exp(s - lse[..., None])
    o = jnp.einsum('bhtm,btmhe->bthe', p, vg)
    return o, lse.transpose(0, 2, 1)


def gla_chunked(q, k, v, log_a, s0, chunk):
    B, T, H, DK = q.shape
    DV = v.shape[-1]
    n = T // chunk

    def to_chunks(a):
        return a.astype(F32).reshape(B, n, chunk, H, a.shape[-1]).transpose(1, 0, 3, 2, 4)

    qc = to_chunks(q) * (DK ** -0.5)
    kc, vc, ac = to_chunks(k), to_chunks(v), to_chunks(log_a)
    causal = jnp.tril(jnp.ones((chunk, chunk), bool))

    def step(S, inp):
        qi, ki, vi, ai = inp
        b = jnp.cumsum(ai, axis=2)
        b_last = b[:, :, -1]
        q_dec = qi * jnp.exp(b)
        att = jnp.einsum('bhtk,bhsk->bhts', q_dec, ki * jnp.exp(-b))
        att = jnp.where(causal, att, 0.0)
        o = jnp.einsum('bhts,bhsv->bhtv', att, vi) + jnp.einsum('bhtk,bhkv->bhtv', q_dec, S)
        S = jnp.exp(b_last)[..., None] * S + jnp.einsum(
            'bhsk,bhsv->bhkv', ki * jnp.exp(b_last[:, :, None] - b), vi)
        return S, o

    S, o = lax.scan(step, s0.astype(F32), (qc, kc, vc, ac))
    return o.transpose(1, 0, 3, 2, 4).reshape(B, T, H, DV), S


def token_mixer(h, pos, lp, cache):
    B, T, _ = h.shape
    qa_w = N_GROUPS * A_WIDTH
    gq = GLA_HEADS * GLA_DK
    gv = GLA_HEADS * GLA_DV
    sizes = (qa_w, qa_w, qa_w, gq, gq, gv, gv, GLA_RANK, D_MODEL)
    cuts = [int(c) for c in np.cumsum(sizes)]
    proj = h @ lp['w_in']
    qa, ka, va, qb, kb, vb, rb, lr, ga, gb = jnp.split(proj, cuts, axis=-1)

    qa = rope(rmsnorm(qa.reshape(B, T, N_GROUPS * A_HEADS, A_HEAD_DIM), lp['q_norm']), pos)
    ka = rope(rmsnorm(ka.reshape(B, T, N_GROUPS * A_HEADS, A_HEAD_DIM), lp['k_norm']), pos)
    va = va.reshape(B, T, N_GROUPS * A_HEADS, A_HEAD_DIM)
    outs, lses, new_k, new_v = [], [], [], []
    for g, (win, dil) in enumerate(DIL_GROUPS):
        sl = slice(g * A_HEADS, (g + 1) * A_HEADS)
        q_g, k_g, v_g = qa[:, :, sl], ka[:, :, sl], va[:, :, sl]
        span = win // dil
        if cache is None:
            o_g, l_g = dilated_prompt(q_g, k_g, v_g, dil, span)
            keep = min(win, T)
            new_k.append(k_g[:, T - keep:])
            new_v.append(v_g[:, T - keep:])
        else:
            o_g, l_g = dilated_sample(q_g, k_g, v_g, cache[0][g], cache[1][g], dil, span)
            new_k.append(k_g)
            new_v.append(v_g)
        outs.append(o_g)
        lses.append(l_g)
    wts = jax.nn.softmax(jnp.stack(lses), axis=0)
    oa = jnp.einsum('gbth,gbthe->bthe', wts, jnp.stack(outs)).reshape(B, T, A_WIDTH).astype(h.dtype)

    qb = qb.reshape(B, T, GLA_HEADS, GLA_DK)
    kb = kb.reshape(B, T, GLA_HEADS, GLA_DK)
    vb = vb.reshape(B, T, GLA_HEADS, GLA_DV)
    log_a = jax.nn.log_sigmoid((lr @ lp['alpha_up'] + lp['alpha_bias']).astype(F32)) / GLA_TAU
    log_a = log_a.reshape(B, T, GLA_HEADS, GLA_DK)
    if cache is None:
        s0 = jnp.zeros((B, GLA_HEADS, GLA_DK, GLA_DV), F32)
    else:
        s0 = cache[2]
    chunk = GLA_CHUNK if T % GLA_CHUNK == 0 else T
    ob, S = gla_chunked(qb, kb, vb, log_a, s0, chunk)
    ob = rmsnorm(ob, lp['gla_norm']) * jax.nn.silu(rb.reshape(B, T, GLA_HEADS, GLA_DV).astype(F32))
    ob = ob.reshape(B, T, gv).astype(h.dtype)

    merged = (jax.nn.sigmoid(ga) * (oa @ lp['w_branch_a'])
              + jax.nn.sigmoid(gb) * (ob @ lp['w_branch_b']))
    return merged @ lp['w_out'], (new_k, new_v, S.astype(h.dtype))


def swiglu(x, wg, wu, wd):
    return (jax.nn.silu(x @ wg) * (x @ wu)) @ wd


def routed_experts(x2, idx, wts, w_gate, w_up, w_down):
    N, D = x2.shape
    E = w_gate.shape[0]
    NK = N * TOP_K
    flat_e = idx.reshape(-1)
    flat_t = jnp.arange(NK, dtype=jnp.int32) // TOP_K
    flat_w = wts.reshape(-1)
    order = jnp.argsort(flat_e)
    e_s, t_s, w_s = flat_e[order], flat_t[order], flat_w[order]
    counts = jnp.bincount(flat_e, length=E)
    start = jnp.cumsum(counts) - counts
    padded = (counts + MOE_BLOCK - 1) // MOE_BLOCK * MOE_BLOCK
    pend = jnp.cumsum(padded)
    dest = (pend - padded)[e_s] + jnp.arange(NK) - start[e_s]
    n_blocks = -(-NK // MOE_BLOCK) + E
    R = n_blocks * MOE_BLOCK
    row_tok = jnp.full((R,), N, jnp.int32).at[dest].set(t_s)
    row_w = jnp.zeros((R,), x2.dtype).at[dest].set(w_s.astype(x2.dtype))
    block_e = jnp.minimum(
        jnp.searchsorted(pend, jnp.arange(n_blocks) * MOE_BLOCK, side='right'), E - 1)
    x_pad = jnp.concatenate([x2, jnp.zeros((1, D), x2.dtype)], axis=0)

    def step(acc, inp):
        e, toks, w = inp
        xb = x_pad[toks]
        hb = jax.nn.silu(xb @ w_gate[e]) * (xb @ w_up[e])
        return acc.at[toks].add((hb @ w_down[e]) * w[:, None]), None

    acc, _ = lax.scan(step, jnp.zeros((N + 1, D), x2.dtype),
                      (block_e, row_tok.reshape(n_blocks, MOE_BLOCK),
                       row_w.reshape(n_blocks, MOE_BLOCK)))
    return acc[:N]


def moe_ffn(h, lp):
    B, T, D = h.shape
    x2 = h.reshape(B * T, D)
    N = B * T
    per = N_EXPERTS // N_EXPERT_GROUPS
    scores = jax.nn.sigmoid((x2 @ lp['w_router']).astype(F32))
    biased = scores + lp['router_bias'].astype(F32)
    grp = lax.top_k(biased.reshape(N, N_EXPERT_GROUPS, per), 2)[0].sum(-1)
    gidx = lax.top_k(grp, TOPK_GROUPS)[1]
    gmask = jnp.any(gidx[..., None] == jnp.arange(N_EXPERT_GROUPS), axis=1)
    emask = jnp.repeat(gmask, per, axis=-1)
    idx = lax.top_k(jnp.where(emask, biased, -jnp.inf), TOP_K)[1]
    w = jnp.take_along_axis(scores, idx, axis=-1)
    w = w / jnp.sum(w, axis=-1, keepdims=True) * ROUTED_SCALE
    routed = routed_experts(x2, idx, w, lp['w_gate_e'], lp['w_up_e'], lp['w_down_e'])
    shared = swiglu(x2, lp['w_gate_s'], lp['w_up_s'], lp['w_down_s'])
    return (routed + shared).reshape(B, T, D)


def decoder_layer(x, c, pos, lp, cache):
    mod = (jax.nn.silu(c) @ lp['w_ada'] + lp['b_ada'])[:, None, :]
    sh1, sc1, g1, sh2, sc2, g2 = jnp.split(mod, 6, axis=-1)
    h = rmsnorm(x, lp['norm_mix']) * (1 + sc1) + sh1
    mix, st = token_mixer(h, pos, lp, cache)
    x = x + g1 * mix
    h = rmsnorm(x, lp['norm_ffn']) * (1 + sc2) + sh2
    x = x + g2 * moe_ffn(h, lp)
    return x, st


def setup_inputs(seed: int = 0) -> dict:
    key = jax.random.key(seed)
    ks = jax.random.split(key, 32)
    D = D_MODEL
    F = D_EXPERT
    E = N_EXPERTS
    n_in = (3 * N_GROUPS * A_WIDTH + 2 * GLA_HEADS * GLA_DK + 2 * GLA_HEADS * GLA_DV
            + GLA_RANK + 2 * D)
    win = [min(w, PAST_LEN) for w, _ in DIL_GROUPS]

    def nrm(k, shape, scale):
        return jax.random.normal(k, shape, F32) * scale

    def cache(k, w):
        return nrm(k, (DEPTH, DEC_BATCH, w, A_HEADS, A_HEAD_DIM), 1.0)

    return {
        'x_prompt': nrm(ks[0], (BATCH, SEQ, D), 1.0),
        'x_sample': nrm(ks[1], (DEC_BATCH, DEC_SEQ, D), 1.0),
        'cache_k_w128': cache(ks[2], win[0]),
        'cache_v_w128': cache(ks[3], win[0]),
        'cache_k_w512': cache(ks[4], win[1]),
        'cache_v_w512': cache(ks[5], win[1]),
        'cache_k_w2048': cache(ks[6], win[2]),
        'cache_v_w2048': cache(ks[7], win[2]),
        'state_gla': nrm(ks[8], (DEPTH, DEC_BATCH, GLA_HEADS, GLA_DK, GLA_DV), 1.0),
        'c_prompt': nrm(ks[9], (BATCH, D), 1.0),
        'c_sample': nrm(ks[10], (DEC_BATCH, D), 1.0),
        'w_in': nrm(ks[11], (DEPTH, D, n_in), D ** -0.5),
        'q_norm': 1.0 + nrm(ks[12], (DEPTH, A_HEAD_DIM), 0.02),
        'k_norm': 1.0 + nrm(ks[13], (DEPTH, A_HEAD_DIM), 0.02),
        'alpha_up': nrm(ks[14], (DEPTH, GLA_RANK, GLA_HEADS * GLA_DK), GLA_RANK ** -0.5),
        'alpha_bias': nrm(ks[15], (DEPTH, GLA_HEADS * GLA_DK), 0.1),
        'gla_norm': 1.0 + nrm(ks[16], (DEPTH, GLA_DV), 0.02),
        'w_branch_a': nrm(ks[17], (DEPTH, A_WIDTH, D), A_WIDTH ** -0.5),
        'w_branch_b': nrm(ks[18], (DEPTH, GLA_HEADS * GLA_DV, D), (GLA_HEADS * GLA_DV) ** -0.5),
        'w_out': nrm(ks[19], (DEPTH, D, D), D ** -0.5),
        'norm_mix': 1.0 + nrm(ks[20], (DEPTH, D), 0.02),
        'norm_ffn': 1.0 + nrm(ks[21], (DEPTH, D), 0.02),
        'w_ada': nrm(ks[22], (DEPTH, D, 6 * D), 0.5 * D ** -0.5),
        'b_ada': nrm(ks[23], (DEPTH, 6 * D), 0.02),
        'w_router': nrm(ks[24], (DEPTH, D, E), D ** -0.5),
        'router_bias': nrm(ks[25], (DEPTH, E), 0.01),
        'w_gate_e': nrm(ks[26], (DEPTH, E, D, F), D ** -0.5),
        'w_up_e': nrm(ks[27], (DEPTH, E, D, F), D ** -0.5),
        'w_down_e': nrm(ks[28], (DEPTH, E, F, D), F ** -0.5),
        'w_gate_s': nrm(ks[29], (DEPTH, D, F), D ** -0.5),
        'w_up_s': nrm(ks[30], (DEPTH, D, F), D ** -0.5),
        'w_down_s': nrm(ks[31], (DEPTH, F, D), F ** -0.5),
    }


def reference(x_prompt, x_sample, cache_k_w128, cache_v_w128, cache_k_w512, cache_v_w512,
              cache_k_w2048, cache_v_w2048, state_gla, c_prompt, c_sample, w_in, q_norm,
              k_norm, alpha_up, alpha_bias, gla_norm, w_branch_a, w_branch_b, w_out,
              norm_mix, norm_ffn, w_ada, b_ada, w_router, router_bias, w_gate_e, w_up_e,
              w_down_e, w_gate_s, w_up_s, w_down_s):
    xp, xs = x_prompt, x_sample
    pos_p = jnp.arange(xp.shape[1])
    pos_s = PAST_LEN + jnp.arange(xs.shape[1])
    cache_k = (cache_k_w128, cache_k_w512, cache_k_w2048)
    cache_v = (cache_v_w128, cache_v_w512, cache_v_w2048)
    pk = [[] for _ in range(N_GROUPS)]
    pv = [[] for _ in range(N_GROUPS)]
    sk = [[] for _ in range(N_GROUPS)]
    sv = [[] for _ in range(N_GROUPS)]
    ps, ss = [], []
    for l in range(DEPTH):
        lp = {
            'w_in': w_in[l], 'q_norm': q_norm[l], 'k_norm': k_norm[l],
            'alpha_up': alpha_up[l], 'alpha_bias': alpha_bias[l], 'gla_norm': gla_norm[l],
            'w_branch_a': w_branch_a[l], 'w_branch_b': w_branch_b[l], 'w_out': w_out[l],
            'norm_mix': norm_mix[l], 'norm_ffn': norm_ffn[l], 'w_ada': w_ada[l],
            'b_ada': b_ada[l], 'w_router': w_router[l], 'router_bias': router_bias[l],
            'w_gate_e': w_gate_e[l], 'w_up_e': w_up_e[l], 'w_down_e': w_down_e[l],
            'w_gate_s': w_gate_s[l], 'w_up_s': w_up_s[l], 'w_down_s': w_down_s[l],
        }
        xp, (nk, nv, nS) = decoder_layer(xp, c_prompt, pos_p, lp, None)
        for g in range(N_GROUPS):
            pk[g].append(nk[g])
            pv[g].append(nv[g])
        ps.append(nS)
        layer_cache = ([ck[l] for ck in cache_k], [cv[l] for cv in cache_v], state_gla[l])
        xs, (nk, nv, nS) = decoder_layer(xs, c_sample, pos_s, lp, layer_cache)
        for g in range(N_GROUPS):
            sk[g].append(nk[g])
            sv[g].append(nv[g])
        ss.append(nS)
    return (xp, xs,
            jnp.stack(pk[0]), jnp.stack(pv[0]), jnp.stack(pk[1]), jnp.stack(pv[1]),
            jnp.stack(pk[2]), jnp.stack(pv[2]), jnp.stack(ps),
            jnp.stack(sk[0]), jnp.stack(sv[0]), jnp.stack(sk[1]), jnp.stack(sv[1]),
            jnp.stack(sk[2]), jnp.stack(sv[2]), jnp.stack(ss))
```

```python
import functools

import jax
import jax.numpy as jnp
import numpy as np
from jax import lax
from jax.experimental import pallas as pl
from jax.experimental.pallas import tpu as pltpu

F32 = jnp.float32
BF16 = jnp.bfloat16
I32 = jnp.int32

DIL_GROUPS = ((128, 1), (512, 4), (2048, 16))
N_GROUPS = 3
A_HEADS = 8
A_HEAD_DIM = 128
A_WIDTH = A_HEADS * A_HEAD_DIM
SPAN = 128
ROPE_DIM = A_HEAD_DIM // 4
ROPE_THETA = 500000.0
GLA_HEADS = 4
GLA_RANK = 16
GLA_TAU = 16.0
GLA_CHUNK = 64
N_EXPERTS = 64
TOP_K = 6
N_EXPERT_GROUPS = 8
TOPK_GROUPS = 4
ROUTED_SCALE = 2.5
EPS = 1e-6
NEG_INF = -1e30
PAST_LEN = 8192

LANE = 128
TOPK_PAD = 8
MOE_ROWS = 256
VMEM_LIMIT = 56 * 1024 * 1024


def _cp(*sem):
    return pltpu.CompilerParams(dimension_semantics=sem, vmem_limit_bytes=VMEM_LIMIT)


def _silu(x):
    return x * jax.nn.sigmoid(x)


def _tile(n, pref):
    if n <= pref:
        return n
    t = pref
    while n % t:
        t //= 2
    return t


def _ada_kernel(c_ref, w_ref, b_ref, o_ref):
    s = _silu(c_ref[...]).astype(BF16)
    o_ref[...] = jnp.dot(s, w_ref[...].astype(BF16), preferred_element_type=F32) + b_ref[...]


def ada_mod(c, w, b):
    g, d = c.shape
    n = w.shape[1]
    tn = _tile(n, 512)
    return pl.pallas_call(
        _ada_kernel,
        out_shape=jax.ShapeDtypeStruct((g, n), F32),
        grid=(n // tn,),
        in_specs=[pl.BlockSpec((g, d), lambda j: (0, 0)),
                  pl.BlockSpec((d, tn), lambda j: (0, j)),
                  pl.BlockSpec((1, tn), lambda j: (0, j))],
        out_specs=pl.BlockSpec((g, tn), lambda j: (0, j)),
        compiler_params=_cp("arbitrary"),
        name="ada_mod",
    )(c, w, b.reshape(1, n))


class Stream:
    def __init__(self, rows, tm, mod, rows_per_group, pos):
        self.rows, self.tm, self.mod, self.rpg, self.pos = rows, tm, mod, rows_per_group, pos
        self.d = mod.shape[-1] // 6
        self.r = mod.shape[1]

    def mod_spec(self, comp, tm, tn=None, col_of=None):
        tn = self.d if tn is None else tn
        nj = self.d // tn
        rpg, r = self.rpg, self.r

        def idx(*g):
            i, j = col_of(*g)
            grp = (i * tm) // rpg if r == 1 else 0
            return (grp, 0, comp * nj + j)
        return pl.BlockSpec((None, r, tn), idx)


def _norm_mod(x, g, sc, sh):
    y = x * lax.rsqrt(jnp.mean(x * x, axis=-1, keepdims=True) + EPS) * g
    return y * (1.0 + sc) + sh


def _norm_mod_kernel(x_ref, g_ref, sc_ref, sh_ref, o_ref):
    o_ref[...] = _norm_mod(x_ref[...], g_ref[...], sc_ref[...], sh_ref[...]).astype(o_ref.dtype)


def norm_mod(x, gain, st, comp_shift, comp_scale):
    m, d = x.shape
    tm = _tile(m, 512) if st.r == 1 else m
    col = lambda i: (i, 0)
    return pl.pallas_call(
        _norm_mod_kernel,
        out_shape=jax.ShapeDtypeStruct((m, d), BF16),
        grid=(m // tm,),
        in_specs=[pl.BlockSpec((tm, d), lambda i: (i, 0)),
                  pl.BlockSpec((1, d), lambda i: (0, 0)),
                  st.mod_spec(comp_scale, tm, col_of=col),
                  st.mod_spec(comp_shift, tm, col_of=col)],
        out_specs=pl.BlockSpec((tm, d), lambda i: (i, 0)),
        compiler_params=_cp("parallel"),
        name="norm_mod",
    )(x, gain.reshape(1, d), st.mod, st.mod)


def _mm_kernel(*refs, n_extra, epilogue, cast_w):
    x_ref, w_ref = refs[0], refs[1]
    extras = refs[2:2 + n_extra]
    o_ref = refs[2 + n_extra]
    if cast_w:
        wbf = refs[3 + n_extra]

        @pl.when(pl.program_id(1) == 0)
        def _():
            wbf[...] = w_ref[...].astype(BF16)
        w = wbf[...]
    else:
        w = w_ref[...]
    acc = jnp.dot(x_ref[...], w, preferred_element_type=F32)
    if epilogue is not None:
        acc = epilogue(acc, *[e[...] for e in extras])
    o_ref[...] = acc.astype(o_ref.dtype)


def matmul(x, w, n_cols, *, tm, tn, out_dtype, extra=(), extra_specs=(), epilogue=None, name="mm"):
    m, k = x.shape
    cast_w = w.dtype != BF16
    scratch = [pltpu.VMEM((k, tn), BF16)] if cast_w else []
    return pl.pallas_call(
        functools.partial(_mm_kernel, n_extra=len(extra), epilogue=epilogue, cast_w=cast_w),
        out_shape=jax.ShapeDtypeStruct((m, n_cols), out_dtype),
        grid=(n_cols // tn, m // tm),
        in_specs=[pl.BlockSpec((tm, k), lambda j, i: (i, 0)),
                  pl.BlockSpec((k, tn), lambda j, i: (0, j))] + list(extra_specs),
        out_specs=pl.BlockSpec((tm, tn), lambda j, i: (i, j)),
        scratch_shapes=scratch,
        compiler_params=_cp("parallel", "arbitrary"),
        name=name,
    )(x, w, *extra)


def _qk_prep_kernel(q_ref, k_ref, v_ref, qn_ref, kn_ref, c_ref, s1_ref, s2_ref,
                    qo_ref, ko_ref, kf_ref, vf_ref, *, n_heads):
    cos, s1, s2 = c_ref[...], s1_ref[...], s2_ref[...]
    half = ROPE_DIM // 2

    def prep(xh, g):
        xf = xh.astype(F32)
        y = xf * lax.rsqrt(jnp.mean(xf * xf, axis=-1, keepdims=True) + EPS) * g
        return (y * cos + pltpu.roll(y, half, 1) * s1
                + pltpu.roll(y, A_HEAD_DIM - half, 1) * s2)

    qn, kn = qn_ref[...], kn_ref[...]
    for h in range(n_heads):
        sl = slice(h * A_HEAD_DIM, (h + 1) * A_HEAD_DIM)
        qo_ref[:, sl] = prep(q_ref[:, sl], qn).astype(qo_ref.dtype)
        kr = prep(k_ref[:, sl], kn)
        ko_ref[:, sl] = kr.astype(ko_ref.dtype)
        kf_ref[:, sl] = kr
    vf_ref[...] = v_ref[...].astype(F32)


def rope_tables(pos):
    half = ROPE_DIM // 2
    inv = ROPE_THETA ** (-jnp.arange(half, dtype=F32) * 2.0 / ROPE_DIM)
    ang = pos.astype(F32)[:, None] * inv[None, :]
    cos, sin = jnp.cos(ang), jnp.sin(ang)
    t = pos.shape[0]
    rest = A_HEAD_DIM - ROPE_DIM
    c = jnp.concatenate([cos, cos, jnp.ones((t, rest), F32)], axis=1)
    s1 = jnp.concatenate([jnp.zeros((t, half), F32), sin, jnp.zeros((t, rest), F32)], axis=1)
    s2 = jnp.concatenate([-sin, jnp.zeros((t, half), F32), jnp.zeros((t, rest), F32)], axis=1)
    return c, s1, s2


def qk_prep(proj, q_norm, k_norm, tables):
    m = proj.shape[0]
    c, s1, s2 = tables
    t = c.shape[0]
    tm = _tile(t, 256)
    nt = t // tm
    qa_w = N_GROUPS * A_WIDTH
    n_heads = N_GROUPS * A_HEADS
    tab = pl.BlockSpec((tm, A_HEAD_DIM), lambda i: (i % nt, 0))
    gspec = pl.BlockSpec((1, A_HEAD_DIM), lambda i: (0, 0))
    blk = lambda j: pl.BlockSpec((tm, qa_w), lambda i: (i, j))
    return pl.pallas_call(
        functools.partial(_qk_prep_kernel, n_heads=n_heads),
        out_shape=(jax.ShapeDtypeStruct((m, qa_w), BF16), jax.ShapeDtypeStruct((m, qa_w), BF16),
                   jax.ShapeDtypeStruct((m, qa_w), F32), jax.ShapeDtypeStruct((m, qa_w), F32)),
        grid=(m // tm,),
        in_specs=[blk(0), blk(1), blk(2), gspec, gspec, tab, tab, tab],
        out_specs=(blk(0), blk(0), blk(0), blk(0)),
        compiler_params=_cp("parallel"),
        name="qk_prep",
    )(proj, proj, proj, q_norm.reshape(1, -1), k_norm.reshape(1, -1), c, s1, s2)


def _attn_prompt_kernel(q_ref, kp_ref, kc_ref, vp_ref, vc_ref, o_ref, lse_ref):
    first = pl.program_id(2) == 0
    qi = lax.broadcasted_iota(I32, (SPAN, 2 * SPAN), 0)
    ki = lax.broadcasted_iota(I32, (SPAN, 2 * SPAN), 1)
    mask = (ki >= qi) & (ki <= qi + SPAN) & (jnp.logical_not(first) | (ki >= SPAN))
    lane = lax.broadcasted_iota(I32, (SPAN, LANE), 1)
    lse_all = jnp.zeros((SPAN, LANE), F32)
    scale = A_HEAD_DIM ** -0.5
    for h in range(A_HEADS):
        sl = slice(h * A_HEAD_DIM, (h + 1) * A_HEAD_DIM)
        kh = jnp.concatenate([kp_ref[:, sl], kc_ref[:, sl]], axis=0)
        vh = jnp.concatenate([vp_ref[:, sl], vc_ref[:, sl]], axis=0)
        s = lax.dot_general(q_ref[:, sl], kh, (((1,), (1,)), ((), ())),
                            preferred_element_type=F32) * scale
        s = jnp.where(mask, s, NEG_INF)
        m = jnp.max(s, axis=-1, keepdims=True)
        p = jnp.exp(s - m)
        l = jnp.sum(p, axis=-1, keepdims=True)
        o = jnp.dot(p.astype(BF16), vh, preferred_element_type=F32) / l
        o_ref[:, sl] = o.astype(o_ref.dtype)
        lse_all = jnp.where(lane == h, m + jnp.log(l), lse_all)
    lse_ref[...] = lse_all


def attn_prompt(q_r, k_r, proj, g, batch, seq):
    _, dil = DIL_GROUPS[g]
    l = seq // dil
    nblk = l // SPAN
    qa_w = N_GROUPS * A_WIDTH
    w = proj.shape[1]
    qv = q_r.reshape(batch, l, dil * qa_w)
    kv = k_r.reshape(batch, l, dil * qa_w)
    pv = proj.reshape(batch, l, dil * w)
    nq = qa_w // A_WIDTH
    npj = w // A_WIDTH
    voff = 2 * qa_w // A_WIDTH
    cur = lambda b, r, i: i
    prev = lambda b, r, i: jnp.maximum(i - 1, 0)
    qspec = lambda rowf: pl.BlockSpec((None, SPAN, A_WIDTH), lambda b, r, i: (b, rowf(b, r, i), r * nq + g))
    vspec = lambda rowf: pl.BlockSpec((None, SPAN, A_WIDTH),
                                      lambda b, r, i: (b, rowf(b, r, i), r * npj + voff + g))
    o, lse = pl.pallas_call(
        _attn_prompt_kernel,
        out_shape=(jax.ShapeDtypeStruct((batch, l, dil * A_WIDTH), BF16),
                   jax.ShapeDtypeStruct((batch, l, dil * LANE), F32)),
        grid=(batch, dil, nblk),
        in_specs=[qspec(cur), qspec(prev), qspec(cur), vspec(prev), vspec(cur)],
        out_specs=(pl.BlockSpec((None, SPAN, A_WIDTH), lambda b, r, i: (b, i, r)),
                   pl.BlockSpec((None, SPAN, LANE), lambda b, r, i: (b, i, r))),
        compiler_params=_cp("parallel", "parallel", "arbitrary"),
        name=f"attn_prompt_g{g}",
    )(qv, kv, kv, pv, pv)
    return o.reshape(batch * seq, A_WIDTH), lse.reshape(batch * seq, LANE)


def _attn_sample_kernel(q_ref, k_ref, v_ref, ck0, cv0, ck1, cv1, ck2, cv2,
                        o0, o1, o2, l0, l1, l2, *, t_new, t_pad):
    caches = ((ck0, cv0), (ck1, cv1), (ck2, cv2))
    outs = ((o0, l0), (o1, l1), (o2, l2))
    scale = A_HEAD_DIM ** -0.5
    row = lax.broadcasted_iota(I32, (t_pad, SPAN), 0)
    col = lax.broadcasted_iota(I32, (t_pad, SPAN), 1)
    rowc = lax.broadcasted_iota(I32, (t_pad, 1), 0)
    lane = lax.broadcasted_iota(I32, (t_pad, LANE), 1)
    nt = (((1,), (1,)), ((), ()))
    for g, (_, dil) in enumerate(DIL_GROUPS):
        ck, cv = caches[g]
        o_ref, lse_ref = outs[g]
        lse_all = jnp.zeros((t_pad, LANE), F32)
        for h in range(A_HEADS):
            base = g * A_WIDTH + h * A_HEAD_DIM
            qh = q_ref[:, base:base + A_HEAD_DIM].astype(F32)
            kn = k_ref[:, base:base + A_HEAD_DIM].astype(F32)
            vn = v_ref[:, base:base + A_HEAD_DIM].astype(F32)
            qb = qh.astype(BF16)
            hs = slice(h * A_HEAD_DIM, (h + 1) * A_HEAD_DIM)
            if dil == 1:
                s_c = lax.dot_general(qb, ck[:, hs].astype(BF16), nt, preferred_element_type=F32)
                s_c = jnp.where(col >= row, s_c * scale, NEG_INF)
            else:
                s_c = jnp.zeros((t_pad, SPAN), F32)
                for t in range(t_new):
                    off = t * A_WIDTH + h * A_HEAD_DIM
                    kt = ck[:, off:off + A_HEAD_DIM].astype(BF16)
                    s_t = lax.dot_general(qb, kt, nt, preferred_element_type=F32)
                    s_c = jnp.where(row == t, s_t * scale, s_c)
            s_n, valid = [], []
            for u in range(t_new):
                s_n.append(jnp.sum(qh * kn[u:u + 1, :], axis=-1, keepdims=True) * scale)
                valid.append((rowc >= u) if dil == 1 else (rowc == u))
            m = jnp.max(s_c, axis=-1, keepdims=True)
            for u in range(t_new):
                m = jnp.maximum(m, jnp.where(valid[u], s_n[u], NEG_INF))
            p_c = jnp.exp(s_c - m)
            lsum = jnp.sum(p_c, axis=-1, keepdims=True)
            pb = p_c.astype(BF16)
            if dil == 1:
                o = jnp.dot(pb, cv[:, hs].astype(BF16), preferred_element_type=F32)
            else:
                o = jnp.zeros((t_pad, A_HEAD_DIM), F32)
                for t in range(t_new):
                    off = t * A_WIDTH + h * A_HEAD_DIM
                    o_t = jnp.dot(pb, cv[:, off:off + A_HEAD_DIM].astype(BF16),
                                  preferred_element_type=F32)
                    o = jnp.where(rowc == t, o_t, o)
            for u in range(t_new):
                p_u = jnp.where(valid[u], jnp.exp(s_n[u] - m), 0.0)
                lsum = lsum + p_u
                o = o + p_u * vn[u:u + 1, :]
            o_ref[:, hs] = (o / lsum).astype(o_ref.dtype)
            lse_all = jnp.where(lane == h, m + jnp.log(lsum), lse_all)
        lse_ref[...] = lse_all


def attn_sample(q8, k8, v8, caches_k, caches_v, layer, t_new):
    db, t_pad, qa_w = q8.shape
    ins, specs = [q8, k8, v8], [pl.BlockSpec((None, t_pad, qa_w), lambda b: (b, 0, 0))] * 3
    for g, (win, dil) in enumerate(DIL_GROUPS):
        for c in (caches_k[g], caches_v[g]):
            depth, _, wc, hh, ee = c.shape
            assert wc == win and hh * ee == A_WIDTH and (dil == 1 or t_new <= dil)
            nres = min(dil, t_new)
            ins.append(c.reshape(depth, db, wc // dil, dil * A_WIDTH))
            specs.append(pl.BlockSpec((None, None, SPAN, nres * A_WIDTH),
                                      lambda b: (layer, b, 0, 0)))
    ospec = pl.BlockSpec((None, t_pad, A_WIDTH), lambda b: (b, 0, 0))
    lspec = pl.BlockSpec((None, t_pad, LANE), lambda b: (b, 0, 0))
    res = pl.pallas_call(
        functools.partial(_attn_sample_kernel, t_new=t_new, t_pad=t_pad),
        out_shape=tuple([jax.ShapeDtypeStruct((db, t_pad, A_WIDTH), BF16)] * 3
                        + [jax.ShapeDtypeStruct((db, t_pad, LANE), F32)] * 3),
        grid=(db,),
        in_specs=specs,
        out_specs=tuple([ospec] * 3 + [lspec] * 3),
        compiler_params=_cp("parallel"),
        name="attn_sample",
    )(*ins)
    return res[:3], res[3:]


def _combine_kernel(o0, o1, o2, l0, l1, l2, out_ref):
    ls = [l0[...], l1[...], l2[...]]
    m = jnp.maximum(jnp.maximum(ls[0], ls[1]), ls[2])
    es = [jnp.exp(x - m) for x in ls]
    den = es[0] + es[1] + es[2]
    r = lax.broadcasted_iota(I32, (LANE, A_WIDTH), 0)
    c = lax.broadcasted_iota(I32, (LANE, A_WIDTH), 1)
    expand = (c // A_HEAD_DIM == r).astype(BF16)
    acc = None
    for e, o in zip(es, (o0, o1, o2)):
        w = e / den
        hi = w.astype(BF16)
        lo = (w - hi.astype(F32)).astype(BF16)
        wx = (jnp.dot(hi, expand, preferred_element_type=F32)
              + jnp.dot(lo, expand, preferred_element_type=F32))
        term = wx * o[...].astype(F32)
        acc = term if acc is None else acc + term
    out_ref[...] = acc.astype(out_ref.dtype)


def combine_groups(os_, ls_):
    m = os_[0].shape[0]
    tm = _tile(m, 512)
    ospec = pl.BlockSpec((tm, A_WIDTH), lambda i: (i, 0))
    lspec = pl.BlockSpec((tm, LANE), lambda i: (i, 0))
    return pl.pallas_call(
        _combine_kernel,
        out_shape=jax.ShapeDtypeStruct((m, A_WIDTH), BF16),
        grid=(m // tm,),
        in_specs=[ospec] * 3 + [lspec] * 3,
        out_specs=ospec,
        compiler_params=_cp("parallel"),
        name="combine_groups",
    )(*os_, *ls_)


def _alpha_kernel(h_ref, w1_ref, w2_ref, b_ref, o_ref):
    lr = jnp.dot(h_ref[...], w1_ref[...], preferred_element_type=F32)
    z = jnp.dot(lr.astype(BF16), w2_ref[...], preferred_element_type=F32) + b_ref[...]
    log_sig = jnp.minimum(z, 0.0) - jnp.log(1.0 + jnp.exp(-jnp.abs(z)))
    o_ref[...] = log_sig / GLA_TAU


def gla_log_decay(h, w_lr, alpha_up, alpha_bias):
    m, d = h.shape
    gq = alpha_up.shape[1]
    w1 = jnp.pad(w_lr, ((0, 0), (0, LANE - GLA_RANK))).astype(BF16)
    w2 = jnp.pad(alpha_up, ((0, LANE - GLA_RANK), (0, 0))).astype(BF16)
    tm = _tile(m, 512)
    return pl.pallas_call(
        _alpha_kernel,
        out_shape=jax.ShapeDtypeStruct((m, gq), F32),
        grid=(m // tm,),
        in_specs=[pl.BlockSpec((tm, d), lambda i: (i, 0)),
                  pl.BlockSpec((d, LANE), lambda i: (0, 0)),
                  pl.BlockSpec((LANE, gq), lambda i: (0, 0)),
                  pl.BlockSpec((1, gq), lambda i: (0, 0))],
        out_specs=pl.BlockSpec((tm, gq), lambda i: (i, 0)),
        compiler_params=_cp("parallel"),
        name="gla_log_decay",
    )(h, w1, w2, alpha_bias.reshape(1, gq))


def _gla_kernel(*refs, chunk, dk, has_s0):
    if has_s0:
        q_ref, k_ref, v_ref, la_ref, rb_ref, gn_ref, s0_ref, ob_ref, s_ref = refs
    else:
        q_ref, k_ref, v_ref, la_ref, rb_ref, gn_ref, ob_ref, s_ref = refs

    @pl.when(pl.program_id(2) == 0)
    def _():
        s_ref[...] = s0_ref[...].astype(F32) if has_s0 else jnp.zeros(s_ref.shape, F32)

    la = la_ref[...]
    ri = lax.broadcasted_iota(I32, (chunk, chunk), 0)
    ci = lax.broadcasted_iota(I32, (chunk, chunk), 1)
    causal = ri >= ci
    b = jnp.dot(causal.astype(F32), la, precision=lax.Precision.HIGHEST,
                preferred_element_type=F32)
    b_last = b[chunk - 1:chunk, :]
    q = q_ref[...].astype(F32) * (dk ** -0.5)
    k = k_ref[...].astype(F32)
    v = v_ref[...]
    q_dec = (q * jnp.exp(b)).astype(BF16)
    k_inv = (k * jnp.exp(-b)).astype(BF16)
    k_dec = (k * jnp.exp(b_last - b)).astype(BF16)
    att = lax.dot_general(q_dec, k_inv, (((1,), (1,)), ((), ())), preferred_element_type=F32)
    att = jnp.where(causal, att, 0.0).astype(BF16)
    s_old = s_ref[...]
    o = (jnp.dot(att, v, preferred_element_type=F32)
         + jnp.dot(q_dec, s_old.astype(BF16), preferred_element_type=F32))
    kv = lax.dot_general(k_dec, v, (((0,), (0,)), ((), ())), preferred_element_type=F32)
    decay = jnp.sum(jnp.transpose(la), axis=1, keepdims=True)
    s_ref[...] = jnp.exp(decay) * s_old + kv
    y = o * lax.rsqrt(jnp.mean(o * o, axis=-1, keepdims=True) + EPS) * gn_ref[...]
    ob_ref[...] = (y * _silu(rb_ref[...].astype(F32))).astype(ob_ref.dtype)


def gla(proj3, la3, gla_norm, s0, layer, dk, dv):
    bsz, t, w = proj3.shape
    chunk = GLA_CHUNK
    qa_w = N_GROUPS * A_WIDTH
    gq, gv = GLA_HEADS * dk, GLA_HEADS * dv
    qoff, koff = 3 * qa_w // dk, (3 * qa_w + gq) // dk
    voff, roff = (3 * qa_w + 2 * gq) // dv, (3 * qa_w + 2 * gq + gv) // dv
    assert (3 * qa_w) % dk == 0 and (3 * qa_w + 2 * gq) % dv == 0
    cspec = lambda width, off: pl.BlockSpec((None, chunk, width), lambda b, h, c: (b, c, off + h))
    ins = [proj3, proj3, proj3, la3, proj3, gla_norm.reshape(1, dv)]
    specs = [cspec(dk, qoff), cspec(dk, koff), cspec(dv, voff), cspec(dk, 0), cspec(dv, roff),
             pl.BlockSpec((1, dv), lambda b, h, c: (0, 0))]
    if s0 is not None:
        ins.append(s0)
        specs.append(pl.BlockSpec((None, None, None, dk, dv), lambda b, h, c: (layer, b, h, 0, 0)))
    return pl.pallas_call(
        functools.partial(_gla_kernel, chunk=chunk, dk=dk, has_s0=s0 is not None),
        out_shape=(jax.ShapeDtypeStruct((bsz, t, gv), BF16),
                   jax.ShapeDtypeStruct((bsz, GLA_HEADS, dk, dv), F32)),
        grid=(bsz, GLA_HEADS, t // chunk),
        in_specs=specs,
        out_specs=(pl.BlockSpec((None, chunk, dv), lambda b, h, c: (b, c, h)),
                   pl.BlockSpec((None, None, dk, dv), lambda b, h, c: (b, h, 0, 0))),
        compiler_params=_cp("parallel", "parallel", "arbitrary"),
        name="gla",
    )(*ins)


def _merge_kernel(h_ref, oa_ref, ob_ref, wga_ref, wgb_ref, wa_ref, wb_ref, o_ref):
    h = h_ref[...]
    ga = jnp.dot(h, wga_ref[...], preferred_element_type=F32)
    gb = jnp.dot(h, wgb_ref[...], preferred_element_type=F32)
    a = jnp.dot(oa_ref[...], wa_ref[...], preferred_element_type=F32)
    b = jnp.dot(ob_ref[...], wb_ref[...], preferred_element_type=F32)
    o_ref[...] = (jax.nn.sigmoid(ga) * a + jax.nn.sigmoid(gb) * b).astype(o_ref.dtype)


def merge_branches(h, oa, ob, w_gates, w_a, w_b, tm):
    m, d = h.shape
    tn = _tile(d, 512)
    nj = d // tn
    row = lambda width: pl.BlockSpec((tm, width), lambda j, i: (i, 0))
    wcol = lambda rows, off: pl.BlockSpec((rows, tn), lambda j, i: (0, j + off))
    return pl.pallas_call(
        _merge_kernel,
        out_shape=jax.ShapeDtypeStruct((m, d), BF16),
        grid=(nj, m // tm),
        in_specs=[row(d), row(oa.shape[1]), row(ob.shape[1]),
                  wcol(d, 0), wcol(d, nj), wcol(w_a.shape[0], 0), wcol(w_b.shape[0], 0)],
        out_specs=pl.BlockSpec((tm, tn), lambda j, i: (i, j)),
        compiler_params=_cp("parallel", "arbitrary"),
        name="merge_branches",
    )(h, oa, ob, w_gates, w_gates, w_a, w_b)


def _norm_router_kernel(x_ref, g_ref, sc_ref, sh_ref, wr_ref, rb_ref, h_ref, idx_ref, wt_ref):
    h = _norm_mod(x_ref[...], g_ref[...], sc_ref[...], sh_ref[...])
    h_ref[...] = h.astype(h_ref.dtype)
    tm = h.shape[0]
    per = N_EXPERTS // N_EXPERT_GROUPS
    logits = lax.dot_general(wr_ref[...], h, (((1,), (1,)), ((), ())),
                             precision=lax.Precision.HIGHEST, preferred_element_type=F32)
    scores = jax.nn.sigmoid(logits)
    biased = scores + rb_ref[...]
    bg = biased.reshape(N_EXPERT_GROUPS, per, tm)
    ip = lax.broadcasted_iota(I32, bg.shape, 1)
    m1 = jnp.max(bg, axis=1, keepdims=True)
    i1 = jnp.min(jnp.where(bg == m1, ip, per), axis=1, keepdims=True)
    m2 = jnp.max(jnp.where(ip == i1, -jnp.inf, bg), axis=1, keepdims=True)
    grp = (m1 + m2)[:, 0, :]
    gi = lax.broadcasted_iota(I32, grp.shape, 0)
    rank = jnp.zeros(grp.shape, I32)
    for g in range(N_EXPERT_GROUPS):
        other = grp[g:g + 1, :]
        ahead = (other > grp) | ((other == grp) & (g < gi))
        rank = rank + ahead.astype(I32)
    sel = (rank < TOPK_GROUPS)[:, None, :]
    masked = jnp.where(sel, bg, -jnp.inf).reshape(N_EXPERTS, tm)
    ie = lax.broadcasted_iota(I32, masked.shape, 0)
    ids, ws = [], []
    for _ in range(TOP_K):
        mx = jnp.max(masked, axis=0, keepdims=True)
        ik = jnp.min(jnp.where(masked == mx, ie, N_EXPERTS), axis=0, keepdims=True)
        hit = ie == ik
        ids.append(ik)
        ws.append(jnp.sum(jnp.where(hit, scores, 0.0), axis=0, keepdims=True))
        masked = jnp.where(hit, -jnp.inf, masked)
    tot = ws[0]
    for wk in ws[1:]:
        tot = tot + wk
    pad = TOPK_PAD - TOP_K
    idx_ref[...] = jnp.concatenate(ids + [jnp.zeros((pad, tm), I32)], axis=0)
    wt_ref[...] = jnp.concatenate([wk / tot * ROUTED_SCALE for wk in ws]
                                  + [jnp.zeros((pad, tm), F32)], axis=0)


def norm_router(x, gain, st, comp_shift, comp_scale, w_router_t, router_bias):
    m, d = x.shape
    tm = _tile(m, 256) if st.r == 1 else m
    col = lambda i: (i, 0)
    e = w_router_t.shape[0]
    return pl.pallas_call(
        _norm_router_kernel,
        out_shape=(jax.ShapeDtypeStruct((m, d), F32),
                   jax.ShapeDtypeStruct((TOPK_PAD, m), I32),
                   jax.ShapeDtypeStruct((TOPK_PAD, m), F32)),
        grid=(m // tm,),
        in_specs=[pl.BlockSpec((tm, d), lambda i: (i, 0)),
                  pl.BlockSpec((1, d), lambda i: (0, 0)),
                  st.mod_spec(comp_scale, tm, col_of=col),
                  st.mod_spec(comp_shift, tm, col_of=col),
                  pl.BlockSpec((e, d), lambda i: (0, 0)),
                  pl.BlockSpec((e, 1), lambda i: (0, 0))],
        out_specs=(pl.BlockSpec((tm, d), lambda i: (i, 0)),
                   pl.BlockSpec((TOPK_PAD, tm), lambda i: (0, i)),
                   pl.BlockSpec((TOPK_PAD, tm), lambda i: (0, i))),
        compiler_params=_cp("parallel"),
        name="norm_router",
    )(x, gain.reshape(1, d), st.mod, st.mod, w_router_t, router_bias.reshape(e, 1))


def _row_copy(src_hbm, row, dst, sem):
    return pltpu.make_async_copy(src_hbm.at[row], dst, sem)


def _rows_to_matrix(load_chunk, n_chunks, dtype):
    return jnp.concatenate([load_chunk(c).astype(dtype) for c in range(n_chunks)], axis=1)


def _moe_kernel(be_ref, nb_ref, tok_cur, tok_nxt, h_hbm, wg_ref, wu_ref, wd_ref, y_ref, xbuf, sem):
    i = pl.program_id(0)
    nb = nb_ref[0]
    slot = i % 2
    rows, n_chunks = xbuf.shape[1], xbuf.shape[2]

    def issue(tok_ref, s):
        def body(r, carry):
            _row_copy(h_hbm, tok_ref[0, 0, r], xbuf.at[s, r], sem.at[s]).start()
            return carry
        lax.fori_loop(0, rows, body, 0)

    @pl.when(i == 0)
    def _():
        issue(tok_cur, 0)

    @pl.when(i + 1 < nb)
    def _():
        issue(tok_nxt, 1 - slot)

    @pl.when(i < nb)
    def _():
        def wait(r, carry):
            _row_copy(h_hbm, 0, xbuf.at[slot, r], sem.at[slot]).wait()
            return carry
        lax.fori_loop(0, rows, wait, 0)
        x = _rows_to_matrix(lambda c: xbuf[slot, :, c, :], n_chunks, BF16)
        g = jnp.dot(x, wg_ref[...], preferred_element_type=F32)
        u = jnp.dot(x, wu_ref[...], preferred_element_type=F32)
        hb = (_silu(g) * u).astype(BF16)
        y = jnp.dot(hb, wd_ref[...], preferred_element_type=F32)
        for c in range(n_chunks):
            y_ref[:, c, :] = y[:, c * LANE:(c + 1) * LANE]

    @pl.when(i >= nb)
    def _():
        y_ref[...] = jnp.zeros(y_ref.shape, y_ref.dtype)


def moe_experts(h_rows, row_tok, block_e, nb_used, wg, wu, wd, layer):
    n_blocks = block_e.shape[0]
    n_chunks = h_rows.shape[1]
    d = n_chunks * LANE
    f = wg.shape[-1]
    tok3 = row_tok.reshape(n_blocks, 1, MOE_ROWS)
    smem = lambda f_: pl.BlockSpec((1, 1, MOE_ROWS), f_, memory_space=pltpu.SMEM)
    last = n_blocks - 1
    grid_spec = pltpu.PrefetchScalarGridSpec(
        num_scalar_prefetch=2,
        grid=(n_blocks,),
        in_specs=[smem(lambda i, be, nb: (i, 0, 0)),
                  smem(lambda i, be, nb: (jnp.minimum(i + 1, last), 0, 0)),
                  pl.BlockSpec(memory_space=pl.ANY),
                  pl.BlockSpec((None, None, d, f), lambda i, be, nb: (layer, be[i], 0, 0)),
                  pl.BlockSpec((None, None, d, f), lambda i, be, nb: (layer, be[i], 0, 0)),
                  pl.BlockSpec((None, None, f, d), lambda i, be, nb: (layer, be[i], 0, 0))],
        out_specs=pl.BlockSpec((MOE_ROWS, n_chunks, LANE), lambda i, be, nb: (i, 0, 0)),
        scratch_shapes=[pltpu.VMEM((2, MOE_ROWS, n_chunks, LANE), F32),
                        pltpu.SemaphoreType.DMA((2,))],
    )
    return pl.pallas_call(
        _moe_kernel,
        out_shape=jax.ShapeDtypeStruct((n_blocks * MOE_ROWS, n_chunks, LANE), F32),
        grid_spec=grid_spec,
        compiler_params=_cp("arbitrary"),
        name="moe_experts",
    )(block_e, nb_used, tok3, tok3, h_rows, wg, wu, wd)


def moe_dispatch(idx, n_tokens):
    nk = n_tokens * TOP_K
    flat_e = idx.reshape(-1)
    order = jnp.argsort(flat_e)
    e_s = flat_e[order]
    t_s = (order // TOP_K).astype(I32)
    counts = jnp.bincount(flat_e, length=N_EXPERTS)
    start = jnp.cumsum(counts) - counts
    padded = (counts + MOE_ROWS - 1) // MOE_ROWS * MOE_ROWS
    pend = jnp.cumsum(padded)
    dest_sorted = ((pend - padded)[e_s] + jnp.arange(nk) - start[e_s]).astype(I32)
    n_blocks = -(-nk // MOE_ROWS) + N_EXPERTS
    row_tok = jnp.zeros((n_blocks * MOE_ROWS,), I32).at[dest_sorted].set(t_s)
    block_e = jnp.minimum(
        jnp.searchsorted(pend, jnp.arange(n_blocks) * MOE_ROWS, side='right'),
        N_EXPERTS - 1).astype(I32)
    nb_used = (pend[-1] // MOE_ROWS).astype(I32).reshape(1)
    dest = jnp.zeros((nk,), I32).at[order].set(dest_sorted).reshape(n_tokens, TOP_K)
    dest = jnp.pad(dest, ((0, 0), (0, TOPK_PAD - TOP_K)))
    return row_tok, block_e, nb_used, dest


def _finish_kernel(dest_cur, dest_nxt, h_ref, wt_ref, x_ref, g2_ref, wg_ref, wu_ref, wd_ref,
                   y_hbm, o_ref, gbuf, sem):
    i = pl.program_id(0)
    n = pl.num_programs(0)
    slot = i % 2
    tm = h_ref.shape[0]
    n_chunks = gbuf.shape[3]

    def issue(dest_ref, s):
        def body(j, carry):
            for k in range(TOP_K):
                _row_copy(y_hbm, dest_ref[0, 0, j * TOPK_PAD + k],
                          gbuf.at[s, k, j], sem.at[s]).start()
            return carry
        lax.fori_loop(0, tm, body, 0)

    @pl.when(i == 0)
    def _():
        issue(dest_cur, 0)

    @pl.when(i + 1 < n)
    def _():
        issue(dest_nxt, 1 - slot)

    h = h_ref[...].astype(BF16)
    g = jnp.dot(h, wg_ref[...], preferred_element_type=F32)
    u = jnp.dot(h, wu_ref[...], preferred_element_type=F32)
    acc = jnp.dot((_silu(g) * u).astype(BF16), wd_ref[...], preferred_element_type=F32)

    def wait(j, carry):
        for k in range(TOP_K):
            _row_copy(y_hbm, 0, gbuf.at[slot, k, j], sem.at[slot]).wait()
        return carry
    lax.fori_loop(0, tm, wait, 0)
    wt = wt_ref[...]
    for k in range(TOP_K):
        yk = _rows_to_matrix(lambda c: gbuf[slot, k, :, c, :], n_chunks, F32)
        acc = acc + wt[:, k:k + 1] * yk
    o_ref[...] = x_ref[...] + g2_ref[...] * acc


def moe_finish(x1, st, h_all, wts, dest, ybuf, wgs, wus, wds, row_off):
    m, d = x1.shape
    f = wgs.shape[1]
    tm = _tile(m, 128)
    nt = m // tm
    off = row_off // tm
    assert row_off % tm == 0
    n_all = h_all.shape[0]
    dest3 = dest.reshape(n_all // tm, 1, tm * TOPK_PAD)
    smem = lambda f_: pl.BlockSpec((1, 1, tm * TOPK_PAD), f_, memory_space=pltpu.SMEM)
    whole = lambda a, b: pl.BlockSpec((a, b), lambda i: (0, 0), pipeline_mode=pl.Buffered(1))
    return pl.pallas_call(
        _finish_kernel,
        out_shape=jax.ShapeDtypeStruct((m, d), F32),
        grid=(nt,),
        in_specs=[smem(lambda i: (i + off, 0, 0)),
                  smem(lambda i: (jnp.minimum(i + 1, nt - 1) + off, 0, 0)),
                  pl.BlockSpec((tm, d), lambda i: (i + off, 0)),
                  pl.BlockSpec((tm, TOPK_PAD), lambda i: (i + off, 0)),
                  pl.BlockSpec((tm, d), lambda i: (i, 0)),
                  st.mod_spec(5, tm, col_of=lambda i: (i, 0)),
                  whole(d, f), whole(d, f), whole(f, d),
                  pl.BlockSpec(memory_space=pl.ANY)],
        out_specs=pl.BlockSpec((tm, d), lambda i: (i, 0)),
        scratch_shapes=[pltpu.VMEM((2, TOP_K, tm, ybuf.shape[1], LANE), F32),
                        pltpu.SemaphoreType.DMA((2,))],
        compiler_params=_cp("arbitrary"),
        name="moe_finish",
    )(dest3, dest3, h_all, wts, x1, st.mod, wgs, wus, wds, ybuf)


def _residual_epilogue(acc, x, gate):
    return x + gate * acc


def decoder_layer(l, xp, xs, c_all, p, caches_k, caches_v, state_gla, dims):
    batch, seq, db, t_new = dims
    d = xp.shape[1]
    mp, ms = xp.shape[0], xs.shape[0]
    qa_w = N_GROUPS * A_WIDTH
    dk, dv = d // 2 // GLA_HEADS, d // GLA_HEADS
    gq, gv = GLA_HEADS * dk, GLA_HEADS * dv
    n_main = 3 * qa_w + 2 * gq + 2 * gv
    t_pad = 8

    mod = ada_mod(c_all, p['w_ada'][l], p['b_ada'][l])
    st_p = Stream(mp, _tile(mp, 1024), mod[:batch].reshape(batch, 1, 6 * d), seq,
                  jnp.arange(seq))
    st_s = Stream(ms, ms, jnp.repeat(mod[batch:], t_new, axis=0).reshape(1, ms, 6 * d), 0,
                  jnp.tile(PAST_LEN + jnp.arange(t_new), db))
    w_in = p['w_in'][l]
    w_lr = w_in[:, n_main:n_main + GLA_RANK]
    w_gates = w_in[:, n_main + GLA_RANK:].astype(BF16)
    w_a, w_b = p['w_branch_a'][l].astype(BF16), p['w_branch_b'][l].astype(BF16)
    w_out = p['w_out'][l].astype(BF16)

    new_k, new_v, new_s, x1s, h2s, idxs, wtss = {}, {}, {}, {}, {}, [], []
    for name, st, x in (('p', st_p, xp), ('s', st_s, xs)):
        m, tm = st.rows, st.tm
        h1 = norm_mod(x, p['norm_mix'][l], st, 0, 1)
        proj = matmul(h1, w_in, n_main, tm=tm, tn=_tile(n_main, 1024), out_dtype=BF16,
                      name="in_proj")
        q_r, k_r, kf, vf = qk_prep(proj, p['q_norm'][l], p['k_norm'][l], rope_tables(st.pos))
        la = gla_log_decay(h1, w_lr, p['alpha_up'][l], p['alpha_bias'][l])
        if name == 'p':
            outs, lses = zip(*[attn_prompt(q_r, k_r, proj, g, batch, seq)
                               for g in range(N_GROUPS)])
            oa = combine_groups(outs, lses)
            ob, s_new = gla(proj.reshape(batch, seq, -1), la.reshape(batch, seq, gq),
                            p['gla_norm'][l], None, l, dk, dv)
            ob = ob.reshape(m, gv)
            kf4 = kf.reshape(batch, seq, N_GROUPS, A_HEADS, A_HEAD_DIM)
            vf4 = vf.reshape(batch, seq, N_GROUPS, A_HEADS, A_HEAD_DIM)
            new_k[name] = [kf4[:, seq - min(w_, seq):, g] for g, (w_, _) in enumerate(DIL_GROUPS)]
            new_v[name] = [vf4[:, seq - min(w_, seq):, g] for g, (w_, _) in enumerate(DIL_GROUPS)]
        else:
            pad8 = lambda a: jnp.pad(a.reshape(db, t_new, -1), ((0, 0), (0, t_pad - t_new), (0, 0)))
            v8 = pad8(proj[:, 2 * qa_w:3 * qa_w])
            outs, lses = attn_sample(pad8(q_r), pad8(k_r), v8, caches_k, caches_v, l, t_new)
            unpad = lambda a: a[:, :t_new].reshape(m, -1)
            oa = combine_groups([unpad(o) for o in outs], [unpad(x_) for x_ in lses])
            padc = lambda a: jnp.pad(a.reshape(db, t_new, -1),
                                     ((0, 0), (0, GLA_CHUNK - t_new), (0, 0)))
            ob, s_new = gla(padc(proj), padc(la), p['gla_norm'][l], state_gla, l, dk, dv)
            ob = ob[:, :t_new].reshape(m, gv)
            kf4 = kf.reshape(db, t_new, N_GROUPS, A_HEADS, A_HEAD_DIM)
            vf4 = vf.reshape(db, t_new, N_GROUPS, A_HEADS, A_HEAD_DIM)
            new_k[name] = [kf4[:, :, g] for g in range(N_GROUPS)]
            new_v[name] = [vf4[:, :, g] for g in range(N_GROUPS)]
        new_s[name] = s_new
        merged = merge_branches(h1, oa, ob, w_gates, w_a, w_b, tm)
        tn = _tile(d, 1024)
        x1 = matmul(merged, w_out, d, tm=tm, tn=tn, out_dtype=F32,
                    extra=(x, st.mod),
                    extra_specs=(pl.BlockSpec((tm, tn), lambda j, i: (i, j)),
                                 st.mod_spec(2, tm, tn, col_of=lambda j, i: (i, j))),
                    epilogue=_residual_epilogue, name="out_proj")
        h2, idx_t, wt_t = norm_router(x1, p['norm_ffn'][l], st, 3, 4,
                                      p['w_router'][l].T, p['router_bias'][l])
        x1s[name], h2s[name] = x1, h2
        idxs.append(idx_t[:TOP_K].T)
        wtss.append(wt_t.T)

    h_all = jnp.concatenate([h2s['p'], h2s['s']], axis=0)
    idx_all = jnp.concatenate(idxs, axis=0)
    wts_all = jnp.concatenate(wtss, axis=0)
    row_tok, block_e, nb_used, dest = moe_dispatch(idx_all, mp + ms)
    h_rows = h_all.reshape(mp + ms, d // LANE, LANE)
    ybuf = moe_experts(h_rows, row_tok, block_e, nb_used, p['w_gate_e_bf'], p['w_up_e_bf'],
                       p['w_down_e_bf'], l)
    wgs, wus, wds = (p['w_gate_s'][l].astype(BF16), p['w_up_s'][l].astype(BF16),
                     p['w_down_s'][l].astype(BF16))
    xp2 = moe_finish(x1s['p'], st_p, h_all, wts_all, dest, ybuf, wgs, wus, wds, 0)
    xs2 = moe_finish(x1s['s'], st_s, h_all, wts_all, dest, ybuf, wgs, wus, wds, mp)
    return xp2, xs2, new_k, new_v, new_s


def kernel(x_prompt, x_sample, cache_k_w128, cache_v_w128, cache_k_w512, cache_v_w512, cache_k_w2048, cache_v_w2048, state_gla, c_prompt, c_sample, w_in, q_norm, k_norm, alpha_up, alpha_bias, gla_norm, w_branch_a, w_branch_b, w_out, norm_mix, norm_ffn, w_ada, b_ada, w_router, router_bias, w_gate_e, w_up_e, w_down_e, w_gate_s, w_up_s, w_down_s):
    batch, seq, d = x_prompt.shape
    db, t_new, _ = x_sample.shape
    depth = w_in.shape[0]
    dims = (batch, seq, db, t_new)
    p = dict(w_in=w_in, q_norm=q_norm, k_norm=k_norm, alpha_up=alpha_up, alpha_bias=alpha_bias,
             gla_norm=gla_norm, w_branch_a=w_branch_a, w_branch_b=w_branch_b, w_out=w_out,
             norm_mix=norm_mix, norm_ffn=norm_ffn, w_ada=w_ada, b_ada=b_ada, w_router=w_router,
             router_bias=router_bias, w_gate_s=w_gate_s, w_up_s=w_up_s, w_down_s=w_down_s,
             w_gate_e_bf=w_gate_e.astype(BF16), w_up_e_bf=w_up_e.astype(BF16),
             w_down_e_bf=w_down_e.astype(BF16))
    caches_k = (cache_k_w128, cache_k_w512, cache_k_w2048)
    caches_v = (cache_v_w128, cache_v_w512, cache_v_w2048)
    c_all = jnp.concatenate([c_prompt, c_sample], axis=0)
    xp = x_prompt.reshape(batch * seq, d)
    xs = x_sample.reshape(db * t_new, d)
    ks, vs, ss = {'p': [], 's': []}, {'p': [], 's': []}, {'p': [], 's': []}
    for l in range(depth):
        xp, xs, nk, nv, ns = decoder_layer(l, xp, xs, c_all, p, caches_k, caches_v, state_gla, dims)
        for name in ('p', 's'):
            ks[name].append(nk[name])
            vs[name].append(nv[name])
            ss[name].append(ns[name])
    out = [xp.reshape(batch, seq, d), xs.reshape(db, t_new, d)]
    for name in ('p', 's'):
        for g in range(N_GROUPS):
            out.append(jnp.stack([k_[g] for k_ in ks[name]]))
            out.append(jnp.stack([v_[g] for v_ in vs[name]]))
        out.append(jnp.stack(ss[name]))
    return tuple(out)
```

```python
import functools

import jax
import jax.numpy as jnp
from jax import lax
from jax.experimental import pallas as pl
from jax.experimental.pallas import tpu as pltpu

F32 = jnp.float32
BF16 = jnp.bfloat16
I32 = jnp.int32
U32 = jnp.uint32

DIL_GROUPS = ((128, 1), (512, 4), (2048, 16))
N_GROUPS = 3
A_HEADS = 8
A_HEAD_DIM = 128
A_WIDTH = A_HEADS * A_HEAD_DIM
SPAN = 128
ROPE_DIM = A_HEAD_DIM // 4
ROPE_THETA = 500000.0
GLA_HEADS = 4
GLA_RANK = 16
GLA_TAU = 16.0
GLA_CHUNK = 64
N_EXPERTS = 64
TOP_K = 6
N_EXPERT_GROUPS = 8
TOPK_GROUPS = 4
ROUTED_SCALE = 2.5
EPS = 1e-6
NEG_INF = -1e30
PAST_LEN = 8192

LANE = 128
TOPK_PAD = 8
MOE_ROWS = 256
DMA_UNROLL = 8
VMEM_LIMIT = 56 * 1024 * 1024


def _cp(*sem):
    return pltpu.CompilerParams(dimension_semantics=sem, vmem_limit_bytes=VMEM_LIMIT)


def _silu(x):
    return x * jax.nn.sigmoid(x)


def _tile(n, pref):
    if n <= pref:
        return n
    t = pref
    while n % t:
        t //= 2
    return t


def _ada_kernel(c_ref, w_ref, b_ref, o_ref):
    s = _silu(c_ref[...]).astype(BF16)
    o_ref[...] = jnp.dot(s, w_ref[...].astype(BF16), preferred_element_type=F32) + b_ref[...]


def ada_mod(c, w, b):
    g, d = c.shape
    n = w.shape[1]
    tn = _tile(n, 512)
    return pl.pallas_call(
        _ada_kernel,
        out_shape=jax.ShapeDtypeStruct((g, n), F32),
        grid=(n // tn,),
        in_specs=[pl.BlockSpec((g, d), lambda j: (0, 0)),
                  pl.BlockSpec((d, tn), lambda j: (0, j)),
                  pl.BlockSpec((1, tn), lambda j: (0, j))],
        out_specs=pl.BlockSpec((g, tn), lambda j: (0, j)),
        compiler_params=_cp("arbitrary"),
        name="ada_mod",
    )(c, w, b.reshape(1, n))


class Stream:
    def __init__(self, rows, tm, mod, rows_per_group, pos):
        self.rows, self.tm, self.mod, self.rpg, self.pos = rows, tm, mod, rows_per_group, pos
        self.d = mod.shape[-1] // 6
        self.r = mod.shape[1]

    def mod_spec(self, comp, tm, tn=None, col_of=None):
        tn = self.d if tn is None else tn
        nj = self.d // tn
        rpg, r = self.rpg, self.r

        def idx(*g):
            i, j = col_of(*g)
            grp = (i * tm) // rpg if r == 1 else 0
            return (grp, 0, comp * nj + j)
        return pl.BlockSpec((None, r, tn), idx)


def _norm_mod(x, g, sc, sh):
    y = x * lax.rsqrt(jnp.mean(x * x, axis=-1, keepdims=True) + EPS) * g
    return y * (1.0 + sc) + sh


def _norm_mod_kernel(x_ref, g_ref, sc_ref, sh_ref, o_ref):
    o_ref[...] = _norm_mod(x_ref[...], g_ref[...], sc_ref[...], sh_ref[...]).astype(o_ref.dtype)


def norm_mod(x, gain, st, comp_shift, comp_scale):
    m, d = x.shape
    tm = _tile(m, 512) if st.r == 1 else m
    col = lambda i: (i, 0)
    return pl.pallas_call(
        _norm_mod_kernel,
        out_shape=jax.ShapeDtypeStruct((m, d), BF16),
        grid=(m // tm,),
        in_specs=[pl.BlockSpec((tm, d), lambda i: (i, 0)),
                  pl.BlockSpec((1, d), lambda i: (0, 0)),
                  st.mod_spec(comp_scale, tm, col_of=col),
                  st.mod_spec(comp_shift, tm, col_of=col)],
        out_specs=pl.BlockSpec((tm, d), lambda i: (i, 0)),
        compiler_params=_cp("parallel"),
        name="norm_mod",
    )(x, gain.reshape(1, d), st.mod, st.mod)


def _mm_kernel(*refs, n_extra, epilogue, cast_w):
    x_ref, w_ref = refs[0], refs[1]
    extras = refs[2:2 + n_extra]
    o_ref = refs[2 + n_extra]
    if cast_w:
        wbf = refs[3 + n_extra]

        @pl.when(pl.program_id(1) == 0)
        def _():
            wbf[...] = w_ref[...].astype(BF16)
        w = wbf[...]
    else:
        w = w_ref[...]
    acc = jnp.dot(x_ref[...], w, preferred_element_type=F32)
    if epilogue is not None:
        acc = epilogue(acc, *[e[...] for e in extras])
    o_ref[...] = acc.astype(o_ref.dtype)


def matmul(x, w, n_cols, *, tm, tn, out_dtype, extra=(), extra_specs=(), epilogue=None, name="mm"):
    m, k = x.shape
    cast_w = w.dtype != BF16
    scratch = [pltpu.VMEM((k, tn), BF16)] if cast_w else []
    return pl.pallas_call(
        functools.partial(_mm_kernel, n_extra=len(extra), epilogue=epilogue, cast_w=cast_w),
        out_shape=jax.ShapeDtypeStruct((m, n_cols), out_dtype),
        grid=(n_cols // tn, m // tm),
        in_specs=[pl.BlockSpec((tm, k), lambda j, i: (i, 0)),
                  pl.BlockSpec((k, tn), lambda j, i: (0, j))] + list(extra_specs),
        out_specs=pl.BlockSpec((tm, tn), lambda j, i: (i, j)),
        scratch_shapes=scratch,
        compiler_params=_cp("parallel", "arbitrary"),
        name=name,
    )(x, w, *extra)


def _qk_prep_kernel(*refs, dils, tm, n_tiles, first_tiles, keep_rows):
    q_ref, k_ref, v_ref, qn_ref, kn_ref, c_ref, s1_ref, s2_ref = refs[:8]
    qkv_out = refs[8:8 + 3 * N_GROUPS]
    kc_out = refs[8 + 3 * N_GROUPS:8 + 4 * N_GROUPS]
    vc_out = refs[8 + 4 * N_GROUPS:8 + 5 * N_GROUPS]
    scr = refs[8 + 5 * N_GROUPS]
    cos, s1, s2 = c_ref[...], s1_ref[...], s2_ref[...]
    half = ROPE_DIM // 2
    ti = pl.program_id(0) % n_tiles

    def prep(xh, g):
        xf = xh.astype(F32)
        y = xf * lax.rsqrt(jnp.mean(xf * xf, axis=-1, keepdims=True) + EPS) * g
        return (y * cos + pltpu.roll(y, half, 1) * s1
                + pltpu.roll(y, A_HEAD_DIM - half, 1) * s2)

    qn, kn = qn_ref[...], kn_ref[...]
    n_slabs = scr.shape[0]
    slab_i = 0
    for g in range(N_GROUPS):
        d = dils[g]
        kb = keep_rows[g]
        k_heads, v_heads = [], []
        for hl in range(A_HEADS):
            h = g * A_HEADS + hl
            sl = slice(h * A_HEAD_DIM, (h + 1) * A_HEAD_DIM)
            ol = slice(hl * A_HEAD_DIM, (hl + 1) * A_HEAD_DIM)
            qh = prep(q_ref[:, sl], qn)
            kh = prep(k_ref[:, sl], kn)
            vh = v_ref[:, sl].astype(F32)
            k_heads.append(kh)
            v_heads.append(vh)
            for which, val in enumerate((qh, kh, vh)):
                out = qkv_out[3 * g + which]
                if d == 1:
                    out[0, :, ol] = val.astype(BF16)
                else:
                    slab = slab_i % n_slabs
                    slab_i += 1
                    scr[slab] = val
                    for r in range(d):
                        out[r, :, ol] = scr[slab, pl.ds(r, tm // d, stride=d), :].astype(BF16)

        def write_cache(g=g, kb=kb, k_heads=k_heads, v_heads=v_heads):
            for hl in range(A_HEADS):
                rows_of_head = pl.ds(hl, kb, stride=A_HEADS)
                kc_out[g][rows_of_head, :] = k_heads[hl][tm - kb:, :]
                vc_out[g][rows_of_head, :] = v_heads[hl][tm - kb:, :]

        if first_tiles[g] == 0:
            write_cache()
        else:
            pl.when(ti >= first_tiles[g])(write_cache)


def rope_tables(pos):
    half = ROPE_DIM // 2
    inv = ROPE_THETA ** (-jnp.arange(half, dtype=F32) * 2.0 / ROPE_DIM)
    ang = pos.astype(F32)[:, None] * inv[None, :]
    cos, sin = jnp.cos(ang), jnp.sin(ang)
    t = pos.shape[0]
    rest = A_HEAD_DIM - ROPE_DIM
    c = jnp.concatenate([cos, cos, jnp.ones((t, rest), F32)], axis=1)
    s1 = jnp.concatenate([jnp.zeros((t, half), F32), sin, jnp.zeros((t, rest), F32)], axis=1)
    s2 = jnp.concatenate([-sin, jnp.zeros((t, half), F32), jnp.zeros((t, rest), F32)], axis=1)
    return c, s1, s2


def qk_prep(proj, q_norm, k_norm, tables, dils, batch, seq, keeps):
    c, s1, s2 = tables
    tm = _tile(seq, 256)
    nt = seq // tm
    qa_w = N_GROUPS * A_WIDTH
    first_tiles, keep_rows, n_keep_blocks = [], [], []
    for keep in keeps:
        kb = min(tm, keep)
        assert keep % kb == 0 and (seq - keep) % kb == 0 and tm % kb == 0
        first_tiles.append((seq - keep) // tm)
        keep_rows.append(kb)
        n_keep_blocks.append(keep // kb)
    tab = pl.BlockSpec((tm, A_HEAD_DIM), lambda i: (i % nt, 0))
    gspec = pl.BlockSpec((1, A_HEAD_DIM), lambda i: (0, 0))
    blk = lambda j: pl.BlockSpec((tm, qa_w), lambda i: (i, j))
    out_shapes, out_specs = [], []
    for g in range(N_GROUPS):
        d = dils[g]
        for _ in range(3):
            out_shapes.append(jax.ShapeDtypeStruct((batch, d, seq // d, A_WIDTH), BF16))
            out_specs.append(pl.BlockSpec((None, d, tm // d, A_WIDTH),
                                          lambda i: (i // nt, 0, i % nt, 0)))
    for _ in range(2):
        for g in range(N_GROUPS):
            kb, nkb, first = keep_rows[g], n_keep_blocks[g], first_tiles[g]
            out_shapes.append(jax.ShapeDtypeStruct((batch * keeps[g] * A_HEADS, A_HEAD_DIM), F32))
            out_specs.append(pl.BlockSpec(
                (kb * A_HEADS, A_HEAD_DIM),
                lambda i, nkb=nkb, first=first:
                ((i // nt) * nkb + jnp.clip(i % nt - first, 0, nkb - 1), 0)))
    res = pl.pallas_call(
        functools.partial(_qk_prep_kernel, dils=tuple(dils), tm=tm, n_tiles=nt,
                          first_tiles=tuple(first_tiles), keep_rows=tuple(keep_rows)),
        out_shape=tuple(out_shapes),
        grid=(batch * nt,),
        in_specs=[blk(0), blk(1), blk(2), gspec, gspec, tab, tab, tab],
        out_specs=tuple(out_specs),
        scratch_shapes=[pltpu.VMEM((12, tm, A_HEAD_DIM), F32)],
        compiler_params=_cp("arbitrary"),
        name="qk_prep",
    )(proj, proj, proj, q_norm.reshape(1, -1), k_norm.reshape(1, -1), c, s1, s2)
    n3 = 3 * N_GROUPS
    return res[:n3], res[n3:n3 + N_GROUPS], res[n3 + N_GROUPS:]


def _attn_prompt_kernel(q_ref, kp_ref, kc_ref, vp_ref, vc_ref, o_ref, lse_ref):
    first = pl.program_id(2) == 0
    qi = lax.broadcasted_iota(I32, (SPAN, 2 * SPAN), 0)
    ki = lax.broadcasted_iota(I32, (SPAN, 2 * SPAN), 1)
    mask = (ki >= qi) & (ki <= qi + SPAN) & (jnp.logical_not(first) | (ki >= SPAN))
    lane = lax.broadcasted_iota(I32, (SPAN, LANE), 1)
    lse_all = jnp.zeros((SPAN, LANE), F32)
    scale = A_HEAD_DIM ** -0.5
    for h in range(A_HEADS):
        sl = slice(h * A_HEAD_DIM, (h + 1) * A_HEAD_DIM)
        kh = jnp.concatenate([kp_ref[:, sl], kc_ref[:, sl]], axis=0)
        vh = jnp.concatenate([vp_ref[:, sl], vc_ref[:, sl]], axis=0)
        s = lax.dot_general(q_ref[:, sl], kh, (((1,), (1,)), ((), ())),
                            preferred_element_type=F32) * scale
        s = jnp.where(mask, s, NEG_INF)
        m = jnp.max(s, axis=-1, keepdims=True)
        p = jnp.exp(s - m)
        l = jnp.sum(p, axis=-1, keepdims=True)
        o = jnp.dot(p.astype(BF16), vh, preferred_element_type=F32) / l
        o_ref[:, sl] = o.astype(o_ref.dtype)
        lse_all = jnp.where(lane == h, m + jnp.log(l), lse_all)
    lse_ref[...] = lse_all


def attn_prompt(q_g, k_g, v_g, g):
    batch, dil, l, _ = q_g.shape
    nblk = l // SPAN
    cur = pl.BlockSpec((None, None, SPAN, A_WIDTH), lambda b, r, i: (b, r, i, 0))
    prev = pl.BlockSpec((None, None, SPAN, A_WIDTH), lambda b, r, i: (b, r, jnp.maximum(i - 1, 0), 0))
    return pl.pallas_call(
        _attn_prompt_kernel,
        out_shape=(jax.ShapeDtypeStruct((batch, dil, l, A_WIDTH), BF16),
                   jax.ShapeDtypeStruct((batch, dil, l, LANE), F32)),
        grid=(batch, dil, nblk),
        in_specs=[cur, prev, cur, prev, cur],
        out_specs=(cur, pl.BlockSpec((None, None, SPAN, LANE), lambda b, r, i: (b, r, i, 0))),
        compiler_params=_cp("parallel", "parallel", "arbitrary"),
        name=f"attn_prompt_g{g}",
    )(q_g, k_g, k_g, v_g, v_g)


def _attn_sample_kernel(q_ref, k_ref, v_ref, ck0, cv0, ck1, cv1, ck2, cv2, o_ref, l_ref, *, t_new):
    caches = ((ck0, cv0), (ck1, cv1), (ck2, cv2))
    scale = A_HEAD_DIM ** -0.5
    jidx = lax.broadcasted_iota(I32, (SPAN, A_HEADS, 1), 0)
    for g, (_, dil) in enumerate(DIL_GROUPS):
        ck, cv = caches[g]
        for t in range(t_new):
            res = 0 if dil == 1 else t
            q = q_ref[t, g]
            s_c = jnp.sum(ck[:, res] * q[None], axis=-1, keepdims=True) * scale
            if dil == 1:
                s_c = jnp.where(jidx >= t, s_c, NEG_INF)
            new = range(t + 1) if dil == 1 else (t,)
            s_n = [jnp.sum(q * k_ref[u, g], axis=-1, keepdims=True) * scale for u in new]
            m = jnp.max(s_c, axis=0)
            for s in s_n:
                m = jnp.maximum(m, s)
            p_c = jnp.exp(s_c - m[None])
            lsum = jnp.sum(p_c, axis=0)
            o = jnp.sum(p_c * cv[:, res], axis=0)
            for u, s in zip(new, s_n):
                p_u = jnp.exp(s - m)
                lsum = lsum + p_u
                o = o + p_u * v_ref[u, g]
            o_ref[t, g] = o / lsum
            l_ref[t, g] = jnp.broadcast_to(m + jnp.log(lsum), (A_HEADS, A_HEAD_DIM))


def attn_sample(q5, k5, v5, caches_k, caches_v, layer):
    db, t_new = q5.shape[:2]
    new_spec = pl.BlockSpec((None, t_new, N_GROUPS, A_HEADS, A_HEAD_DIM), lambda b: (b, 0, 0, 0, 0))
    ins, specs = [q5, k5, v5], [new_spec] * 3
    for g, (win, dil) in enumerate(DIL_GROUPS):
        for c in (caches_k[g], caches_v[g]):
            depth, _, wc, hh, ee = c.shape
            assert wc == win and (hh, ee) == (A_HEADS, A_HEAD_DIM) and (dil == 1 or t_new <= dil)
            nres = 1 if dil == 1 else t_new
            ins.append(c.reshape(depth, db, wc // dil, dil, hh, ee))
            specs.append(pl.BlockSpec((None, None, SPAN, nres, hh, ee),
                                      lambda b: (layer, b, 0, 0, 0, 0)))
    return pl.pallas_call(
        functools.partial(_attn_sample_kernel, t_new=t_new),
        out_shape=(jax.ShapeDtypeStruct(q5.shape, F32), jax.ShapeDtypeStruct(q5.shape, F32)),
        grid=(db,),
        in_specs=specs,
        out_specs=(new_spec, new_spec),
        compiler_params=_cp("parallel"),
        name="attn_sample",
    )(*ins)


def _combine_kernel(*refs, dils, tm):
    o_refs, l_refs, out_ref, scr_o, scr_l = refs[:3], refs[3:6], refs[6], refs[7], refs[8]
    ls = []
    for g, d in enumerate(dils):
        if d == 1:
            ls.append(l_refs[g][0])
        else:
            for r in range(d):
                scr_l[g, pl.ds(r, tm // d, stride=d), :] = l_refs[g][r]
            ls.append(scr_l[g])
    m = jnp.maximum(jnp.maximum(ls[0], ls[1]), ls[2])
    es = [jnp.exp(x - m) for x in ls]
    den = es[0] + es[1] + es[2]
    rr = lax.broadcasted_iota(I32, (LANE, A_WIDTH), 0)
    cc = lax.broadcasted_iota(I32, (LANE, A_WIDTH), 1)
    expand = (cc // A_HEAD_DIM == rr).astype(BF16)
    wxs = []
    for e in es:
        w = e / den
        hi = w.astype(BF16)
        lo = (w - hi.astype(F32)).astype(BF16)
        wxs.append(jnp.dot(hi, expand, preferred_element_type=F32)
                   + jnp.dot(lo, expand, preferred_element_type=F32))
    for h in range(A_HEADS):
        hs = slice(h * A_HEAD_DIM, (h + 1) * A_HEAD_DIM)
        acc = None
        for g, d in enumerate(dils):
            if d == 1:
                o_tok = o_refs[g][0, :, hs].astype(F32)
            else:
                for r in range(d):
                    scr_o[g, h, pl.ds(r, tm // d, stride=d), :] = o_refs[g][r, :, hs].astype(F32)
                o_tok = scr_o[g, h]
            term = wxs[g][:, hs] * o_tok
            acc = term if acc is None else acc + term
        out_ref[:, hs] = acc.astype(out_ref.dtype)


def combine_groups(os_, ls_):
    batch = os_[0].shape[0]
    dils = tuple(o.shape[1] for o in os_)
    seq = os_[0].shape[1] * os_[0].shape[2]
    tm = _tile(seq, 256)
    nt = seq // tm
    specs = []
    for width in (A_WIDTH, LANE):
        for d in dils:
            specs.append(pl.BlockSpec((None, d, tm // d, width), lambda i: (i // nt, 0, i % nt, 0)))
    return pl.pallas_call(
        functools.partial(_combine_kernel, dils=dils, tm=tm),
        out_shape=jax.ShapeDtypeStruct((batch * seq, A_WIDTH), BF16),
        grid=(batch * nt,),
        in_specs=specs,
        out_specs=pl.BlockSpec((tm, A_WIDTH), lambda i: (i, 0)),
        scratch_shapes=[pltpu.VMEM((N_GROUPS, A_HEADS, tm, A_HEAD_DIM), F32),
                        pltpu.VMEM((N_GROUPS, tm, LANE), F32)],
        compiler_params=_cp("parallel"),
        name="combine_groups",
    )(*os_, *ls_)


def _alpha_kernel(h_ref, w1_ref, w2_ref, b_ref, o_ref):
    lr = jnp.dot(h_ref[...], w1_ref[...], preferred_element_type=F32)
    z = jnp.dot(lr.astype(BF16), w2_ref[...], preferred_element_type=F32) + b_ref[...]
    log_sig = jnp.minimum(z, 0.0) - jnp.log(1.0 + jnp.exp(-jnp.abs(z)))
    o_ref[...] = log_sig / GLA_TAU


def gla_log_decay(h, w_lr, alpha_up, alpha_bias):
    m, d = h.shape
    gq = alpha_up.shape[1]
    w1 = jnp.pad(w_lr, ((0, 0), (0, LANE - GLA_RANK))).astype(BF16)
    w2 = jnp.pad(alpha_up, ((0, LANE - GLA_RANK), (0, 0))).astype(BF16)
    tm = _tile(m, 512)
    return pl.pallas_call(
        _alpha_kernel,
        out_shape=jax.ShapeDtypeStruct((m, gq), F32),
        grid=(m // tm,),
        in_specs=[pl.BlockSpec((tm, d), lambda i: (i, 0)),
                  pl.BlockSpec((d, LANE), lambda i: (0, 0)),
                  pl.BlockSpec((LANE, gq), lambda i: (0, 0)),
                  pl.BlockSpec((1, gq), lambda i: (0, 0))],
        out_specs=pl.BlockSpec((tm, gq), lambda i: (i, 0)),
        compiler_params=_cp("parallel"),
        name="gla_log_decay",
    )(h, w1, w2, alpha_bias.reshape(1, gq))


def _gla_kernel(*refs, chunk, dk, dv, has_s0):
    if has_s0:
        q_ref, k_ref, v0_ref, v1_ref, la_ref, r0_ref, r1_ref, gn_ref, s0_ref, ob_ref, s_ref = refs
    else:
        q_ref, k_ref, v0_ref, v1_ref, la_ref, r0_ref, r1_ref, gn_ref, ob_ref, s_ref = refs

    @pl.when(pl.program_id(1) == 0)
    def _():
        s_ref[...] = s0_ref[...].astype(F32) if has_s0 else jnp.zeros(s_ref.shape, F32)

    ri = lax.broadcasted_iota(I32, (chunk, chunk), 0)
    ci = lax.broadcasted_iota(I32, (chunk, chunk), 1)
    causal = ri >= ci
    tri = causal.astype(F32)
    gn = gn_ref[...]
    per_block = GLA_HEADS // 2
    for h in range(GLA_HEADS):
        ks = slice(h * dk, (h + 1) * dk)
        vs = slice((h % per_block) * dv, (h % per_block + 1) * dv)
        v_ref, r_ref = (v0_ref, r0_ref) if h < per_block else (v1_ref, r1_ref)
        la = la_ref[:, ks]
        b = jnp.dot(tri, la, precision=lax.Precision.HIGHEST,
                    preferred_element_type=F32)
        b_last = b[chunk - 1:chunk, :]
        q = q_ref[:, ks].astype(F32) * (dk ** -0.5)
        k = k_ref[:, ks].astype(F32)
        v = v_ref[:, vs]
        q_dec = (q * jnp.exp(b)).astype(BF16)
        k_inv = (k * jnp.exp(-b)).astype(BF16)
        k_dec = (k * jnp.exp(b_last - b)).astype(BF16)
        att = lax.dot_general(q_dec, k_inv, (((1,), (1,)), ((), ())), preferred_element_type=F32)
        att = jnp.where(causal, att, 0.0).astype(BF16)
        s_old = s_ref[h]
        o = (jnp.dot(att, v, preferred_element_type=F32)
             + jnp.dot(q_dec, s_old.astype(BF16), preferred_element_type=F32))
        kv = lax.dot_general(k_dec, v, (((0,), (0,)), ((), ())), preferred_element_type=F32)
        decay = jnp.sum(jnp.transpose(la), axis=1, keepdims=True)
        s_ref[h] = jnp.exp(decay) * s_old + kv
        y = o * lax.rsqrt(jnp.mean(o * o, axis=-1, keepdims=True) + EPS) * gn
        ob_ref[:, h * dv:(h + 1) * dv] = (y * _silu(r_ref[:, vs].astype(F32))).astype(ob_ref.dtype)


def gla(proj3, la3, gla_norm, s0, layer, dk, dv):
    bsz, t, w = proj3.shape
    chunk = GLA_CHUNK
    qa_w = N_GROUPS * A_WIDTH
    gq, gv = GLA_HEADS * dk, GLA_HEADS * dv
    assert gv == 2 * gq and (3 * qa_w) % gq == 0
    base = 3 * qa_w // gq
    cspec = lambda off: pl.BlockSpec((None, chunk, gq), lambda b, c: (b, c, off))
    ins = [proj3, proj3, proj3, proj3, la3, proj3, proj3, gla_norm.reshape(1, dv)]
    specs = [cspec(base), cspec(base + 1), cspec(base + 2), cspec(base + 3), cspec(0),
             cspec(base + 4), cspec(base + 5), pl.BlockSpec((1, dv), lambda b, c: (0, 0))]
    if s0 is not None:
        ins.append(s0)
        specs.append(pl.BlockSpec((None, None, GLA_HEADS, dk, dv), lambda b, c: (layer, b, 0, 0, 0)))
    return pl.pallas_call(
        functools.partial(_gla_kernel, chunk=chunk, dk=dk, dv=dv, has_s0=s0 is not None),
        out_shape=(jax.ShapeDtypeStruct((bsz, t, gv), BF16),
                   jax.ShapeDtypeStruct((bsz, GLA_HEADS, dk, dv), F32)),
        grid=(bsz, t // chunk),
        in_specs=specs,
        out_specs=(pl.BlockSpec((None, chunk, gv), lambda b, c: (b, c, 0)),
                   pl.BlockSpec((None, GLA_HEADS, dk, dv), lambda b, c: (b, 0, 0, 0))),
        compiler_params=_cp("parallel", "arbitrary"),
        name="gla",
    )(*ins)


def _merge_kernel(h_ref, oa_ref, ob_ref, wga_ref, wgb_ref, wa_ref, wb_ref, o_ref):
    h = h_ref[...]
    ga = jnp.dot(h, wga_ref[...], preferred_element_type=F32)
    gb = jnp.dot(h, wgb_ref[...], preferred_element_type=F32)
    a = jnp.dot(oa_ref[...], wa_ref[...], preferred_element_type=F32)
    b = jnp.dot(ob_ref[...], wb_ref[...], preferred_element_type=F32)
    o_ref[...] = (jax.nn.sigmoid(ga) * a + jax.nn.sigmoid(gb) * b).astype(o_ref.dtype)


def merge_branches(h, oa, ob, w_gates, w_a, w_b, tm):
    m, d = h.shape
    tn = _tile(d, 512)
    nj = d // tn
    row = lambda width: pl.BlockSpec((tm, width), lambda j, i: (i, 0))
    wcol = lambda rows, off: pl.BlockSpec((rows, tn), lambda j, i: (0, j + off))
    return pl.pallas_call(
        _merge_kernel,
        out_shape=jax.ShapeDtypeStruct((m, d), BF16),
        grid=(nj, m // tm),
        in_specs=[row(d), row(oa.shape[1]), row(ob.shape[1]),
                  wcol(d, 0), wcol(d, nj), wcol(w_a.shape[0], 0), wcol(w_b.shape[0], 0)],
        out_specs=pl.BlockSpec((tm, tn), lambda j, i: (i, j)),
        compiler_params=_cp("parallel", "arbitrary"),
        name="merge_branches",
    )(h, oa, ob, w_gates, w_gates, w_a, w_b)


def _pack_rows(x):
    half = x.shape[1] // 2
    lo = lax.bitcast_convert_type(x[:, :half].astype(BF16).astype(F32), U32)
    hi = lax.bitcast_convert_type(x[:, half:].astype(BF16).astype(F32), U32)
    return (lo >> 16) | (hi & jnp.uint32(0xFFFF0000))


def _unpack_chunk(u):
    return (lax.bitcast_convert_type(u << 16, F32),
            lax.bitcast_convert_type(u & jnp.uint32(0xFFFF0000), F32))


def _store_rows(ref, packed, rows):
    n_chunks = packed.shape[1] // LANE
    for c in range(n_chunks):
        ref[pl.ds(c, rows, stride=n_chunks), :] = packed[:, c * LANE:(c + 1) * LANE]


def _load_rows(ref, rows, n_chunks, dtype, start=0):
    lo, hi = [], []
    for c in range(n_chunks):
        a, b = _unpack_chunk(ref[pl.ds(start + c, rows, stride=n_chunks), :])
        lo.append(a.astype(dtype))
        hi.append(b.astype(dtype))
    return jnp.concatenate(lo + hi, axis=1)


def _norm_router_kernel(x_ref, g_ref, sc_ref, sh_ref, wr_ref, rb_ref, hr_ref, idx_ref, wt_ref):
    h = _norm_mod(x_ref[...], g_ref[...], sc_ref[...], sh_ref[...])
    tm = h.shape[0]
    _store_rows(hr_ref, _pack_rows(h), tm)
    per = N_EXPERTS // N_EXPERT_GROUPS
    logits = lax.dot_general(wr_ref[...], h, (((1,), (1,)), ((), ())),
                             precision=lax.Precision.HIGHEST, preferred_element_type=F32)
    scores = jax.nn.sigmoid(logits)
    biased = scores + rb_ref[...]
    bg = biased.reshape(N_EXPERT_GROUPS, per, tm)
    ip = lax.broadcasted_iota(I32, bg.shape, 1)
    m1 = jnp.max(bg, axis=1, keepdims=True)
    i1 = jnp.min(jnp.where(bg == m1, ip, per), axis=1, keepdims=True)
    m2 = jnp.max(jnp.where(ip == i1, -jnp.inf, bg), axis=1, keepdims=True)
    grp = (m1 + m2)[:, 0, :]
    gi = lax.broadcasted_iota(I32, grp.shape, 0)
    rank = jnp.zeros(grp.shape, I32)
    for g in range(N_EXPERT_GROUPS):
        other = grp[g:g + 1, :]
        ahead = (other > grp) | ((other == grp) & (g < gi))
        rank = rank + ahead.astype(I32)
    sel = (rank < TOPK_GROUPS)[:, None, :]
    masked = jnp.where(sel, bg, -jnp.inf).reshape(N_EXPERTS, tm)
    ie = lax.broadcasted_iota(I32, masked.shape, 0)
    ids, ws = [], []
    for _ in range(TOP_K):
        mx = jnp.max(masked, axis=0, keepdims=True)
        ik = jnp.min(jnp.where(masked == mx, ie, N_EXPERTS), axis=0, keepdims=True)
        hit = ie == ik
        ids.append(ik)
        ws.append(jnp.sum(jnp.where(hit, scores, 0.0), axis=0, keepdims=True))
        masked = jnp.where(hit, -jnp.inf, masked)
    tot = ws[0]
    for wk in ws[1:]:
        tot = tot + wk
    pad = TOPK_PAD - TOP_K
    idx_ref[...] = jnp.concatenate(ids + [jnp.zeros((pad, tm), I32)], axis=0)
    wt_ref[...] = jnp.concatenate([wk / tot * ROUTED_SCALE for wk in ws]
                                  + [jnp.zeros((pad, tm), F32)], axis=0)


def norm_router(x, gain, st, comp_shift, comp_scale, w_router_t, router_bias):
    m, d = x.shape
    tm = _tile(m, 256) if st.r == 1 else m
    nc = d // 2 // LANE
    col = lambda i: (i, 0)
    e = w_router_t.shape[0]
    return pl.pallas_call(
        _norm_router_kernel,
        out_shape=(jax.ShapeDtypeStruct((m * nc, LANE), U32),
                   jax.ShapeDtypeStruct((TOPK_PAD, m), I32),
                   jax.ShapeDtypeStruct((TOPK_PAD, m), F32)),
        grid=(m // tm,),
        in_specs=[pl.BlockSpec((tm, d), lambda i: (i, 0)),
                  pl.BlockSpec((1, d), lambda i: (0, 0)),
                  st.mod_spec(comp_scale, tm, col_of=col),
                  st.mod_spec(comp_shift, tm, col_of=col),
                  pl.BlockSpec((e, d), lambda i: (0, 0)),
                  pl.BlockSpec((e, 1), lambda i: (0, 0))],
        out_specs=(pl.BlockSpec((tm * nc, LANE), lambda i: (i, 0)),
                   pl.BlockSpec((TOPK_PAD, tm), lambda i: (0, i)),
                   pl.BlockSpec((TOPK_PAD, tm), lambda i: (0, i))),
        compiler_params=_cp("parallel"),
        name="norm_router",
    )(x, gain.reshape(1, d), st.mod, st.mod, w_router_t, router_bias.reshape(e, 1))


def _row_dma_loop(n_rows, start_one):
    def body(i, carry):
        for j in range(DMA_UNROLL):
            start_one(i * DMA_UNROLL + j)
        return carry
    lax.fori_loop(0, n_rows // DMA_UNROLL, body, 0)


def _moe_kernel(be_ref, nb_ref, cnt_ref, tok_cur, tok_nxt, asg_ref, h_hbm, wg_ref, wu_ref, wd_ref,
                y_hbm, xbuf, ybuf, sem_in, sem_out):
    i = pl.program_id(0)
    n = pl.num_programs(0)
    nb = nb_ref[0]
    slot = i % 2
    nc = h_hbm.shape[1]
    rows = xbuf.shape[1] // nc
    row_at = lambda buf, s, r: buf.at[s, pl.ds(pl.multiple_of(r * nc, nc), nc)]

    def gather(tok_ref, s):
        _row_dma_loop(rows, lambda r: pltpu.make_async_copy(
            h_hbm.at[tok_ref[0, 0, r]], row_at(xbuf, s, r), sem_in.at[s]).start())

    def out_copy(s, r, dst_row):
        return pltpu.make_async_copy(row_at(ybuf, s, r), y_hbm.at[dst_row], sem_out.at[s])

    def one_by_one(count, fn):
        def body(r, carry):
            fn(r)
            return carry
        lax.fori_loop(0, count, body, 0)

    def wait_scatter(block, s):
        cnt = cnt_ref[block]

        @pl.when(cnt == rows)
        def _():
            pltpu.make_async_copy(ybuf.at[s], ybuf.at[s], sem_out.at[s]).wait()

        @pl.when(cnt < rows)
        def _():
            one_by_one(cnt, lambda r: out_copy(s, r, 0).wait())

    @pl.when(i == 0)
    def _():
        gather(tok_cur, 0)

    @pl.when(i + 1 < nb)
    def _():
        gather(tok_nxt, 1 - slot)

    @pl.when(i < nb)
    def _():
        pltpu.make_async_copy(xbuf.at[slot], xbuf.at[slot], sem_in.at[slot]).wait()

        @pl.when(i >= 2)
        def _():
            wait_scatter(i - 2, slot)
        x = _load_rows(xbuf.at[slot], rows, nc, BF16)
        g = jnp.dot(x, wg_ref[...], preferred_element_type=F32)
        u = jnp.dot(x, wu_ref[...], preferred_element_type=F32)
        hb = (_silu(g) * u).astype(BF16)
        y = jnp.dot(hb, wd_ref[...], preferred_element_type=F32)
        _store_rows(ybuf.at[slot], _pack_rows(y), rows)
        cnt = cnt_ref[i]
        start_row = lambda r: out_copy(slot, r, asg_ref[0, 0, r]).start()

        @pl.when(cnt == rows)
        def _():
            _row_dma_loop(rows, start_row)

        @pl.when(cnt < rows)
        def _():
            one_by_one(cnt, start_row)

    @pl.when(i == n - 1)
    def _():
        @pl.when(nb >= 1)
        def _():
            wait_scatter(nb - 1, (nb - 1) % 2)

        @pl.when(nb >= 2)
        def _():
            wait_scatter(nb - 2, nb % 2)


def moe_experts(h_rows, row_tok, row_asg, block_e, nb_used, block_cnt, wg, wu, wd, layer):
    n_blocks = block_e.shape[0]
    n_tokens, nc = h_rows.shape[:2]
    d = 2 * nc * LANE
    f = wg.shape[-1]
    tok3 = row_tok.reshape(n_blocks, 1, MOE_ROWS)
    asg3 = row_asg.reshape(n_blocks, 1, MOE_ROWS)
    smem = lambda f_: pl.BlockSpec((1, 1, MOE_ROWS), f_, memory_space=pltpu.SMEM)
    last = n_blocks - 1
    grid_spec = pltpu.PrefetchScalarGridSpec(
        num_scalar_prefetch=3,
        grid=(n_blocks,),
        in_specs=[smem(lambda i, be, nb, cnt: (i, 0, 0)),
                  smem(lambda i, be, nb, cnt: (jnp.minimum(i + 1, last), 0, 0)),
                  smem(lambda i, be, nb, cnt: (i, 0, 0)),
                  pl.BlockSpec(memory_space=pl.ANY),
                  pl.BlockSpec((None, None, d, f), lambda i, be, nb, cnt: (layer, be[i], 0, 0)),
                  pl.BlockSpec((None, None, d, f), lambda i, be, nb, cnt: (layer, be[i], 0, 0)),
                  pl.BlockSpec((None, None, f, d), lambda i, be, nb, cnt: (layer, be[i], 0, 0))],
        out_specs=pl.BlockSpec(memory_space=pl.ANY),
        scratch_shapes=[pltpu.VMEM((2, MOE_ROWS * nc, LANE), U32),
                        pltpu.VMEM((2, MOE_ROWS * nc, LANE), U32),
                        pltpu.SemaphoreType.DMA((2,)), pltpu.SemaphoreType.DMA((2,))],
    )
    return pl.pallas_call(
        _moe_kernel,
        out_shape=jax.ShapeDtypeStruct((n_tokens * TOP_K, nc, LANE), U32),
        grid_spec=grid_spec,
        compiler_params=_cp("arbitrary"),
        name="moe_experts",
    )(block_e, nb_used, block_cnt, tok3, tok3, asg3, h_rows, wg, wu, wd)


def moe_dispatch(idx_t, n_tokens):
    nk = n_tokens * TOP_K
    flat_e = idx_t.reshape(-1)
    order = jnp.argsort(flat_e).astype(I32)
    experts = jnp.arange(N_EXPERTS, dtype=I32)
    counts = jnp.sum((flat_e[:, None] == experts[None, :]).astype(I32), axis=0)
    start = jnp.cumsum(counts) - counts
    padded = (counts + MOE_ROWS - 1) // MOE_ROWS * MOE_ROWS
    pend = jnp.cumsum(padded)
    pad_start = pend - padded
    n_blocks = -(-nk // MOE_ROWS) + N_EXPERTS
    blk_first = jnp.arange(n_blocks, dtype=I32) * MOE_ROWS
    block_e = jnp.minimum(jnp.sum((pend[None, :] <= blk_first[:, None]).astype(I32), axis=1),
                          N_EXPERTS - 1)
    nb_used = (pend[-1] // MOE_ROWS).astype(I32).reshape(1)
    p = jnp.arange(n_blocks * MOE_ROWS, dtype=I32)
    e_p = jnp.repeat(block_e, MOE_ROWS)
    k_p = p - pad_start[e_p]
    valid = (k_p < counts[e_p]) & (p < pend[-1])
    a_p = order[jnp.clip(start[e_p] + k_p, 0, nk - 1)]
    row_tok = jnp.where(valid, a_p % n_tokens, 0).astype(I32)
    row_asg = jnp.where(valid, a_p, 0).astype(I32)
    block_cnt = jnp.sum(valid.reshape(n_blocks, MOE_ROWS).astype(I32), axis=1)
    return row_tok, row_asg, block_e.astype(I32), nb_used, block_cnt


def _finish_kernel(*refs):
    h_ref = refs[0]
    y_refs = refs[1:1 + TOP_K]
    wt_ref, x_ref, g2_ref, wg_ref, wu_ref, wd_ref, o_ref = refs[1 + TOP_K:]
    tm, d = x_ref.shape
    nc = d // 2 // LANE
    h = _load_rows(h_ref, tm, nc, BF16)
    g = jnp.dot(h, wg_ref[...], preferred_element_type=F32)
    u = jnp.dot(h, wu_ref[...], preferred_element_type=F32)
    acc = jnp.dot((_silu(g) * u).astype(BF16), wd_ref[...], preferred_element_type=F32)
    wt = wt_ref[...]
    for k in range(TOP_K):
        acc = acc + wt[:, k:k + 1] * _load_rows(y_refs[k], tm, nc, F32)
    o_ref[...] = x_ref[...] + g2_ref[...] * acc


def moe_finish(x1, st, h_rows2, wts, y_rows2, wgs, wus, wds, row_off, n_total):
    m, d = x1.shape
    f = wgs.shape[1]
    tm = _tile(m, 128)
    nt = m // tm
    assert row_off % tm == 0 and n_total % tm == 0
    off = row_off // tm
    per_slot = n_total // tm
    nc = d // 2 // LANE
    rows_spec = lambda base: pl.BlockSpec((tm * nc, LANE), lambda i: (i + off + base, 0))
    whole = lambda a, b: pl.BlockSpec((a, b), lambda i: (0, 0), pipeline_mode=pl.Buffered(1))
    return pl.pallas_call(
        _finish_kernel,
        out_shape=jax.ShapeDtypeStruct((m, d), F32),
        grid=(nt,),
        in_specs=[rows_spec(0)] + [rows_spec(k * per_slot) for k in range(TOP_K)]
                 + [pl.BlockSpec((tm, TOPK_PAD), lambda i: (i + off, 0)),
                    pl.BlockSpec((tm, d), lambda i: (i, 0)),
                    st.mod_spec(5, tm, col_of=lambda i: (i, 0)),
                    whole(d, f), whole(d, f), whole(f, d)],
        out_specs=pl.BlockSpec((tm, d), lambda i: (i, 0)),
        compiler_params=_cp("parallel"),
        name="moe_finish",
    )(h_rows2, *([y_rows2] * TOP_K), wts, x1, st.mod, wgs, wus, wds)


def _residual_epilogue(acc, x, gate):
    return x + gate * acc


def decoder_layer(l, xp, xs, c_all, p, caches_k, caches_v, state_gla, dims):
    batch, seq, db, t_new = dims
    d = xp.shape[1]
    mp, ms = xp.shape[0], xs.shape[0]
    n_all = mp + ms
    qa_w = N_GROUPS * A_WIDTH
    dk, dv = d // 2 // GLA_HEADS, d // GLA_HEADS
    gq, gv = GLA_HEADS * dk, GLA_HEADS * dv
    n_main = 3 * qa_w + 2 * gq + 2 * gv
    nc = d // 2 // LANE

    mod = ada_mod(c_all, p['w_ada'][l], p['b_ada'][l])
    st_p = Stream(mp, _tile(mp, 1024), mod[:batch].reshape(batch, 1, 6 * d), seq,
                  jnp.arange(seq))
    st_s = Stream(ms, ms, jnp.repeat(mod[batch:], t_new, axis=0).reshape(1, ms, 6 * d), 0,
                  jnp.tile(PAST_LEN + jnp.arange(t_new), db))
    w_in = p['w_in'][l]
    w_lr = w_in[:, n_main:n_main + GLA_RANK]
    w_gates = w_in[:, n_main + GLA_RANK:].astype(BF16)
    w_a, w_b = p['w_branch_a'][l].astype(BF16), p['w_branch_b'][l].astype(BF16)
    w_out = p['w_out'][l].astype(BF16)
    w_router_t = p['w_router'][l].T

    new_k, new_v, new_s, x1s, idxs, wtss, hrs = {}, {}, {}, {}, [], [], []
    for name, st, x in (('p', st_p, xp), ('s', st_s, xs)):
        m, tm = st.rows, st.tm
        h1 = norm_mod(x, p['norm_mix'][l], st, 0, 1)
        proj = matmul(h1, w_in, n_main, tm=tm, tn=_tile(n_main, 1024), out_dtype=BF16,
                      name="in_proj")
        la = gla_log_decay(h1, w_lr, p['alpha_up'][l], p['alpha_bias'][l])
        tables = rope_tables(st.pos)
        if name == 'p':
            dils = [dil for _, dil in DIL_GROUPS]
            keeps = [min(w_, seq) for w_, _ in DIL_GROUPS]
            qkv, kc, vc = qk_prep(proj, p['q_norm'][l], p['k_norm'][l], tables, dils, batch, seq,
                                  keeps)
            outs, lses = zip(*[attn_prompt(qkv[3 * g], qkv[3 * g + 1], qkv[3 * g + 2], g)
                               for g in range(N_GROUPS)])
            oa = combine_groups(outs, lses)
            ob, s_new = gla(proj.reshape(batch, seq, -1), la.reshape(batch, seq, gq),
                            p['gla_norm'][l], None, l, dk, dv)
            ob = ob.reshape(m, gv)
            shape4 = lambda a, keep: a.reshape(batch, keep, A_HEADS, A_HEAD_DIM)
            new_k[name] = [shape4(a, keep) for a, keep in zip(kc, keeps)]
            new_v[name] = [shape4(a, keep) for a, keep in zip(vc, keeps)]
        else:
            qkv, kc, vc = qk_prep(proj, p['q_norm'][l], p['k_norm'][l], tables, [1] * N_GROUPS,
                                  1, m, [m] * N_GROUPS)
            new5 = lambda which: jnp.stack(
                [qkv[3 * g + which].reshape(db, t_new, A_HEADS, A_HEAD_DIM)
                 for g in range(N_GROUPS)], axis=2).astype(F32)
            o5, l5 = attn_sample(new5(0), new5(1), new5(2), caches_k, caches_v, l)
            outs = [o5[:, :, g].reshape(1, 1, m, A_WIDTH).astype(BF16) for g in range(N_GROUPS)]
            lses = [jnp.pad(l5[:, :, g, :, 0].reshape(m, A_HEADS),
                            ((0, 0), (0, LANE - A_HEADS))).reshape(1, 1, m, LANE)
                    for g in range(N_GROUPS)]
            oa = combine_groups(outs, lses)
            padc = lambda a: jnp.pad(a.reshape(db, t_new, -1),
                                     ((0, 0), (0, GLA_CHUNK - t_new), (0, 0)))
            ob, s_new = gla(padc(proj), padc(la), p['gla_norm'][l], state_gla, l, dk, dv)
            ob = ob[:, :t_new].reshape(m, gv)
            shape4 = lambda a: a.reshape(db, t_new, A_HEADS, A_HEAD_DIM)
            new_k[name] = [shape4(a) for a in kc]
            new_v[name] = [shape4(a) for a in vc]
        new_s[name] = s_new
        merged = merge_branches(h1, oa, ob, w_gates, w_a, w_b, tm)
        tn = _tile(d, 1024)
        x1 = matmul(merged, w_out, d, tm=tm, tn=tn, out_dtype=F32,
                    extra=(x, st.mod),
                    extra_specs=(pl.BlockSpec((tm, tn), lambda j, i: (i, j)),
                                 st.mod_spec(2, tm, tn, col_of=lambda j, i: (i, j))),
                    epilogue=_residual_epilogue, name="out_proj")
        h_rows, idx_t, wt_t = norm_router(x1, p['norm_ffn'][l], st, 3, 4, w_router_t,
                                          p['router_bias'][l])
        x1s[name] = x1
        hrs.append(h_rows)
        idxs.append(idx_t[:TOP_K])
        wtss.append(wt_t)

    idx_all = jnp.concatenate(idxs, axis=1)
    wts_all = jnp.concatenate(wtss, axis=1).T
    h_rows2 = jnp.concatenate(hrs, axis=0)
    row_tok, row_asg, block_e, nb_used, block_cnt = moe_dispatch(idx_all, n_all)
    y_rows = moe_experts(h_rows2.reshape(n_all, nc, LANE), row_tok, row_asg, block_e, nb_used,
                         block_cnt, p['w_gate_e_bf'], p['w_up_e_bf'], p['w_down_e_bf'], l)
    y_rows2 = y_rows.reshape(-1, LANE)
    wgs, wus, wds = (p['w_gate_s'][l].astype(BF16), p['w_up_s'][l].astype(BF16),
                     p['w_down_s'][l].astype(BF16))
    xp2 = moe_finish(x1s['p'], st_p, h_rows2, wts_all, y_rows2, wgs, wus, wds, 0, n_all)
    xs2 = moe_finish(x1s['s'], st_s, h_rows2, wts_all, y_rows2, wgs, wus, wds, mp, n_all)
    return xp2, xs2, new_k, new_v, new_s


def kernel(x_prompt, x_sample, cache_k_w128, cache_v_w128, cache_k_w512, cache_v_w512, cache_k_w2048, cache_v_w2048, state_gla, c_prompt, c_sample, w_in, q_norm, k_norm, alpha_up, alpha_bias, gla_norm, w_branch_a, w_branch_b, w_out, norm_mix, norm_ffn, w_ada, b_ada, w_router, router_bias, w_gate_e, w_up_e, w_down_e, w_gate_s, w_up_s, w_down_s):
    batch, seq, d = x_prompt.shape
    db, t_new, _ = x_sample.shape
    depth = w_in.shape[0]
    dims = (batch, seq, db, t_new)
    p = dict(w_in=w_in, q_norm=q_norm, k_norm=k_norm, alpha_up=alpha_up, alpha_bias=alpha_bias,
             gla_norm=gla_norm, w_branch_a=w_branch_a, w_branch_b=w_branch_b, w_out=w_out,
             norm_mix=norm_mix, norm_ffn=norm_ffn, w_ada=w_ada, b_ada=b_ada, w_router=w_router,
             router_bias=router_bias, w_gate_s=w_gate_s, w_up_s=w_up_s, w_down_s=w_down_s,
             w_gate_e_bf=w_gate_e.astype(BF16), w_up_e_bf=w_up_e.astype(BF16),
             w_down_e_bf=w_down_e.astype(BF16))
    caches_k = (cache_k_w128, cache_k_w512, cache_k_w2048)
    caches_v = (cache_v_w128, cache_v_w512, cache_v_w2048)
    c_all = jnp.concatenate([c_prompt, c_sample], axis=0)
    xp = x_prompt.reshape(batch * seq, d)
    xs = x_sample.reshape(db * t_new, d)
    ks, vs, ss = {'p': [], 's': []}, {'p': [], 's': []}, {'p': [], 's': []}
    for l in range(depth):
        xp, xs, nk, nv, ns = decoder_layer(l, xp, xs, c_all, p, caches_k, caches_v, state_gla, dims)
        for name in ('p', 's'):
            ks[name].append(nk[name])
            vs[name].append(nv[name])
            ss[name].append(ns[name])
    out = [xp.reshape(batch, seq, d), xs.reshape(db, t_new, d)]
    for name in ('p', 's'):
        for g in range(N_GROUPS):
            out.append(jnp.stack([k_[g] for k_ in ks[name]]))
            out.append(jnp.stack([v_[g] for v_ in vs[name]]))
        out.append(jnp.stack(ss[name]))
    return tuple(out)
```

```python
import functools

import jax
import jax.numpy as jnp
from jax import lax
from jax.experimental import pallas as pl
from jax.experimental.pallas import tpu as pltpu

F32 = jnp.float32
BF16 = jnp.bfloat16
I32 = jnp.int32
U32 = jnp.uint32

DIL_GROUPS = ((128, 1), (512, 4), (2048, 16))
N_GROUPS = 3
A_HEADS = 8
A_HEAD_DIM = 128
A_WIDTH = A_HEADS * A_HEAD_DIM
SPAN = 128
ROPE_DIM = A_HEAD_DIM // 4
ROPE_THETA = 500000.0
GLA_HEADS = 4
GLA_RANK = 16
GLA_TAU = 16.0
GLA_CHUNK = 64
N_EXPERTS = 64
TOP_K = 6
N_EXPERT_GROUPS = 8
TOPK_GROUPS = 4
ROUTED_SCALE = 2.5
EPS = 1e-6
NEG_INF = -1e30
PAST_LEN = 8192

LANE = 128
TOPK_PAD = 8
MOE_ROWS = 256
DMA_UNROLL = 8
VMEM_LIMIT = 56 * 1024 * 1024


def _cp(*sem):
    return pltpu.CompilerParams(dimension_semantics=sem, vmem_limit_bytes=VMEM_LIMIT)


def _silu(x):
    return x * jax.nn.sigmoid(x)


def _tile(n, pref):
    if n <= pref:
        return n
    t = pref
    while n % t:
        t //= 2
    return t


def _ada_kernel(c_ref, w_ref, b_ref, o_ref):
    s = _silu(c_ref[...]).astype(BF16)
    o_ref[...] = jnp.dot(s, w_ref[...].astype(BF16), preferred_element_type=F32) + b_ref[...]


def ada_mod(c, w, b, layer):
    g, d = c.shape
    n = w.shape[-1]
    tn = _tile(n, 512)
    return pl.pallas_call(
        _ada_kernel,
        out_shape=jax.ShapeDtypeStruct((g, n), F32),
        grid=(n // tn,),
        in_specs=[pl.BlockSpec((g, d), lambda j: (0, 0)),
                  pl.BlockSpec((None, d, tn), lambda j: (layer, 0, j)),
                  pl.BlockSpec((None, 1, tn), lambda j: (layer, 0, j))],
        out_specs=pl.BlockSpec((g, tn), lambda j: (0, j)),
        compiler_params=_cp("arbitrary"),
        name="ada_mod",
    )(c, w, b.reshape(b.shape[0], 1, n))


class Stream:
    def __init__(self, rows, tm, mod, rows_per_group, pos):
        self.rows, self.tm, self.mod, self.rpg, self.pos = rows, tm, mod, rows_per_group, pos
        self.d = mod.shape[-1] // 6
        self.r = mod.shape[1]

    def mod_spec(self, comp, tm, tn=None, col_of=None):
        tn = self.d if tn is None else tn
        nj = self.d // tn
        rpg, r = self.rpg, self.r

        def idx(*g):
            i, j = col_of(*g)
            grp = (i * tm) // rpg if r == 1 else 0
            return (grp, 0, comp * nj + j)
        return pl.BlockSpec((None, r, tn), idx)


def _norm_mod(x, g, sc, sh):
    y = x * lax.rsqrt(jnp.mean(x * x, axis=-1, keepdims=True) + EPS) * g
    return y * (1.0 + sc) + sh


def _norm_mod_kernel(x_ref, g_ref, sc_ref, sh_ref, o_ref):
    o_ref[...] = _norm_mod(x_ref[...], g_ref[...], sc_ref[...], sh_ref[...]).astype(o_ref.dtype)


def norm_mod(x, gain, st, comp_shift, comp_scale):
    m, d = x.shape
    tm = _tile(m, 512) if st.r == 1 else m
    col = lambda i: (i, 0)
    return pl.pallas_call(
        _norm_mod_kernel,
        out_shape=jax.ShapeDtypeStruct((m, d), BF16),
        grid=(m // tm,),
        in_specs=[pl.BlockSpec((tm, d), lambda i: (i, 0)),
                  pl.BlockSpec((1, d), lambda i: (0, 0)),
                  st.mod_spec(comp_scale, tm, col_of=col),
                  st.mod_spec(comp_shift, tm, col_of=col)],
        out_specs=pl.BlockSpec((tm, d), lambda i: (i, 0)),
        compiler_params=_cp("parallel"),
        name="norm_mod",
    )(x, gain.reshape(1, d), st.mod, st.mod)


def _mm_kernel(*refs, n_extra, epilogue, cast_w):
    x_ref, w_ref = refs[0], refs[1]
    extras = refs[2:2 + n_extra]
    o_ref = refs[2 + n_extra]
    if cast_w:
        wbf = refs[3 + n_extra]

        @pl.when(pl.program_id(1) == 0)
        def _():
            wbf[...] = w_ref[...].astype(BF16)
        w = wbf[...]
    else:
        w = w_ref[...]
    acc = jnp.dot(x_ref[...], w, preferred_element_type=F32)
    if epilogue is not None:
        acc = epilogue(acc, *[e[...] for e in extras])
    o_ref[...] = acc.astype(o_ref.dtype)


def matmul(x, w, n_cols, *, tm, tn, out_dtype, extra=(), extra_specs=(), epilogue=None, name="mm",
           layer=None):
    m, k = x.shape
    cast_w = w.dtype != BF16
    if layer is None:
        w_spec = pl.BlockSpec((k, tn), lambda j, i: (0, j))
    else:
        w_spec = pl.BlockSpec((None, k, tn), lambda j, i: (layer, 0, j))
    scratch = [pltpu.VMEM((k, tn), BF16)] if cast_w else []
    return pl.pallas_call(
        functools.partial(_mm_kernel, n_extra=len(extra), epilogue=epilogue, cast_w=cast_w),
        out_shape=jax.ShapeDtypeStruct((m, n_cols), out_dtype),
        grid=(n_cols // tn, m // tm),
        in_specs=[pl.BlockSpec((tm, k), lambda j, i: (i, 0)), w_spec] + list(extra_specs),
        out_specs=pl.BlockSpec((tm, tn), lambda j, i: (i, j)),
        scratch_shapes=scratch,
        compiler_params=_cp("parallel", "arbitrary"),
        name=name,
    )(x, w, *extra)


def _qk_prep_kernel(*refs, dils, tm, n_tiles, first_tiles, keep_rows):
    (q_ref, k_ref, v_ref, qn_ref, kn_ref, c_ref, sn_ref, sum_ref, exp_ref, swap_ref) = refs[:10]
    qkv_out = refs[10:10 + 3 * N_GROUPS]
    kc_out = refs[10 + 3 * N_GROUPS:10 + 4 * N_GROUPS]
    vc_out = refs[10 + 4 * N_GROUPS:10 + 5 * N_GROUPS]
    scr = refs[10 + 5 * N_GROUPS]
    cos, sin = c_ref[...], sn_ref[...]
    swap = swap_ref[...]
    ti = pl.program_id(0) % n_tiles

    def normed(x_ref, gain_ref):
        x = x_ref[...].astype(F32)
        ss = jnp.dot((x * x).astype(BF16), sum_ref[...], preferred_element_type=F32)
        rs = lax.rsqrt(ss * (1.0 / A_HEAD_DIM) + EPS)
        hi = rs.astype(BF16)
        lo = (rs - hi.astype(F32)).astype(BF16)
        rs_full = (jnp.dot(hi, exp_ref[...], preferred_element_type=F32)
                   + jnp.dot(lo, exp_ref[...], preferred_element_type=F32))
        return x * rs_full * gain_ref[...]

    def rotary(yh):
        partner = jnp.dot(yh.astype(BF16), swap, preferred_element_type=F32)
        return yh * cos + partner * sin

    qy = normed(q_ref, qn_ref)
    ky = normed(k_ref, kn_ref)
    n_slabs = scr.shape[0]
    slab_i = 0
    for g in range(N_GROUPS):
        d = dils[g]
        kb = keep_rows[g]
        k_heads, v_heads = [], []
        for hl in range(A_HEADS):
            h = g * A_HEADS + hl
            sl = slice(h * A_HEAD_DIM, (h + 1) * A_HEAD_DIM)
            ol = slice(hl * A_HEAD_DIM, (hl + 1) * A_HEAD_DIM)
            qh = rotary(qy[:, sl])
            kh = rotary(ky[:, sl])
            vh = v_ref[:, sl].astype(F32)
            k_heads.append(kh)
            v_heads.append(vh)
            for which, val in enumerate((qh, kh, vh)):
                out = qkv_out[3 * g + which]
                if d == 1:
                    out[0, :, ol] = val.astype(BF16)
                else:
                    slab = slab_i % n_slabs
                    slab_i += 1
                    scr[slab] = val
                    for r in range(d):
                        out[r, :, ol] = scr[slab, pl.ds(r, tm // d, stride=d), :].astype(BF16)

        def write_cache(g=g, kb=kb, k_heads=k_heads, v_heads=v_heads):
            for hl in range(A_HEADS):
                rows_of_head = pl.ds(hl, kb, stride=A_HEADS)
                kc_out[g][rows_of_head, :] = k_heads[hl][tm - kb:, :]
                vc_out[g][rows_of_head, :] = v_heads[hl][tm - kb:, :]

        if first_tiles[g] == 0:
            write_cache()
        else:
            pl.when(ti >= first_tiles[g])(write_cache)


def rope_tables(pos):
    half = ROPE_DIM // 2
    inv = ROPE_THETA ** (-jnp.arange(half, dtype=F32) * 2.0 / ROPE_DIM)
    ang = pos.astype(F32)[:, None] * inv[None, :]
    cos, sin = jnp.cos(ang), jnp.sin(ang)
    t = pos.shape[0]
    rest = A_HEAD_DIM - ROPE_DIM
    c = jnp.concatenate([cos, cos, jnp.ones((t, rest), F32)], axis=1)
    s = jnp.concatenate([-sin, sin, jnp.zeros((t, rest), F32)], axis=1)
    return c, s


def _head_matrices(n_heads):
    col = jnp.arange(n_heads * A_HEAD_DIM) // A_HEAD_DIM
    lane = jnp.arange(LANE)
    summing = (col[:, None] == lane[None, :]).astype(BF16)
    half = ROPE_DIM // 2
    src = jnp.arange(A_HEAD_DIM)
    partner = jnp.where(src < half, src + half, jnp.where(src < ROPE_DIM, src - half, -1))
    swap = (src[:, None] == partner[None, :]).astype(BF16)
    return summing, summing.T, swap


def qk_prep(proj, q_norm, k_norm, tables, dils, batch, seq, keeps):
    c, s = tables
    tm = _tile(seq, 256)
    nt = seq // tm
    qa_w = N_GROUPS * A_WIDTH
    n_heads = N_GROUPS * A_HEADS
    summing, expanding, swap = _head_matrices(n_heads)
    first_tiles, keep_rows, n_keep_blocks = [], [], []
    for keep in keeps:
        kb = min(tm, keep)
        assert keep % kb == 0 and (seq - keep) % kb == 0 and tm % kb == 0
        first_tiles.append((seq - keep) // tm)
        keep_rows.append(kb)
        n_keep_blocks.append(keep // kb)
    tab = pl.BlockSpec((tm, A_HEAD_DIM), lambda i: (i % nt, 0))
    whole = lambda a: pl.BlockSpec(a.shape, lambda i: (0, 0))
    blk = lambda j: pl.BlockSpec((tm, qa_w), lambda i: (i, j))
    gains = [jnp.tile(w.reshape(1, -1), (1, n_heads)) for w in (q_norm, k_norm)]
    out_shapes, out_specs = [], []
    for g in range(N_GROUPS):
        d = dils[g]
        for _ in range(3):
            out_shapes.append(jax.ShapeDtypeStruct((batch, d, seq // d, A_WIDTH), BF16))
            out_specs.append(pl.BlockSpec((None, d, tm // d, A_WIDTH),
                                          lambda i: (i // nt, 0, i % nt, 0)))
    for _ in range(2):
        for g in range(N_GROUPS):
            kb, nkb, first = keep_rows[g], n_keep_blocks[g], first_tiles[g]
            out_shapes.append(jax.ShapeDtypeStruct((batch * keeps[g] * A_HEADS, A_HEAD_DIM), F32))
            out_specs.append(pl.BlockSpec(
                (kb * A_HEADS, A_HEAD_DIM),
                lambda i, nkb=nkb, first=first:
                ((i // nt) * nkb + jnp.clip(i % nt - first, 0, nkb - 1), 0)))
    res = pl.pallas_call(
        functools.partial(_qk_prep_kernel, dils=tuple(dils), tm=tm, n_tiles=nt,
                          first_tiles=tuple(first_tiles), keep_rows=tuple(keep_rows)),
        out_shape=tuple(out_shapes),
        grid=(batch * nt,),
        in_specs=[blk(0), blk(1), blk(2), whole(gains[0]), whole(gains[1]), tab, tab,
                  whole(summing), whole(expanding), whole(swap)],
        out_specs=tuple(out_specs),
        scratch_shapes=[pltpu.VMEM((24, tm, A_HEAD_DIM), F32)],
        compiler_params=_cp("arbitrary"),
        name="qk_prep",
    )(proj, proj, proj, gains[0], gains[1], c, s, summing, expanding, swap)
    n3 = 3 * N_GROUPS
    return res[:n3], res[n3:n3 + N_GROUPS], res[n3 + N_GROUPS:]


def _attn_prompt_kernel(q_ref, kp_ref, kc_ref, vp_ref, vc_ref, o_ref, lse_ref, *, n_res, n_qb):
    qi = lax.broadcasted_iota(I32, (SPAN, 2 * SPAN), 0)
    ki = lax.broadcasted_iota(I32, (SPAN, 2 * SPAN), 1)
    band = (ki >= qi) & (ki <= qi + SPAN)
    band_first = band & ((pl.program_id(2) > 0) | (ki >= SPAN))
    lane = lax.broadcasted_iota(I32, (SPAN, LANE), 1)
    scale = A_HEAD_DIM ** -0.5
    for rr in range(n_res):
        for jb in range(n_qb):
            rows = slice(jb * SPAN, (jb + 1) * SPAN)
            prev = slice((jb - 1) * SPAN, jb * SPAN)
            mask = band_first if jb == 0 else band
            lse_all = jnp.zeros((SPAN, LANE), F32)
            for h in range(A_HEADS):
                sl = slice(h * A_HEAD_DIM, (h + 1) * A_HEAD_DIM)
                k_prev = kp_ref[rr, :, sl] if jb == 0 else kc_ref[rr, prev, sl]
                v_prev = vp_ref[rr, :, sl] if jb == 0 else vc_ref[rr, prev, sl]
                kh = jnp.concatenate([k_prev, kc_ref[rr, rows, sl]], axis=0)
                vh = jnp.concatenate([v_prev, vc_ref[rr, rows, sl]], axis=0)
                s = lax.dot_general(q_ref[rr, rows, sl], kh, (((1,), (1,)), ((), ())),
                                    preferred_element_type=F32) * scale
                s = jnp.where(mask, s, NEG_INF)
                m = jnp.max(s, axis=-1, keepdims=True)
                p = jnp.exp(s - m)
                l = jnp.sum(p, axis=-1, keepdims=True)
                o = jnp.dot(p.astype(BF16), vh, preferred_element_type=F32) / l
                o_ref[rr, rows, sl] = o.astype(o_ref.dtype)
                lse_all = jnp.where(lane == h, m + jnp.log(l), lse_all)
            lse_ref[rr, rows, :] = lse_all


def attn_prompt(q_g, k_g, v_g, g):
    batch, dil, l, _ = q_g.shape
    nblk = l // SPAN
    n_qb = min(4, nblk)
    n_res = min(max(4 // n_qb, 1), dil)
    assert nblk % n_qb == 0 and dil % n_res == 0
    cur = lambda w: pl.BlockSpec((None, n_res, n_qb * SPAN, w), lambda b, r, i: (b, r, i, 0))
    prev = pl.BlockSpec((None, n_res, SPAN, A_WIDTH),
                        lambda b, r, i: (b, r, jnp.maximum(i * n_qb - 1, 0), 0))
    return pl.pallas_call(
        functools.partial(_attn_prompt_kernel, n_res=n_res, n_qb=n_qb),
        out_shape=(jax.ShapeDtypeStruct((batch, dil, l, A_WIDTH), BF16),
                   jax.ShapeDtypeStruct((batch, dil, l, LANE), F32)),
        grid=(batch, dil // n_res, nblk // n_qb),
        in_specs=[cur(A_WIDTH), prev, cur(A_WIDTH), prev, cur(A_WIDTH)],
        out_specs=(cur(A_WIDTH), cur(LANE)),
        compiler_params=_cp("parallel", "parallel", "arbitrary"),
        name=f"attn_prompt_g{g}",
    )(q_g, k_g, k_g, v_g, v_g)


def _attn_sample_kernel(q_ref, k_ref, v_ref, ck0, cv0, ck1, cv1, ck2, cv2, o_ref, l_ref, *, t_new):
    caches = ((ck0, cv0), (ck1, cv1), (ck2, cv2))
    scale = A_HEAD_DIM ** -0.5
    jidx = lax.broadcasted_iota(I32, (SPAN, A_HEADS, 1), 0)
    for g, (_, dil) in enumerate(DIL_GROUPS):
        ck, cv = caches[g]
        for t in range(t_new):
            res = 0 if dil == 1 else t
            q = q_ref[t, g]
            s_c = jnp.sum(ck[:, res] * q[None], axis=-1, keepdims=True) * scale
            if dil == 1:
                s_c = jnp.where(jidx >= t, s_c, NEG_INF)
            new = range(t + 1) if dil == 1 else (t,)
            s_n = [jnp.sum(q * k_ref[u, g], axis=-1, keepdims=True) * scale for u in new]
            m = jnp.max(s_c, axis=0)
            for s in s_n:
                m = jnp.maximum(m, s)
            p_c = jnp.exp(s_c - m[None])
            lsum = jnp.sum(p_c, axis=0)
            o = jnp.sum(p_c * cv[:, res], axis=0)
            for u, s in zip(new, s_n):
                p_u = jnp.exp(s - m)
                lsum = lsum + p_u
                o = o + p_u * v_ref[u, g]
            o_ref[t, g] = o / lsum
            l_ref[t, g] = jnp.broadcast_to(m + jnp.log(lsum), (A_HEADS, A_HEAD_DIM))


def attn_sample(q5, k5, v5, caches_k, caches_v, layer):
    db, t_new = q5.shape[:2]
    new_spec = pl.BlockSpec((None, t_new, N_GROUPS, A_HEADS, A_HEAD_DIM), lambda b: (b, 0, 0, 0, 0))
    ins, specs = [q5, k5, v5], [new_spec] * 3
    for g, (win, dil) in enumerate(DIL_GROUPS):
        for c in (caches_k[g], caches_v[g]):
            depth, _, wc, hh, ee = c.shape
            assert wc == win and (hh, ee) == (A_HEADS, A_HEAD_DIM) and (dil == 1 or t_new <= dil)
            nres = 1 if dil == 1 else t_new
            ins.append(c.reshape(depth, db, wc // dil, dil, hh, ee))
            specs.append(pl.BlockSpec((None, None, SPAN, nres, hh, ee),
                                      lambda b: (layer, b, 0, 0, 0, 0)))
    return pl.pallas_call(
        functools.partial(_attn_sample_kernel, t_new=t_new),
        out_shape=(jax.ShapeDtypeStruct(q5.shape, F32), jax.ShapeDtypeStruct(q5.shape, F32)),
        grid=(db,),
        in_specs=specs,
        out_specs=(new_spec, new_spec),
        compiler_params=_cp("parallel"),
        name="attn_sample",
    )(*ins)


def _combine_kernel(*refs, dils, tm):
    o_refs, l_refs, out_ref, scr_o, scr_l = refs[:3], refs[3:6], refs[6], refs[7], refs[8]
    ls = []
    for g, d in enumerate(dils):
        if d == 1:
            ls.append(l_refs[g][0])
        else:
            for r in range(d):
                scr_l[g, pl.ds(r, tm // d, stride=d), :] = l_refs[g][r]
            ls.append(scr_l[g])
    m = jnp.maximum(jnp.maximum(ls[0], ls[1]), ls[2])
    es = [jnp.exp(x - m) for x in ls]
    den = es[0] + es[1] + es[2]
    rr = lax.broadcasted_iota(I32, (LANE, A_WIDTH), 0)
    cc = lax.broadcasted_iota(I32, (LANE, A_WIDTH), 1)
    expand = (cc // A_HEAD_DIM == rr).astype(BF16)
    wxs = []
    for e in es:
        w = e / den
        hi = w.astype(BF16)
        lo = (w - hi.astype(F32)).astype(BF16)
        wxs.append(jnp.dot(hi, expand, preferred_element_type=F32)
                   + jnp.dot(lo, expand, preferred_element_type=F32))
    for h in range(A_HEADS):
        hs = slice(h * A_HEAD_DIM, (h + 1) * A_HEAD_DIM)
        acc = None
        for g, d in enumerate(dils):
            if d == 1:
                o_tok = o_refs[g][0, :, hs].astype(F32)
            else:
                for r in range(d):
                    scr_o[g, h, pl.ds(r, tm // d, stride=d), :] = o_refs[g][r, :, hs].astype(F32)
                o_tok = scr_o[g, h]
            term = wxs[g][:, hs] * o_tok
            acc = term if acc is None else acc + term
        out_ref[:, hs] = acc.astype(out_ref.dtype)


def combine_groups(os_, ls_):
    batch = os_[0].shape[0]
    dils = tuple(o.shape[1] for o in os_)
    seq = os_[0].shape[1] * os_[0].shape[2]
    tm = _tile(seq, 256)
    nt = seq // tm
    specs = []
    for width in (A_WIDTH, LANE):
        for d in dils:
            specs.append(pl.BlockSpec((None, d, tm // d, width), lambda i: (i // nt, 0, i % nt, 0)))
    return pl.pallas_call(
        functools.partial(_combine_kernel, dils=dils, tm=tm),
        out_shape=jax.ShapeDtypeStruct((batch * seq, A_WIDTH), BF16),
        grid=(batch * nt,),
        in_specs=specs,
        out_specs=pl.BlockSpec((tm, A_WIDTH), lambda i: (i, 0)),
        scratch_shapes=[pltpu.VMEM((N_GROUPS, A_HEADS, tm, A_HEAD_DIM), F32),
                        pltpu.VMEM((N_GROUPS, tm, LANE), F32)],
        compiler_params=_cp("parallel"),
        name="combine_groups",
    )(*os_, *ls_)


def _alpha_kernel(h_ref, w1_ref, w2_ref, b_ref, o_ref):
    lr = jnp.dot(h_ref[...], w1_ref[...], preferred_element_type=F32)
    z = jnp.dot(lr.astype(BF16), w2_ref[...], preferred_element_type=F32) + b_ref[...]
    log_sig = jnp.minimum(z, 0.0) - jnp.log(1.0 + jnp.exp(-jnp.abs(z)))
    o_ref[...] = log_sig / GLA_TAU


def gla_log_decay(h, w_lr, alpha_up, alpha_bias):
    m, d = h.shape
    gq = alpha_up.shape[1]
    w1 = jnp.pad(w_lr, ((0, 0), (0, LANE - GLA_RANK))).astype(BF16)
    w2 = jnp.pad(alpha_up, ((0, LANE - GLA_RANK), (0, 0))).astype(BF16)
    tm = _tile(m, 512)
    return pl.pallas_call(
        _alpha_kernel,
        out_shape=jax.ShapeDtypeStruct((m, gq), F32),
        grid=(m // tm,),
        in_specs=[pl.BlockSpec((tm, d), lambda i: (i, 0)),
                  pl.BlockSpec((d, LANE), lambda i: (0, 0)),
                  pl.BlockSpec((LANE, gq), lambda i: (0, 0)),
                  pl.BlockSpec((1, gq), lambda i: (0, 0))],
        out_specs=pl.BlockSpec((tm, gq), lambda i: (i, 0)),
        compiler_params=_cp("parallel"),
        name="gla_log_decay",
    )(h, w1, w2, alpha_bias.reshape(1, gq))


def _gla_kernel(*refs, chunk, dk, dv, has_s0):
    if has_s0:
        q_ref, k_ref, v0_ref, v1_ref, la_ref, r0_ref, r1_ref, gn_ref, s0_ref, ob_ref, s_ref = refs
    else:
        q_ref, k_ref, v0_ref, v1_ref, la_ref, r0_ref, r1_ref, gn_ref, ob_ref, s_ref = refs

    @pl.when(pl.program_id(1) == 0)
    def _():
        s_ref[...] = s0_ref[...].astype(F32) if has_s0 else jnp.zeros(s_ref.shape, F32)

    ri = lax.broadcasted_iota(I32, (chunk, chunk), 0)
    ci = lax.broadcasted_iota(I32, (chunk, chunk), 1)
    causal = ri >= ci
    tri = causal.astype(F32)
    gn = gn_ref[...]
    per_block = GLA_HEADS // 2
    for h in range(GLA_HEADS):
        ks = slice(h * dk, (h + 1) * dk)
        vs = slice((h % per_block) * dv, (h % per_block + 1) * dv)
        v_ref, r_ref = (v0_ref, r0_ref) if h < per_block else (v1_ref, r1_ref)
        la = la_ref[:, ks]
        b = jnp.dot(tri, la, precision=lax.Precision.HIGHEST,
                    preferred_element_type=F32)
        b_last = b[chunk - 1:chunk, :]
        q = q_ref[:, ks].astype(F32) * (dk ** -0.5)
        k = k_ref[:, ks].astype(F32)
        v = v_ref[:, vs]
        q_dec = (q * jnp.exp(b)).astype(BF16)
        k_inv = (k * jnp.exp(-b)).astype(BF16)
        k_dec = (k * jnp.exp(b_last - b)).astype(BF16)
        att = lax.dot_general(q_dec, k_inv, (((1,), (1,)), ((), ())), preferred_element_type=F32)
        att = jnp.where(causal, att, 0.0).astype(BF16)
        s_old = s_ref[h]
        o = (jnp.dot(att, v, preferred_element_type=F32)
             + jnp.dot(q_dec, s_old.astype(BF16), preferred_element_type=F32))
        kv = lax.dot_general(k_dec, v, (((0,), (0,)), ((), ())), preferred_element_type=F32)
        decay = jnp.sum(jnp.transpose(la), axis=1, keepdims=True)
        s_ref[h] = jnp.exp(decay) * s_old + kv
        y = o * lax.rsqrt(jnp.mean(o * o, axis=-1, keepdims=True) + EPS) * gn
        ob_ref[:, h * dv:(h + 1) * dv] = (y * _silu(r_ref[:, vs].astype(F32))).astype(ob_ref.dtype)


def gla(proj3, la3, gla_norm, s0, layer, dk, dv):
    bsz, t, w = proj3.shape
    chunk = GLA_CHUNK
    qa_w = N_GROUPS * A_WIDTH
    gq, gv = GLA_HEADS * dk, GLA_HEADS * dv
    assert gv == 2 * gq and (3 * qa_w) % gq == 0
    base = 3 * qa_w // gq
    cspec = lambda off: pl.BlockSpec((None, chunk, gq), lambda b, c: (b, c, off))
    ins = [proj3, proj3, proj3, proj3, la3, proj3, proj3, gla_norm.reshape(1, dv)]
    specs = [cspec(base), cspec(base + 1), cspec(base + 2), cspec(base + 3), cspec(0),
             cspec(base + 4), cspec(base + 5), pl.BlockSpec((1, dv), lambda b, c: (0, 0))]
    if s0 is not None:
        ins.append(s0)
        specs.append(pl.BlockSpec((None, None, GLA_HEADS, dk, dv), lambda b, c: (layer, b, 0, 0, 0)))
    return pl.pallas_call(
        functools.partial(_gla_kernel, chunk=chunk, dk=dk, dv=dv, has_s0=s0 is not None),
        out_shape=(jax.ShapeDtypeStruct((bsz, t, gv), BF16),
                   jax.ShapeDtypeStruct((bsz, GLA_HEADS, dk, dv), F32)),
        grid=(bsz, t // chunk),
        in_specs=specs,
        out_specs=(pl.BlockSpec((None, chunk, gv), lambda b, c: (b, c, 0)),
                   pl.BlockSpec((None, GLA_HEADS, dk, dv), lambda b, c: (b, 0, 0, 0))),
        compiler_params=_cp("parallel", "arbitrary"),
        name="gla",
    )(*ins)


def _merge_kernel(h_ref, oa_ref, ob_ref, wga_ref, wgb_ref, wa_ref, wb_ref, o_ref):
    h = h_ref[...]
    ga = jnp.dot(h, wga_ref[...], preferred_element_type=F32)
    gb = jnp.dot(h, wgb_ref[...], preferred_element_type=F32)
    a = jnp.dot(oa_ref[...], wa_ref[...], preferred_element_type=F32)
    b = jnp.dot(ob_ref[...], wb_ref[...], preferred_element_type=F32)
    o_ref[...] = (jax.nn.sigmoid(ga) * a + jax.nn.sigmoid(gb) * b).astype(o_ref.dtype)


def merge_branches(h, oa, ob, w_gates, w_a, w_b, tm, layer):
    m, d = h.shape
    tn = _tile(d, 512)
    nj = d // tn
    row = lambda width: pl.BlockSpec((tm, width), lambda j, i: (i, 0))
    wcol = lambda rows, off: pl.BlockSpec((rows, tn), lambda j, i: (0, j + off))
    gcol = lambda off: pl.BlockSpec((None, d, tn), lambda j, i: (layer, 0, j + off))
    return pl.pallas_call(
        _merge_kernel,
        out_shape=jax.ShapeDtypeStruct((m, d), BF16),
        grid=(nj, m // tm),
        in_specs=[row(d), row(oa.shape[1]), row(ob.shape[1]),
                  gcol(0), gcol(nj), wcol(w_a.shape[0], 0), wcol(w_b.shape[0], 0)],
        out_specs=pl.BlockSpec((tm, tn), lambda j, i: (i, j)),
        compiler_params=_cp("parallel", "arbitrary"),
        name="merge_branches",
    )(h, oa, ob, w_gates, w_gates, w_a, w_b)


def _pack_rows(x):
    half = x.shape[1] // 2
    lo = lax.bitcast_convert_type(x[:, :half].astype(BF16).astype(F32), U32)
    hi = lax.bitcast_convert_type(x[:, half:].astype(BF16).astype(F32), U32)
    return (lo >> 16) | (hi & jnp.uint32(0xFFFF0000))


def _unpack_chunk(u):
    return (lax.bitcast_convert_type(u << 16, F32),
            lax.bitcast_convert_type(u & jnp.uint32(0xFFFF0000), F32))


def _store_rows(ref, packed, rows):
    n_chunks = packed.shape[1] // LANE
    for c in range(n_chunks):
        ref[pl.ds(c, rows, stride=n_chunks), :] = packed[:, c * LANE:(c + 1) * LANE]


def _load_rows(ref, rows, n_chunks, dtype, start=0):
    lo, hi = [], []
    for c in range(n_chunks):
        a, b = _unpack_chunk(ref[pl.ds(start + c, rows, stride=n_chunks), :])
        lo.append(a.astype(dtype))
        hi.append(b.astype(dtype))
    return jnp.concatenate(lo + hi, axis=1)


def _norm_router_kernel(x_ref, g_ref, sc_ref, sh_ref, wr_ref, rb_ref, hr_ref, idx_ref, wt_ref):
    h = _norm_mod(x_ref[...], g_ref[...], sc_ref[...], sh_ref[...])
    tm = h.shape[0]
    _store_rows(hr_ref, _pack_rows(h), tm)
    per = N_EXPERTS // N_EXPERT_GROUPS
    logits = lax.dot_general(wr_ref[...], h, (((1,), (1,)), ((), ())),
                             precision=lax.Precision.HIGHEST, preferred_element_type=F32)
    scores = jax.nn.sigmoid(logits)
    biased = scores + rb_ref[...]
    bg = biased.reshape(N_EXPERT_GROUPS, per, tm)
    ip = lax.broadcasted_iota(I32, bg.shape, 1)
    m1 = jnp.max(bg, axis=1, keepdims=True)
    i1 = jnp.min(jnp.where(bg == m1, ip, per), axis=1, keepdims=True)
    m2 = jnp.max(jnp.where(ip == i1, -jnp.inf, bg), axis=1, keepdims=True)
    grp = (m1 + m2)[:, 0, :]
    gi = lax.broadcasted_iota(I32, grp.shape, 0)
    rank = jnp.zeros(grp.shape, I32)
    for g in range(N_EXPERT_GROUPS):
        other = grp[g:g + 1, :]
        ahead = (other > grp) | ((other == grp) & (g < gi))
        rank = rank + ahead.astype(I32)
    sel = (rank < TOPK_GROUPS)[:, None, :]
    masked = jnp.where(sel, bg, -jnp.inf).reshape(N_EXPERTS, tm)
    ie = lax.broadcasted_iota(I32, masked.shape, 0)
    ids, ws = [], []
    for _ in range(TOP_K):
        mx = jnp.max(masked, axis=0, keepdims=True)
        ik = jnp.min(jnp.where(masked == mx, ie, N_EXPERTS), axis=0, keepdims=True)
        hit = ie == ik
        ids.append(ik)
        ws.append(jnp.sum(jnp.where(hit, scores, 0.0), axis=0, keepdims=True))
        masked = jnp.where(hit, -jnp.inf, masked)
    tot = ws[0]
    for wk in ws[1:]:
        tot = tot + wk
    pad = TOPK_PAD - TOP_K
    idx_ref[...] = jnp.concatenate(ids + [jnp.zeros((pad, tm), I32)], axis=0)
    wt_ref[...] = jnp.concatenate([wk / tot * ROUTED_SCALE for wk in ws]
                                  + [jnp.zeros((pad, tm), F32)], axis=0)


def norm_router(x, gain, st, comp_shift, comp_scale, w_router_t, router_bias):
    m, d = x.shape
    tm = _tile(m, 256) if st.r == 1 else m
    nc = d // 2 // LANE
    col = lambda i: (i, 0)
    e = w_router_t.shape[0]
    return pl.pallas_call(
        _norm_router_kernel,
        out_shape=(jax.ShapeDtypeStruct((m * nc, LANE), U32),
                   jax.ShapeDtypeStruct((TOPK_PAD, m), I32),
                   jax.ShapeDtypeStruct((TOPK_PAD, m), F32)),
        grid=(m // tm,),
        in_specs=[pl.BlockSpec((tm, d), lambda i: (i, 0)),
                  pl.BlockSpec((1, d), lambda i: (0, 0)),
                  st.mod_spec(comp_scale, tm, col_of=col),
                  st.mod_spec(comp_shift, tm, col_of=col),
                  pl.BlockSpec((e, d), lambda i: (0, 0)),
                  pl.BlockSpec((e, 1), lambda i: (0, 0))],
        out_specs=(pl.BlockSpec((tm * nc, LANE), lambda i: (i, 0)),
                   pl.BlockSpec((TOPK_PAD, tm), lambda i: (0, i)),
                   pl.BlockSpec((TOPK_PAD, tm), lambda i: (0, i))),
        compiler_params=_cp("parallel"),
        name="norm_router",
    )(x, gain.reshape(1, d), st.mod, st.mod, w_router_t, router_bias.reshape(e, 1))


def _row_dma_loop(n_rows, start_one):
    def body(i, carry):
        for j in range(DMA_UNROLL):
            start_one(i * DMA_UNROLL + j)
        return carry
    lax.fori_loop(0, n_rows // DMA_UNROLL, body, 0)


def _moe_kernel(be_ref, nb_ref, cnt_ref, tok_cur, tok_nxt, asg_ref, h_hbm, wg_ref, wu_ref, wd_ref,
                y_hbm, xbuf, ybuf, sem_in, sem_out):
    i = pl.program_id(0)
    n = pl.num_programs(0)
    nb = nb_ref[0]
    slot = i % 2
    nc = h_hbm.shape[1]
    rows = xbuf.shape[1] // nc
    row_at = lambda buf, s, r: buf.at[s, pl.ds(pl.multiple_of(r * nc, nc), nc)]

    def in_copy(tok_ref, s, r):
        return pltpu.make_async_copy(h_hbm.at[tok_ref[0, 0, r]], row_at(xbuf, s, r), sem_in.at[s])

    def out_copy(s, r, dst_row):
        return pltpu.make_async_copy(row_at(ybuf, s, r), y_hbm.at[dst_row], sem_out.at[s])

    def one_by_one(count, fn):
        def body(r, carry):
            fn(r)
            return carry
        lax.fori_loop(0, count, body, 0)

    def wait_gather(s):
        pltpu.make_async_copy(xbuf.at[s], xbuf.at[s], sem_in.at[s]).wait()

    def wait_scatter(block, s):
        cnt = cnt_ref[block]

        @pl.when(cnt == rows)
        def _():
            pltpu.make_async_copy(ybuf.at[s], ybuf.at[s], sem_out.at[s]).wait()

        @pl.when(cnt < rows)
        def _():
            one_by_one(cnt, lambda r: out_copy(s, r, 0).wait())

    @pl.when(i == 0)
    def _():
        _row_dma_loop(rows, lambda r: in_copy(tok_cur, 0, r).start())

    @pl.when(i < nb)
    def _():
        wait_gather(slot)

        @pl.when(i >= 2)
        def _():
            wait_scatter(i - 2, slot)
        x = _load_rows(xbuf.at[slot], rows, nc, BF16)
        f = wg_ref.shape[1]
        chunks = [(c, min(c + 256, f)) for c in range(0, f, 256)]
        per = -(-rows // (2 * len(chunks)))
        gs, us, r0 = [], [], 0
        for lo, hi in chunks:
            for acc, w_ref in ((gs, wg_ref), (us, wu_ref)):
                acc.append(jnp.dot(x, w_ref[:, lo:hi], preferred_element_type=F32))
                for r in range(r0, min(r0 + per, rows)):
                    in_copy(tok_nxt, 1 - slot, r).start()
                r0 = min(r0 + per, rows)
        assert r0 == rows
        hb = (_silu(jnp.concatenate(gs, axis=1)) * jnp.concatenate(us, axis=1)).astype(BF16)
        y = jnp.dot(hb, wd_ref[...], preferred_element_type=F32)
        _store_rows(ybuf.at[slot], _pack_rows(y), rows)
        cnt = cnt_ref[i]
        start_row = lambda r: out_copy(slot, r, asg_ref[0, 0, r]).start()

        @pl.when(cnt == rows)
        def _():
            _row_dma_loop(rows, start_row)

        @pl.when(cnt < rows)
        def _():
            one_by_one(cnt, start_row)

    @pl.when(i == nb)
    def _():
        wait_gather(slot)

    @pl.when(i == n - 1)
    def _():
        @pl.when(nb >= 1)
        def _():
            wait_scatter(nb - 1, (nb - 1) % 2)

        @pl.when(nb >= 2)
        def _():
            wait_scatter(nb - 2, nb % 2)


def moe_experts(h_rows, row_tok, row_asg, block_e, nb_used, block_cnt, wg, wu, wd, layer):
    n_blocks = block_e.shape[0]
    n_tokens, nc = h_rows.shape[:2]
    d = 2 * nc * LANE
    f = wg.shape[-1]
    tok3 = row_tok.reshape(n_blocks, 1, MOE_ROWS)
    asg3 = row_asg.reshape(n_blocks, 1, MOE_ROWS)
    smem = lambda f_: pl.BlockSpec((1, 1, MOE_ROWS), f_, memory_space=pltpu.SMEM)
    last = n_blocks - 1
    wspec = lambda a, b: pl.BlockSpec(
        (None, None, a, b), lambda i, be, nb, cnt: (layer, be[jnp.minimum(i, last)], 0, 0))
    grid_spec = pltpu.PrefetchScalarGridSpec(
        num_scalar_prefetch=3,
        grid=(n_blocks + 1,),
        in_specs=[smem(lambda i, be, nb, cnt: (jnp.minimum(i, last), 0, 0)),
                  smem(lambda i, be, nb, cnt: (jnp.minimum(i + 1, last), 0, 0)),
                  smem(lambda i, be, nb, cnt: (jnp.minimum(i, last), 0, 0)),
                  pl.BlockSpec(memory_space=pl.ANY),
                  wspec(d, f), wspec(d, f), wspec(f, d)],
        out_specs=pl.BlockSpec(memory_space=pl.ANY),
        scratch_shapes=[pltpu.VMEM((2, MOE_ROWS * nc, LANE), U32),
                        pltpu.VMEM((2, MOE_ROWS * nc, LANE), U32),
                        pltpu.SemaphoreType.DMA((2,)), pltpu.SemaphoreType.DMA((2,))],
    )
    return pl.pallas_call(
        _moe_kernel,
        out_shape=jax.ShapeDtypeStruct((n_tokens * TOP_K, nc, LANE), U32),
        grid_spec=grid_spec,
        compiler_params=_cp("arbitrary"),
        name="moe_experts",
    )(block_e, nb_used, block_cnt, tok3, tok3, asg3, h_rows, wg, wu, wd)


def moe_dispatch(idx_ts):
    sizes = [ix.shape[1] for ix in idx_ts]
    n_tokens = sum(sizes)
    flat_e = jnp.concatenate([ix.reshape(-1) for ix in idx_ts])
    nk = n_tokens * TOP_K
    order = jnp.argsort(flat_e).astype(I32)
    experts = jnp.arange(N_EXPERTS, dtype=I32)
    counts = jnp.sum((flat_e[None, :] == experts[:, None]).astype(I32), axis=1)
    start = jnp.cumsum(counts) - counts
    padded = (counts + MOE_ROWS - 1) // MOE_ROWS * MOE_ROWS
    pend = jnp.cumsum(padded)
    pad_start = pend - padded
    n_blocks = -(-nk // MOE_ROWS) + N_EXPERTS
    blk_first = jnp.arange(n_blocks, dtype=I32) * MOE_ROWS
    block_e = jnp.minimum(jnp.sum((pend[None, :] <= blk_first[:, None]).astype(I32), axis=1),
                          N_EXPERTS - 1)
    nb_used = (pend[-1] // MOE_ROWS).astype(I32).reshape(1)
    p = jnp.arange(n_blocks * MOE_ROWS, dtype=I32)
    e_p = jnp.repeat(block_e, MOE_ROWS)
    k_p = p - pad_start[e_p]
    valid = (k_p < counts[e_p]) & (p < pend[-1])
    a_p = order[jnp.clip(start[e_p] + k_p, 0, nk - 1)]
    tok = jnp.zeros_like(a_p)
    base_a, base_t = 0, 0
    for m in sizes:
        local = a_p - base_a
        tok = jnp.where((local >= 0) & (local < TOP_K * m), base_t + local % m, tok)
        base_a += TOP_K * m
        base_t += m
    row_tok = jnp.where(valid, tok, 0).astype(I32)
    row_asg = jnp.where(valid, a_p, 0).astype(I32)
    block_cnt = jnp.sum(valid.reshape(n_blocks, MOE_ROWS).astype(I32), axis=1)
    return row_tok, row_asg, block_e.astype(I32), nb_used, block_cnt


def _finish_kernel(*refs):
    h_ref = refs[0]
    y_refs = refs[1:1 + TOP_K]
    wt_ref, x_ref, g2_ref, wg_ref, wu_ref, wd_ref, o_ref = refs[1 + TOP_K:]
    tm, d = x_ref.shape
    nc = d // 2 // LANE
    h = _load_rows(h_ref, tm, nc, BF16)
    g = jnp.dot(h, wg_ref[...], preferred_element_type=F32)
    u = jnp.dot(h, wu_ref[...], preferred_element_type=F32)
    acc = jnp.dot((_silu(g) * u).astype(BF16), wd_ref[...], preferred_element_type=F32)
    wt = wt_ref[...]
    for k in range(TOP_K):
        acc = acc + wt[:, k:k + 1] * _load_rows(y_refs[k], tm, nc, F32)
    o_ref[...] = x_ref[...] + g2_ref[...] * acc


def moe_finish(x1, st, h_rows2, wts, y_rows2, wgs, wus, wds, row_off, asg_off):
    m, d = x1.shape
    f = wgs.shape[1]
    tm = _tile(m, 256)
    nt = m // tm
    assert row_off % tm == 0 and asg_off % tm == 0
    off = row_off // tm
    nc = d // 2 // LANE
    rows_spec = lambda base: pl.BlockSpec((tm * nc, LANE), lambda i: (i + base, 0))
    whole = lambda a, b: pl.BlockSpec((a, b), lambda i: (0, 0), pipeline_mode=pl.Buffered(1))
    return pl.pallas_call(
        _finish_kernel,
        out_shape=jax.ShapeDtypeStruct((m, d), F32),
        grid=(nt,),
        in_specs=[rows_spec(off)] + [rows_spec(asg_off // tm + k * nt) for k in range(TOP_K)]
                 + [pl.BlockSpec((tm, TOPK_PAD), lambda i: (i + off, 0)),
                    pl.BlockSpec((tm, d), lambda i: (i, 0)),
                    st.mod_spec(5, tm, col_of=lambda i: (i, 0)),
                    whole(d, f), whole(d, f), whole(f, d)],
        out_specs=pl.BlockSpec((tm, d), lambda i: (i, 0)),
        compiler_params=_cp("parallel"),
        name="moe_finish",
    )(h_rows2, *([y_rows2] * TOP_K), wts, x1, st.mod, wgs, wus, wds)


def _residual_epilogue(acc, x, gate):
    return x + gate * acc


def decoder_layer(l, xp, xs, c_all, p, caches_k, caches_v, state_gla, dims):
    batch, seq, db, t_new = dims
    d = xp.shape[1]
    mp, ms = xp.shape[0], xs.shape[0]
    n_all = mp + ms
    qa_w = N_GROUPS * A_WIDTH
    dk, dv = d // 2 // GLA_HEADS, d // GLA_HEADS
    gq, gv = GLA_HEADS * dk, GLA_HEADS * dv
    n_main = 3 * qa_w + 2 * gq + 2 * gv
    nc = d // 2 // LANE

    mod = ada_mod(c_all, p['w_ada'], p['b_ada'], l)
    st_p = Stream(mp, _tile(mp, 1024), mod[:batch].reshape(batch, 1, 6 * d), seq,
                  jnp.arange(seq))
    st_s = Stream(ms, ms, jnp.repeat(mod[batch:], t_new, axis=0).reshape(1, ms, 6 * d), 0,
                  jnp.tile(PAST_LEN + jnp.arange(t_new), db))
    w_lr = p['w_lr'][l]
    w_a, w_b = p['w_branch_a'][l].astype(BF16), p['w_branch_b'][l].astype(BF16)
    w_out = p['w_out'][l].astype(BF16)
    w_router_t = p['w_router'][l].T

    new_k, new_v, new_s, x1s, idxs, wtss, hrs = {}, {}, {}, {}, [], [], []
    for name, st, x in (('p', st_p, xp), ('s', st_s, xs)):
        m, tm = st.rows, st.tm
        h1 = norm_mod(x, p['norm_mix'][l], st, 0, 1)
        proj = matmul(h1, p['w_in'], n_main, tm=tm, tn=_tile(n_main, 1024), out_dtype=BF16,
                      name="in_proj", layer=l)
        la = gla_log_decay(h1, w_lr, p['alpha_up'][l], p['alpha_bias'][l])
        tables = rope_tables(st.pos)
        if name == 'p':
            dils = [dil for _, dil in DIL_GROUPS]
            keeps = [min(w_, seq) for w_, _ in DIL_GROUPS]
            qkv, kc, vc = qk_prep(proj, p['q_norm'][l], p['k_norm'][l], tables, dils, batch, seq,
                                  keeps)
            outs, lses = zip(*[attn_prompt(qkv[3 * g], qkv[3 * g + 1], qkv[3 * g + 2], g)
                               for g in range(N_GROUPS)])
            oa = combine_groups(outs, lses)
            ob, s_new = gla(proj.reshape(batch, seq, -1), la.reshape(batch, seq, gq),
                            p['gla_norm'][l], None, l, dk, dv)
            ob = ob.reshape(m, gv)
            shape4 = lambda a, keep: a.reshape(batch, keep, A_HEADS, A_HEAD_DIM)
            new_k[name] = [shape4(a, keep) for a, keep in zip(kc, keeps)]
            new_v[name] = [shape4(a, keep) for a, keep in zip(vc, keeps)]
        else:
            qkv, kc, vc = qk_prep(proj, p['q_norm'][l], p['k_norm'][l], tables, [1] * N_GROUPS,
                                  1, m, [m] * N_GROUPS)
            new5 = lambda which: jnp.stack(
                [qkv[3 * g + which].reshape(db, t_new, A_HEADS, A_HEAD_DIM)
                 for g in range(N_GROUPS)], axis=2).astype(F32)
            o5, l5 = attn_sample(new5(0), new5(1), new5(2), caches_k, caches_v, l)
            outs = [o5[:, :, g].reshape(1, 1, m, A_WIDTH).astype(BF16) for g in range(N_GROUPS)]
            lses = [jnp.pad(l5[:, :, g, :, 0].reshape(m, A_HEADS),
                            ((0, 0), (0, LANE - A_HEADS))).reshape(1, 1, m, LANE)
                    for g in range(N_GROUPS)]
            oa = combine_groups(outs, lses)
            padc = lambda a: jnp.pad(a.reshape(db, t_new, -1),
                                     ((0, 0), (0, GLA_CHUNK - t_new), (0, 0)))
            ob, s_new = gla(padc(proj), padc(la), p['gla_norm'][l], state_gla, l, dk, dv)
            ob = ob[:, :t_new].reshape(m, gv)
            shape4 = lambda a: a.reshape(db, t_new, A_HEADS, A_HEAD_DIM)
            new_k[name] = [shape4(a) for a in kc]
            new_v[name] = [shape4(a) for a in vc]
        new_s[name] = s_new
        merged = merge_branches(h1, oa, ob, p['w_gates'], w_a, w_b, tm, l)
        tn = _tile(d, 1024)
        x1 = matmul(merged, w_out, d, tm=tm, tn=tn, out_dtype=F32,
                    extra=(x, st.mod),
                    extra_specs=(pl.BlockSpec((tm, tn), lambda j, i: (i, j)),
                                 st.mod_spec(2, tm, tn, col_of=lambda j, i: (i, j))),
                    epilogue=_residual_epilogue, name="out_proj")
        h_rows, idx_t, wt_t = norm_router(x1, p['norm_ffn'][l], st, 3, 4, w_router_t,
                                          p['router_bias'][l])
        x1s[name] = x1
        hrs.append(h_rows)
        idxs.append(idx_t[:TOP_K])
        wtss.append(wt_t)

    wts_all = jnp.concatenate(wtss, axis=1).T
    h_rows2 = jnp.concatenate(hrs, axis=0)
    row_tok, row_asg, block_e, nb_used, block_cnt = moe_dispatch(idxs)
    y_rows = moe_experts(h_rows2.reshape(n_all, nc, LANE), row_tok, row_asg, block_e, nb_used,
                         block_cnt, p['w_gate_e_bf'], p['w_up_e_bf'], p['w_down_e_bf'], l)
    y_rows2 = y_rows.reshape(-1, LANE)
    wgs, wus, wds = (p['w_gate_s'][l].astype(BF16), p['w_up_s'][l].astype(BF16),
                     p['w_down_s'][l].astype(BF16))
    xp2 = moe_finish(x1s['p'], st_p, h_rows2, wts_all, y_rows2, wgs, wus, wds, 0, 0)
    xs2 = moe_finish(x1s['s'], st_s, h_rows2, wts_all, y_rows2, wgs, wus, wds, mp, TOP_K * mp)
    return xp2, xs2, new_k, new_v, new_s


def kernel(x_prompt, x_sample, cache_k_w128, cache_v_w128, cache_k_w512, cache_v_w512, cache_k_w2048, cache_v_w2048, state_gla, c_prompt, c_sample, w_in, q_norm, k_norm, alpha_up, alpha_bias, gla_norm, w_branch_a, w_branch_b, w_out, norm_mix, norm_ffn, w_ada, b_ada, w_router, router_bias, w_gate_e, w_up_e, w_down_e, w_gate_s, w_up_s, w_down_s):
    batch, seq, d = x_prompt.shape
    db, t_new, _ = x_sample.shape
    depth = w_in.shape[0]
    dims = (batch, seq, db, t_new)
    n_main = w_in.shape[-1] - GLA_RANK - 2 * d
    p = dict(w_in=w_in, w_lr=w_in[:, :, n_main:n_main + GLA_RANK],
             w_gates=w_in[:, :, n_main + GLA_RANK:].astype(BF16),
             q_norm=q_norm, k_norm=k_norm, alpha_up=alpha_up, alpha_bias=alpha_bias,
             gla_norm=gla_norm, w_branch_a=w_branch_a, w_branch_b=w_branch_b, w_out=w_out,
             norm_mix=norm_mix, norm_ffn=norm_ffn, w_ada=w_ada, b_ada=b_ada, w_router=w_router,
             router_bias=router_bias, w_gate_s=w_gate_s, w_up_s=w_up_s, w_down_s=w_down_s,
             w_gate_e_bf=w_gate_e.astype(BF16), w_up_e_bf=w_up_e.astype(BF16),
             w_down_e_bf=w_down_e.astype(BF16))
    caches_k = (cache_k_w128, cache_k_w512, cache_k_w2048)
    caches_v = (cache_v_w128, cache_v_w512, cache_v_w2048)
    c_all = jnp.concatenate([c_prompt, c_sample], axis=0)
    xp = x_prompt.reshape(batch * seq, d)
    xs = x_sample.reshape(db * t_new, d)
    ks, vs, ss = {'p': [], 's': []}, {'p': [], 's': []}, {'p': [], 's': []}
    for l in range(depth):
        xp, xs, nk, nv, ns = decoder_layer(l, xp, xs, c_all, p, caches_k, caches_v, state_gla, dims)
        for name in ('p', 's'):
            ks[name].append(nk[name])
            vs[name].append(nv[name])
            ss[name].append(ns[name])
    out = [xp.reshape(batch, seq, d), xs.reshape(db, t_new, d)]
    for name in ('p', 's'):
        for g in range(N_GROUPS):
            out.append(jnp.stack([k_[g] for k_ in ks[name]]))
            out.append(jnp.stack([v_[g] for v_ in vs[name]]))
        out.append(jnp.stack(ss[name]))
    return tuple(out)
```

```python
import functools

import jax
import jax.numpy as jnp
from jax import lax
from jax.experimental import pallas as pl
from jax.experimental.pallas import tpu as pltpu

F32 = jnp.float32
BF16 = jnp.bfloat16
I32 = jnp.int32
U32 = jnp.uint32

DIL_GROUPS = ((128, 1), (512, 4), (2048, 16))
N_GROUPS = 3
A_HEADS = 8
A_HEAD_DIM = 128
A_WIDTH = A_HEADS * A_HEAD_DIM
SPAN = 128
ROPE_DIM = A_HEAD_DIM // 4
ROPE_THETA = 500000.0
GLA_HEADS = 4
GLA_RANK = 16
GLA_TAU = 16.0
GLA_CHUNK = 64
N_EXPERTS = 64
TOP_K = 6
N_EXPERT_GROUPS = 8
TOPK_GROUPS = 4
ROUTED_SCALE = 2.5
EPS = 1e-6
NEG_INF = -1e30
PAST_LEN = 8192

LANE = 128
TOPK_PAD = 8
MOE_ROWS = 256
DMA_UNROLL = 32
VMEM_LIMIT = 56 * 1024 * 1024
MOE_VMEM_LIMIT = 60 * 1024 * 1024


def _cp(*sem):
    return pltpu.CompilerParams(dimension_semantics=sem, vmem_limit_bytes=VMEM_LIMIT)


def _silu(x):
    return x * jax.nn.sigmoid(x)


def _tile(n, pref):
    if n <= pref:
        return n
    t = pref
    while n % t:
        t //= 2
    return t


def _ada_kernel(c_ref, w_ref, b_ref, o_ref):
    s = _silu(c_ref[...]).astype(BF16)
    o_ref[...] = jnp.dot(s, w_ref[...].astype(BF16), preferred_element_type=F32) + b_ref[...]


def ada_mod(c, w, b, layer):
    g, d = c.shape
    n = w.shape[-1]
    tn = _tile(n, 512)
    return pl.pallas_call(
        _ada_kernel,
        out_shape=jax.ShapeDtypeStruct((g, n), F32),
        grid=(n // tn,),
        in_specs=[pl.BlockSpec((g, d), lambda j: (0, 0)),
                  pl.BlockSpec((None, d, tn), lambda j: (layer, 0, j)),
                  pl.BlockSpec((None, 1, tn), lambda j: (layer, 0, j))],
        out_specs=pl.BlockSpec((g, tn), lambda j: (0, j)),
        compiler_params=_cp("arbitrary"),
        name="ada_mod",
    )(c, w, b.reshape(b.shape[0], 1, n))


class Stream:
    def __init__(self, rows, tm, mod, rows_per_group, pos):
        self.rows, self.tm, self.mod, self.rpg, self.pos = rows, tm, mod, rows_per_group, pos
        self.d = mod.shape[-1] // 6
        self.r = mod.shape[1]

    def mod_spec(self, comp, tm, tn=None, col_of=None):
        tn = self.d if tn is None else tn
        nj = self.d // tn
        rpg, r = self.rpg, self.r

        def idx(*g):
            i, j = col_of(*g)
            grp = (i * tm) // rpg if r == 1 else 0
            return (grp, 0, comp * nj + j)
        return pl.BlockSpec((None, r, tn), idx)


def _norm_mod(x, g, sc, sh):
    y = x * lax.rsqrt(jnp.mean(x * x, axis=-1, keepdims=True) + EPS) * g
    return y * (1.0 + sc) + sh


def _norm_mod_kernel(x_ref, g_ref, sc_ref, sh_ref, o_ref):
    o_ref[...] = _norm_mod(x_ref[...], g_ref[...], sc_ref[...], sh_ref[...]).astype(o_ref.dtype)


def norm_mod(x, gain, st, comp_shift, comp_scale):
    m, d = x.shape
    tm = _tile(m, 512) if st.r == 1 else m
    col = lambda i: (i, 0)
    return pl.pallas_call(
        _norm_mod_kernel,
        out_shape=jax.ShapeDtypeStruct((m, d), BF16),
        grid=(m // tm,),
        in_specs=[pl.BlockSpec((tm, d), lambda i: (i, 0)),
                  pl.BlockSpec((1, d), lambda i: (0, 0)),
                  st.mod_spec(comp_scale, tm, col_of=col),
                  st.mod_spec(comp_shift, tm, col_of=col)],
        out_specs=pl.BlockSpec((tm, d), lambda i: (i, 0)),
        compiler_params=_cp("parallel"),
        name="norm_mod",
    )(x, gain.reshape(1, d), st.mod, st.mod)


def _mm_kernel(*refs, n_extra, epilogue, cast_w):
    x_ref, w_ref = refs[0], refs[1]
    extras = refs[2:2 + n_extra]
    o_ref = refs[2 + n_extra]
    if cast_w:
        wbf = refs[3 + n_extra]

        @pl.when(pl.program_id(1) == 0)
        def _():
            wbf[...] = w_ref[...].astype(BF16)
        w = wbf[...]
    else:
        w = w_ref[...]
    acc = jnp.dot(x_ref[...], w, preferred_element_type=F32)
    if epilogue is not None:
        acc = epilogue(acc, *[e[...] for e in extras])
    o_ref[...] = acc.astype(o_ref.dtype)


def matmul(x, w, n_cols, *, tm, tn, out_dtype, extra=(), extra_specs=(), epilogue=None, name="mm",
           layer=None):
    m, k = x.shape
    cast_w = w.dtype != BF16
    if layer is None:
        w_spec = pl.BlockSpec((k, tn), lambda j, i: (0, j))
    else:
        w_spec = pl.BlockSpec((None, k, tn), lambda j, i: (layer, 0, j))
    scratch = [pltpu.VMEM((k, tn), BF16)] if cast_w else []
    return pl.pallas_call(
        functools.partial(_mm_kernel, n_extra=len(extra), epilogue=epilogue, cast_w=cast_w),
        out_shape=jax.ShapeDtypeStruct((m, n_cols), out_dtype),
        grid=(n_cols // tn, m // tm),
        in_specs=[pl.BlockSpec((tm, k), lambda j, i: (i, 0)), w_spec] + list(extra_specs),
        out_specs=pl.BlockSpec((tm, tn), lambda j, i: (i, j)),
        scratch_shapes=scratch,
        compiler_params=_cp("parallel", "arbitrary"),
        name=name,
    )(x, w, *extra)


def _qk_prep_kernel(*refs, dils, tm, n_tiles, first_tiles, keep_rows):
    (q_ref, k_ref, v_ref, qn_ref, kn_ref, c_ref, sn_ref, sum_ref, exp_ref, swap_ref) = refs[:10]
    qkv_out = refs[10:10 + 3 * N_GROUPS]
    kc_out = refs[10 + 3 * N_GROUPS:10 + 4 * N_GROUPS]
    vc_out = refs[10 + 4 * N_GROUPS:10 + 5 * N_GROUPS]
    scr = refs[10 + 5 * N_GROUPS]
    cos, sin = c_ref[...], sn_ref[...]
    swap = swap_ref[...]
    ti = pl.program_id(0) % n_tiles

    def normed(x_ref, gain_ref):
        x = x_ref[...].astype(F32)
        ss = jnp.dot((x * x).astype(BF16), sum_ref[...], preferred_element_type=F32)
        rs = lax.rsqrt(ss * (1.0 / A_HEAD_DIM) + EPS)
        hi = rs.astype(BF16)
        lo = (rs - hi.astype(F32)).astype(BF16)
        rs_full = (jnp.dot(hi, exp_ref[...], preferred_element_type=F32)
                   + jnp.dot(lo, exp_ref[...], preferred_element_type=F32))
        return x * rs_full * gain_ref[...]

    def rotary(yh):
        partner = jnp.dot(yh.astype(BF16), swap, preferred_element_type=F32)
        return yh * cos + partner * sin

    qy = normed(q_ref, qn_ref)
    ky = normed(k_ref, kn_ref)
    n_slabs = scr.shape[0]
    slab_i = 0
    for g in range(N_GROUPS):
        d = dils[g]
        kb = keep_rows[g]
        k_heads, v_heads = [], []
        for hl in range(A_HEADS):
            h = g * A_HEADS + hl
            sl = slice(h * A_HEAD_DIM, (h + 1) * A_HEAD_DIM)
            ol = slice(hl * A_HEAD_DIM, (hl + 1) * A_HEAD_DIM)
            qh = rotary(qy[:, sl])
            kh = rotary(ky[:, sl])
            vh = v_ref[:, sl].astype(F32)
            k_heads.append(kh)
            v_heads.append(vh)
            for which, val in enumerate((qh, kh, vh)):
                out = qkv_out[3 * g + which]
                if d == 1:
                    out[0, :, ol] = val.astype(BF16)
                else:
                    slab = slab_i % n_slabs
                    slab_i += 1
                    scr[slab] = val
                    for r in range(d):
                        out[r, :, ol] = scr[slab, pl.ds(r, tm // d, stride=d), :].astype(BF16)

        def write_cache(g=g, kb=kb, k_heads=k_heads, v_heads=v_heads):
            for hl in range(A_HEADS):
                rows_of_head = pl.ds(hl, kb, stride=A_HEADS)
                kc_out[g][rows_of_head, :] = k_heads[hl][tm - kb:, :]
                vc_out[g][rows_of_head, :] = v_heads[hl][tm - kb:, :]

        if first_tiles[g] == 0:
            write_cache()
        else:
            pl.when(ti >= first_tiles[g])(write_cache)


def rope_tables(pos):
    half = ROPE_DIM // 2
    inv = ROPE_THETA ** (-jnp.arange(half, dtype=F32) * 2.0 / ROPE_DIM)
    ang = pos.astype(F32)[:, None] * inv[None, :]
    cos, sin = jnp.cos(ang), jnp.sin(ang)
    t = pos.shape[0]
    rest = A_HEAD_DIM - ROPE_DIM
    c = jnp.concatenate([cos, cos, jnp.ones((t, rest), F32)], axis=1)
    s = jnp.concatenate([-sin, sin, jnp.zeros((t, rest), F32)], axis=1)
    return c, s


def _head_matrices(n_heads):
    col = jnp.arange(n_heads * A_HEAD_DIM) // A_HEAD_DIM
    lane = jnp.arange(LANE)
    summing = (col[:, None] == lane[None, :]).astype(BF16)
    half = ROPE_DIM // 2
    src = jnp.arange(A_HEAD_DIM)
    partner = jnp.where(src < half, src + half, jnp.where(src < ROPE_DIM, src - half, -1))
    swap = (src[:, None] == partner[None, :]).astype(BF16)
    return summing, summing.T, swap


def qk_prep(proj, q_norm, k_norm, tables, dils, batch, seq, keeps):
    c, s = tables
    tm = _tile(seq, 256)
    nt = seq // tm
    qa_w = N_GROUPS * A_WIDTH
    n_heads = N_GROUPS * A_HEADS
    summing, expanding, swap = _head_matrices(n_heads)
    first_tiles, keep_rows, n_keep_blocks = [], [], []
    for keep in keeps:
        kb = min(tm, keep)
        assert keep % kb == 0 and (seq - keep) % kb == 0 and tm % kb == 0
        first_tiles.append((seq - keep) // tm)
        keep_rows.append(kb)
        n_keep_blocks.append(keep // kb)
    tab = pl.BlockSpec((tm, A_HEAD_DIM), lambda i: (i % nt, 0))
    whole = lambda a: pl.BlockSpec(a.shape, lambda i: (0, 0))
    blk = lambda j: pl.BlockSpec((tm, qa_w), lambda i: (i, j))
    gains = [jnp.tile(w.reshape(1, -1), (1, n_heads)) for w in (q_norm, k_norm)]
    out_shapes, out_specs = [], []
    for g in range(N_GROUPS):
        d = dils[g]
        for _ in range(3):
            out_shapes.append(jax.ShapeDtypeStruct((batch, d, seq // d, A_WIDTH), BF16))
            out_specs.append(pl.BlockSpec((None, d, tm // d, A_WIDTH),
                                          lambda i: (i // nt, 0, i % nt, 0)))
    for _ in range(2):
        for g in range(N_GROUPS):
            kb, nkb, first = keep_rows[g], n_keep_blocks[g], first_tiles[g]
            out_shapes.append(jax.ShapeDtypeStruct((batch * keeps[g] * A_HEADS, A_HEAD_DIM), F32))
            out_specs.append(pl.BlockSpec(
                (kb * A_HEADS, A_HEAD_DIM),
                lambda i, nkb=nkb, first=first:
                ((i // nt) * nkb + jnp.clip(i % nt - first, 0, nkb - 1), 0)))
    res = pl.pallas_call(
        functools.partial(_qk_prep_kernel, dils=tuple(dils), tm=tm, n_tiles=nt,
                          first_tiles=tuple(first_tiles), keep_rows=tuple(keep_rows)),
        out_shape=tuple(out_shapes),
        grid=(batch * nt,),
        in_specs=[blk(0), blk(1), blk(2), whole(gains[0]), whole(gains[1]), tab, tab,
                  whole(summing), whole(expanding), whole(swap)],
        out_specs=tuple(out_specs),
        scratch_shapes=[pltpu.VMEM((24, tm, A_HEAD_DIM), F32)],
        compiler_params=_cp("arbitrary"),
        name="qk_prep",
    )(proj, proj, proj, gains[0], gains[1], c, s, summing, expanding, swap)
    n3 = 3 * N_GROUPS
    return res[:n3], res[n3:n3 + N_GROUPS], res[n3 + N_GROUPS:]


def _attn_prompt_kernel(q_ref, kp_ref, kc_ref, vp_ref, vc_ref, o_ref, lse_ref, *, n_res, n_qb):
    qi = lax.broadcasted_iota(I32, (SPAN, 2 * SPAN), 0)
    ki = lax.broadcasted_iota(I32, (SPAN, 2 * SPAN), 1)
    band = (ki >= qi) & (ki <= qi + SPAN)
    band_first = band & ((pl.program_id(2) > 0) | (ki >= SPAN))
    lane = lax.broadcasted_iota(I32, (SPAN, LANE), 1)
    scale = A_HEAD_DIM ** -0.5
    for rr in range(n_res):
        for jb in range(n_qb):
            rows = slice(jb * SPAN, (jb + 1) * SPAN)
            prev = slice((jb - 1) * SPAN, jb * SPAN)
            mask = band_first if jb == 0 else band
            lse_all = jnp.zeros((SPAN, LANE), F32)
            for h in range(A_HEADS):
                sl = slice(h * A_HEAD_DIM, (h + 1) * A_HEAD_DIM)
                k_prev = kp_ref[rr, :, sl] if jb == 0 else kc_ref[rr, prev, sl]
                v_prev = vp_ref[rr, :, sl] if jb == 0 else vc_ref[rr, prev, sl]
                kh = jnp.concatenate([k_prev, kc_ref[rr, rows, sl]], axis=0)
                vh = jnp.concatenate([v_prev, vc_ref[rr, rows, sl]], axis=0)
                s = lax.dot_general(q_ref[rr, rows, sl], kh, (((1,), (1,)), ((), ())),
                                    preferred_element_type=F32) * scale
                s = jnp.where(mask, s, NEG_INF)
                m = jnp.max(s, axis=-1, keepdims=True)
                p = jnp.exp(s - m)
                l = jnp.sum(p, axis=-1, keepdims=True)
                o = jnp.dot(p.astype(BF16), vh, preferred_element_type=F32) / l
                o_ref[rr, rows, sl] = o.astype(o_ref.dtype)
                lse_all = jnp.where(lane == h, m + jnp.log(l), lse_all)
            lse_ref[rr, rows, :] = lse_all


def attn_prompt(q_g, k_g, v_g, g):
    batch, dil, l, _ = q_g.shape
    nblk = l // SPAN
    n_qb = min(4, nblk)
    n_res = min(max(4 // n_qb, 1), dil)
    assert nblk % n_qb == 0 and dil % n_res == 0
    cur = lambda w: pl.BlockSpec((None, n_res, n_qb * SPAN, w), lambda b, r, i: (b, r, i, 0))
    prev = pl.BlockSpec((None, n_res, SPAN, A_WIDTH),
                        lambda b, r, i: (b, r, jnp.maximum(i * n_qb - 1, 0), 0))
    return pl.pallas_call(
        functools.partial(_attn_prompt_kernel, n_res=n_res, n_qb=n_qb),
        out_shape=(jax.ShapeDtypeStruct((batch, dil, l, A_WIDTH), BF16),
                   jax.ShapeDtypeStruct((batch, dil, l, LANE), F32)),
        grid=(batch, dil // n_res, nblk // n_qb),
        in_specs=[cur(A_WIDTH), prev, cur(A_WIDTH), prev, cur(A_WIDTH)],
        out_specs=(cur(A_WIDTH), cur(LANE)),
        compiler_params=_cp("parallel", "parallel", "arbitrary"),
        name=f"attn_prompt_g{g}",
    )(q_g, k_g, k_g, v_g, v_g)


def _attn_sample_kernel(q_ref, k_ref, v_ref, ck0, cv0, ck1, cv1, ck2, cv2, o_ref, l_ref, *, t_new):
    caches = ((ck0, cv0), (ck1, cv1), (ck2, cv2))
    scale = A_HEAD_DIM ** -0.5
    jidx = lax.broadcasted_iota(I32, (SPAN, A_HEADS, 1), 0)
    for g, (_, dil) in enumerate(DIL_GROUPS):
        ck, cv = caches[g]
        for t in range(t_new):
            res = 0 if dil == 1 else t
            q = q_ref[t, g]
            s_c = jnp.sum(ck[:, res] * q[None], axis=-1, keepdims=True) * scale
            if dil == 1:
                s_c = jnp.where(jidx >= t, s_c, NEG_INF)
            new = range(t + 1) if dil == 1 else (t,)
            s_n = [jnp.sum(q * k_ref[u, g], axis=-1, keepdims=True) * scale for u in new]
            m = jnp.max(s_c, axis=0)
            for s in s_n:
                m = jnp.maximum(m, s)
            p_c = jnp.exp(s_c - m[None])
            lsum = jnp.sum(p_c, axis=0)
            o = jnp.sum(p_c * cv[:, res], axis=0)
            for u, s in zip(new, s_n):
                p_u = jnp.exp(s - m)
                lsum = lsum + p_u
                o = o + p_u * v_ref[u, g]
            o_ref[t, g] = o / lsum
            l_ref[t, g] = jnp.broadcast_to(m + jnp.log(lsum), (A_HEADS, A_HEAD_DIM))


def attn_sample(q5, k5, v5, caches_k, caches_v, layer):
    db, t_new = q5.shape[:2]
    new_spec = pl.BlockSpec((None, t_new, N_GROUPS, A_HEADS, A_HEAD_DIM), lambda b: (b, 0, 0, 0, 0))
    ins, specs = [q5, k5, v5], [new_spec] * 3
    for g, (win, dil) in enumerate(DIL_GROUPS):
        for c in (caches_k[g], caches_v[g]):
            depth, _, wc, hh, ee = c.shape
            assert wc == win and (hh, ee) == (A_HEADS, A_HEAD_DIM) and (dil == 1 or t_new <= dil)
            nres = 1 if dil == 1 else t_new
            ins.append(c.reshape(depth, db, wc // dil, dil, hh, ee))
            specs.append(pl.BlockSpec((None, None, SPAN, nres, hh, ee),
                                      lambda b: (layer, b, 0, 0, 0, 0)))
    return pl.pallas_call(
        functools.partial(_attn_sample_kernel, t_new=t_new),
        out_shape=(jax.ShapeDtypeStruct(q5.shape, F32), jax.ShapeDtypeStruct(q5.shape, F32)),
        grid=(db,),
        in_specs=specs,
        out_specs=(new_spec, new_spec),
        compiler_params=_cp("parallel"),
        name="attn_sample",
    )(*ins)


def _combine_kernel(*refs, dils, tm):
    o_refs, l_refs, out_ref, scr_o, scr_l = refs[:3], refs[3:6], refs[6], refs[7], refs[8]
    ls = []
    for g, d in enumerate(dils):
        if d == 1:
            ls.append(l_refs[g][0])
        else:
            for r in range(d):
                scr_l[g, pl.ds(r, tm // d, stride=d), :] = l_refs[g][r]
            ls.append(scr_l[g])
    m = jnp.maximum(jnp.maximum(ls[0], ls[1]), ls[2])
    es = [jnp.exp(x - m) for x in ls]
    den = es[0] + es[1] + es[2]
    rr = lax.broadcasted_iota(I32, (LANE, A_WIDTH), 0)
    cc = lax.broadcasted_iota(I32, (LANE, A_WIDTH), 1)
    expand = (cc // A_HEAD_DIM == rr).astype(BF16)
    wxs = []
    for e in es:
        w = e / den
        hi = w.astype(BF16)
        lo = (w - hi.astype(F32)).astype(BF16)
        wxs.append(jnp.dot(hi, expand, preferred_element_type=F32)
                   + jnp.dot(lo, expand, preferred_element_type=F32))
    for h in range(A_HEADS):
        hs = slice(h * A_HEAD_DIM, (h + 1) * A_HEAD_DIM)
        acc = None
        for g, d in enumerate(dils):
            if d == 1:
                o_tok = o_refs[g][0, :, hs].astype(F32)
            else:
                for r in range(d):
                    scr_o[g, h, pl.ds(r, tm // d, stride=d), :] = o_refs[g][r, :, hs].astype(F32)
                o_tok = scr_o[g, h]
            term = wxs[g][:, hs] * o_tok
            acc = term if acc is None else acc + term
        out_ref[:, hs] = acc.astype(out_ref.dtype)


def combine_groups(os_, ls_):
    batch = os_[0].shape[0]
    dils = tuple(o.shape[1] for o in os_)
    seq = os_[0].shape[1] * os_[0].shape[2]
    tm = _tile(seq, 256)
    nt = seq // tm
    specs = []
    for width in (A_WIDTH, LANE):
        for d in dils:
            specs.append(pl.BlockSpec((None, d, tm // d, width), lambda i: (i // nt, 0, i % nt, 0)))
    return pl.pallas_call(
        functools.partial(_combine_kernel, dils=dils, tm=tm),
        out_shape=jax.ShapeDtypeStruct((batch * seq, A_WIDTH), BF16),
        grid=(batch * nt,),
        in_specs=specs,
        out_specs=pl.BlockSpec((tm, A_WIDTH), lambda i: (i, 0)),
        scratch_shapes=[pltpu.VMEM((N_GROUPS, A_HEADS, tm, A_HEAD_DIM), F32),
                        pltpu.VMEM((N_GROUPS, tm, LANE), F32)],
        compiler_params=_cp("parallel"),
        name="combine_groups",
    )(*os_, *ls_)


def _alpha_kernel(h_ref, w1_ref, w2_ref, b_ref, o_ref):
    lr = jnp.dot(h_ref[...], w1_ref[...], preferred_element_type=F32)
    z = jnp.dot(lr.astype(BF16), w2_ref[...], preferred_element_type=F32) + b_ref[...]
    log_sig = jnp.minimum(z, 0.0) - jnp.log(1.0 + jnp.exp(-jnp.abs(z)))
    o_ref[...] = log_sig / GLA_TAU


def gla_log_decay(h, w_lr, alpha_up, alpha_bias):
    m, d = h.shape
    gq = alpha_up.shape[1]
    w1 = jnp.pad(w_lr, ((0, 0), (0, LANE - GLA_RANK))).astype(BF16)
    w2 = jnp.pad(alpha_up, ((0, LANE - GLA_RANK), (0, 0))).astype(BF16)
    tm = _tile(m, 512)
    return pl.pallas_call(
        _alpha_kernel,
        out_shape=jax.ShapeDtypeStruct((m, gq), F32),
        grid=(m // tm,),
        in_specs=[pl.BlockSpec((tm, d), lambda i: (i, 0)),
                  pl.BlockSpec((d, LANE), lambda i: (0, 0)),
                  pl.BlockSpec((LANE, gq), lambda i: (0, 0)),
                  pl.BlockSpec((1, gq), lambda i: (0, 0))],
        out_specs=pl.BlockSpec((tm, gq), lambda i: (i, 0)),
        compiler_params=_cp("parallel"),
        name="gla_log_decay",
    )(h, w1, w2, alpha_bias.reshape(1, gq))


def _gla_kernel(*refs, chunk, dk, dv, has_s0):
    if has_s0:
        q_ref, k_ref, v0_ref, v1_ref, la_ref, r0_ref, r1_ref, gn_ref, s0_ref, ob_ref, s_ref = refs
    else:
        q_ref, k_ref, v0_ref, v1_ref, la_ref, r0_ref, r1_ref, gn_ref, ob_ref, s_ref = refs

    @pl.when(pl.program_id(1) == 0)
    def _():
        s_ref[...] = s0_ref[...].astype(F32) if has_s0 else jnp.zeros(s_ref.shape, F32)

    ri = lax.broadcasted_iota(I32, (chunk, chunk), 0)
    ci = lax.broadcasted_iota(I32, (chunk, chunk), 1)
    causal = ri >= ci
    tri = causal.astype(F32)
    gn = gn_ref[...]
    per_block = GLA_HEADS // 2
    for h in range(GLA_HEADS):
        ks = slice(h * dk, (h + 1) * dk)
        vs = slice((h % per_block) * dv, (h % per_block + 1) * dv)
        v_ref, r_ref = (v0_ref, r0_ref) if h < per_block else (v1_ref, r1_ref)
        la = la_ref[:, ks]
        b = jnp.dot(tri, la, precision=lax.Precision.HIGHEST,
                    preferred_element_type=F32)
        b_last = b[chunk - 1:chunk, :]
        q = q_ref[:, ks].astype(F32) * (dk ** -0.5)
        k = k_ref[:, ks].astype(F32)
        v = v_ref[:, vs]
        q_dec = (q * jnp.exp(b)).astype(BF16)
        k_inv = (k * jnp.exp(-b)).astype(BF16)
        k_dec = (k * jnp.exp(b_last - b)).astype(BF16)
        att = lax.dot_general(q_dec, k_inv, (((1,), (1,)), ((), ())), preferred_element_type=F32)
        att = jnp.where(causal, att, 0.0).astype(BF16)
        s_old = s_ref[h]
        o = (jnp.dot(att, v, preferred_element_type=F32)
             + jnp.dot(q_dec, s_old.astype(BF16), preferred_element_type=F32))
        kv = lax.dot_general(k_dec, v, (((0,), (0,)), ((), ())), preferred_element_type=F32)
        decay = jnp.sum(jnp.transpose(la), axis=1, keepdims=True)
        s_ref[h] = jnp.exp(decay) * s_old + kv
        y = o * lax.rsqrt(jnp.mean(o * o, axis=-1, keepdims=True) + EPS) * gn
        ob_ref[:, h * dv:(h + 1) * dv] = (y * _silu(r_ref[:, vs].astype(F32))).astype(ob_ref.dtype)


def gla(proj3, la3, gla_norm, s0, layer, dk, dv):
    bsz, t, w = proj3.shape
    chunk = GLA_CHUNK
    qa_w = N_GROUPS * A_WIDTH
    gq, gv = GLA_HEADS * dk, GLA_HEADS * dv
    assert gv == 2 * gq and (3 * qa_w) % gq == 0
    base = 3 * qa_w // gq
    cspec = lambda off: pl.BlockSpec((None, chunk, gq), lambda b, c: (b, c, off))
    ins = [proj3, proj3, proj3, proj3, la3, proj3, proj3, gla_norm.reshape(1, dv)]
    specs = [cspec(base), cspec(base + 1), cspec(base + 2), cspec(base + 3), cspec(0),
             cspec(base + 4), cspec(base + 5), pl.BlockSpec((1, dv), lambda b, c: (0, 0))]
    if s0 is not None:
        ins.append(s0)
        specs.append(pl.BlockSpec((None, None, GLA_HEADS, dk, dv), lambda b, c: (layer, b, 0, 0, 0)))
    return pl.pallas_call(
        functools.partial(_gla_kernel, chunk=chunk, dk=dk, dv=dv, has_s0=s0 is not None),
        out_shape=(jax.ShapeDtypeStruct((bsz, t, gv), BF16),
                   jax.ShapeDtypeStruct((bsz, GLA_HEADS, dk, dv), F32)),
        grid=(bsz, t // chunk),
        in_specs=specs,
        out_specs=(pl.BlockSpec((None, chunk, gv), lambda b, c: (b, c, 0)),
                   pl.BlockSpec((None, GLA_HEADS, dk, dv), lambda b, c: (b, 0, 0, 0))),
        compiler_params=_cp("parallel", "arbitrary"),
        name="gla",
    )(*ins)


def _gate_weights_kernel(w_ref, nxt_ref, o_ref, *, shift):
    tn = o_ref.shape[1]
    both = jnp.concatenate([w_ref[...], nxt_ref[...]], axis=1)
    o_ref[...] = both[:, shift:shift + tn].astype(o_ref.dtype)


def gate_weights(w_in, col0, n_cols):
    depth, k, n = w_in.shape
    tn = _tile(n_cols, 512)
    base = col0 // LANE * LANE
    shift = col0 - base
    assert base % tn == 0 and n_cols % tn == 0 and col0 + n_cols <= n
    return pl.pallas_call(
        functools.partial(_gate_weights_kernel, shift=shift),
        out_shape=jax.ShapeDtypeStruct((depth, k, n_cols), BF16),
        grid=(depth, n_cols // tn),
        in_specs=[pl.BlockSpec((None, k, tn), lambda l, j: (l, 0, base // tn + j)),
                  pl.BlockSpec((None, k, LANE), lambda l, j: (l, 0, (base + (j + 1) * tn) // LANE))],
        out_specs=pl.BlockSpec((None, k, tn), lambda l, j: (l, 0, j)),
        compiler_params=_cp("parallel", "parallel"),
        name="gate_weights",
    )(w_in, w_in)


def _merge_kernel(h_ref, oa_ref, ob_ref, wga_ref, wgb_ref, wa_ref, wb_ref, o_ref):
    h = h_ref[...]
    ga = jnp.dot(h, wga_ref[...], preferred_element_type=F32)
    gb = jnp.dot(h, wgb_ref[...], preferred_element_type=F32)
    a = jnp.dot(oa_ref[...], wa_ref[...], preferred_element_type=F32)
    b = jnp.dot(ob_ref[...], wb_ref[...], preferred_element_type=F32)
    o_ref[...] = (jax.nn.sigmoid(ga) * a + jax.nn.sigmoid(gb) * b).astype(o_ref.dtype)


def merge_branches(h, oa, ob, w_gates, w_a, w_b, tm, layer):
    m, d = h.shape
    tn = _tile(d, 512)
    nj = d // tn
    row = lambda width: pl.BlockSpec((tm, width), lambda j, i: (i, 0))
    wcol = lambda rows, off: pl.BlockSpec((rows, tn), lambda j, i: (0, j + off))
    gcol = lambda off: pl.BlockSpec((None, d, tn), lambda j, i: (layer, 0, j + off))
    return pl.pallas_call(
        _merge_kernel,
        out_shape=jax.ShapeDtypeStruct((m, d), BF16),
        grid=(nj, m // tm),
        in_specs=[row(d), row(oa.shape[1]), row(ob.shape[1]),
                  gcol(0), gcol(nj), wcol(w_a.shape[0], 0), wcol(w_b.shape[0], 0)],
        out_specs=pl.BlockSpec((tm, tn), lambda j, i: (i, j)),
        compiler_params=_cp("parallel", "arbitrary"),
        name="merge_branches",
    )(h, oa, ob, w_gates, w_gates, w_a, w_b)


def _pack_rows(x):
    half = x.shape[1] // 2
    lo = lax.bitcast_convert_type(x[:, :half].astype(BF16).astype(F32), U32)
    hi = lax.bitcast_convert_type(x[:, half:].astype(BF16).astype(F32), U32)
    return (lo >> 16) | (hi & jnp.uint32(0xFFFF0000))


def _unpack_chunk(u):
    return (lax.bitcast_convert_type(u << 16, F32),
            lax.bitcast_convert_type(u & jnp.uint32(0xFFFF0000), F32))


def _store_rows(ref, packed, rows):
    n_chunks = packed.shape[1] // LANE
    for c in range(n_chunks):
        ref[pl.ds(c, rows, stride=n_chunks), :] = packed[:, c * LANE:(c + 1) * LANE]


def _load_rows(ref, rows, n_chunks, dtype, start=0):
    lo, hi = [], []
    for c in range(n_chunks):
        a, b = _unpack_chunk(ref[pl.ds(start + c, rows, stride=n_chunks), :])
        lo.append(a.astype(dtype))
        hi.append(b.astype(dtype))
    return jnp.concatenate(lo + hi, axis=1)


def _norm_router_kernel(x_ref, g_ref, sc_ref, sh_ref, wr_ref, rb_ref, hr_ref, idx_ref, wt_ref,
                        cnt_ref):
    h = _norm_mod(x_ref[...], g_ref[...], sc_ref[...], sh_ref[...])
    tm = h.shape[0]
    _store_rows(hr_ref, _pack_rows(h), tm)
    per = N_EXPERTS // N_EXPERT_GROUPS
    logits = lax.dot_general(wr_ref[...], h, (((1,), (1,)), ((), ())),
                             precision=lax.Precision.HIGHEST, preferred_element_type=F32)
    scores = jax.nn.sigmoid(logits)
    biased = scores + rb_ref[...]
    bg = biased.reshape(N_EXPERT_GROUPS, per, tm)
    ip = lax.broadcasted_iota(I32, bg.shape, 1)
    m1 = jnp.max(bg, axis=1, keepdims=True)
    i1 = jnp.min(jnp.where(bg == m1, ip, per), axis=1, keepdims=True)
    m2 = jnp.max(jnp.where(ip == i1, -jnp.inf, bg), axis=1, keepdims=True)
    grp = (m1 + m2)[:, 0, :]
    gi = lax.broadcasted_iota(I32, grp.shape, 0)
    rank = jnp.zeros(grp.shape, I32)
    for g in range(N_EXPERT_GROUPS):
        other = grp[g:g + 1, :]
        ahead = (other > grp) | ((other == grp) & (g < gi))
        rank = rank + ahead.astype(I32)
    sel = (rank < TOPK_GROUPS)[:, None, :]
    masked = jnp.where(sel, bg, -jnp.inf).reshape(N_EXPERTS, tm)
    ie = lax.broadcasted_iota(I32, masked.shape, 0)
    ids, ws = [], []
    chosen = jnp.zeros(masked.shape, F32)
    for _ in range(TOP_K):
        mx = jnp.max(masked, axis=0, keepdims=True)
        ik = jnp.min(jnp.where(masked == mx, ie, N_EXPERTS), axis=0, keepdims=True)
        hit = ie == ik
        ids.append(ik)
        ws.append(jnp.sum(jnp.where(hit, scores, 0.0), axis=0, keepdims=True))
        masked = jnp.where(hit, -jnp.inf, masked)
        chosen = chosen + hit.astype(F32)
    cnt_ref[...] = jnp.sum(chosen, axis=1, keepdims=True).astype(I32)
    tot = ws[0]
    for wk in ws[1:]:
        tot = tot + wk
    pad = TOPK_PAD - TOP_K
    idx_ref[...] = jnp.concatenate(ids + [jnp.zeros((pad, tm), I32)], axis=0)
    wt_ref[...] = jnp.concatenate([wk / tot * ROUTED_SCALE for wk in ws]
                                  + [jnp.zeros((pad, tm), F32)], axis=0)


def norm_router(x, gain, st, comp_shift, comp_scale, w_router_t, router_bias):
    m, d = x.shape
    tm = _tile(m, 256) if st.r == 1 else m
    nc = d // 2 // LANE
    col = lambda i: (i, 0)
    e = w_router_t.shape[0]
    return pl.pallas_call(
        _norm_router_kernel,
        out_shape=(jax.ShapeDtypeStruct((m * nc, LANE), U32),
                   jax.ShapeDtypeStruct((TOPK_PAD, m), I32),
                   jax.ShapeDtypeStruct((TOPK_PAD, m), F32),
                   jax.ShapeDtypeStruct((m // tm, e, 1), I32)),
        grid=(m // tm,),
        in_specs=[pl.BlockSpec((tm, d), lambda i: (i, 0)),
                  pl.BlockSpec((1, d), lambda i: (0, 0)),
                  st.mod_spec(comp_scale, tm, col_of=col),
                  st.mod_spec(comp_shift, tm, col_of=col),
                  pl.BlockSpec((e, d), lambda i: (0, 0)),
                  pl.BlockSpec((e, 1), lambda i: (0, 0))],
        out_specs=(pl.BlockSpec((tm * nc, LANE), lambda i: (i, 0)),
                   pl.BlockSpec((TOPK_PAD, tm), lambda i: (0, i)),
                   pl.BlockSpec((TOPK_PAD, tm), lambda i: (0, i)),
                   pl.BlockSpec((None, e, 1), lambda i: (i, 0, 0))),
        compiler_params=_cp("parallel"),
        name="norm_router",
    )(x, gain.reshape(1, d), st.mod, st.mod, w_router_t, router_bias.reshape(e, 1))


def _row_dma_loop(n_rows, start_one):
    def body(i, carry):
        for j in range(DMA_UNROLL):
            start_one(i * DMA_UNROLL + j)
        return carry
    lax.fori_loop(0, n_rows // DMA_UNROLL, body, 0)


def _moe_kernel(be_ref, nb_ref, cnt_ref, first_ref, enext_ref, tok_cur, tok_nxt, asg_ref, h_hbm,
                wg_hbm, wu_hbm, wd_hbm, y_hbm, xbuf, ybuf, stage_a, stage_b, wg_bf, wu_bf, wd_bf,
                sem_in, sem_out, sem_a, sem_b, *, layer):
    i = pl.program_id(0)
    n = pl.num_programs(0)
    nb = nb_ref[0]
    slot = i % 2
    nc = h_hbm.shape[1]
    rows = xbuf.shape[1] // nc
    d_half = stage_a.shape[1]
    f_half = stage_b.shape[1]
    row_at = lambda buf, s, r: buf.at[s, pl.ds(pl.multiple_of(r * nc, nc), nc)]

    def gather(tok_ref, s):
        _row_dma_loop(rows, lambda r: pltpu.make_async_copy(
            h_hbm.at[tok_ref[0, 0, r]], row_at(xbuf, s, r), sem_in.at[s]).start())

    def out_copy(s, r, dst_row):
        return pltpu.make_async_copy(row_at(ybuf, s, r), y_hbm.at[dst_row], sem_out.at[s])

    def one_by_one(count, fn):
        def body(r, carry):
            fn(r)
            return carry
        lax.fori_loop(0, count, body, 0)

    def wait_scatter(block, s):
        cnt = cnt_ref[block]

        @pl.when(cnt == rows)
        def _():
            pltpu.make_async_copy(ybuf.at[s], ybuf.at[s], sem_out.at[s]).wait()

        @pl.when(cnt < rows)
        def _():
            one_by_one(cnt, lambda r: out_copy(s, r, 0).wait())

    def up_copy(w_hbm, e, h):
        return pltpu.make_async_copy(w_hbm.at[layer, e, pl.ds(h * d_half, d_half)], stage_a.at[h],
                                     sem_a.at[h])

    def down_copy(e, h):
        return pltpu.make_async_copy(wd_hbm.at[layer, e, pl.ds(h * f_half, f_half)], stage_b.at[h],
                                     sem_b.at[h])

    def prefetch(e):
        for h in range(2):
            up_copy(wg_hbm, e, h).start()
            down_copy(e, h).start()

    @pl.when(i == 0)
    def _():
        gather(tok_cur, 0)
        prefetch(be_ref[0])

    @pl.when(i + 1 < nb)
    def _():
        gather(tok_nxt, 1 - slot)

    @pl.when(i < nb)
    def _():
        e = be_ref[i]

        @pl.when(first_ref[i] == 1)
        def _():
            for h in range(2):
                up_copy(wg_hbm, e, h).wait()
                wg_bf[pl.ds(h * d_half, d_half), :] = stage_a[h].astype(BF16)
                up_copy(wu_hbm, e, h).start()
            for h in range(2):
                down_copy(e, h).wait()
                wd_bf[pl.ds(h * f_half, f_half), :] = stage_b[h].astype(BF16)
            for h in range(2):
                up_copy(wu_hbm, e, h).wait()
                wu_bf[pl.ds(h * d_half, d_half), :] = stage_a[h].astype(BF16)

            @pl.when(enext_ref[i] >= 0)
            def _():
                prefetch(enext_ref[i])

        pltpu.make_async_copy(xbuf.at[slot], xbuf.at[slot], sem_in.at[slot]).wait()

        @pl.when(i >= 2)
        def _():
            wait_scatter(i - 2, slot)
        x = _load_rows(xbuf.at[slot], rows, nc, BF16)
        g = jnp.dot(x, wg_bf[...], preferred_element_type=F32)
        u = jnp.dot(x, wu_bf[...], preferred_element_type=F32)
        hb = (_silu(g) * u).astype(BF16)
        y = jnp.dot(hb, wd_bf[...], preferred_element_type=F32)
        _store_rows(ybuf.at[slot], _pack_rows(y), rows)
        cnt = cnt_ref[i]
        start_row = lambda r: out_copy(slot, r, asg_ref[0, 0, r]).start()

        @pl.when(cnt == rows)
        def _():
            _row_dma_loop(rows, start_row)

        @pl.when(cnt < rows)
        def _():
            one_by_one(cnt, start_row)

    @pl.when(i == n - 1)
    def _():
        @pl.when(nb >= 1)
        def _():
            wait_scatter(nb - 1, (nb - 1) % 2)

        @pl.when(nb >= 2)
        def _():
            wait_scatter(nb - 2, nb % 2)


def moe_experts(h_rows, row_tok, row_asg, block_e, nb_used, block_cnt, wg, wu, wd, layer):
    n_blocks = block_e.shape[0]
    n_tokens, nc = h_rows.shape[:2]
    d = 2 * nc * LANE
    f = wg.shape[-1]
    tok3 = row_tok.reshape(n_blocks, 1, MOE_ROWS)
    asg3 = row_asg.reshape(n_blocks, 1, MOE_ROWS)
    idx = jnp.arange(n_blocks, dtype=I32)
    in_use = idx < nb_used[0]
    prev_e = jnp.concatenate([jnp.full((1,), -1, I32), block_e[:-1]])
    first = (in_use & (block_e != prev_e)).astype(I32)
    later_first = (first[None, :] == 1) & (idx[None, :] > idx[:, None])
    nxt_pos = jnp.min(jnp.where(later_first, idx[None, :], n_blocks), axis=1)
    e_next = jnp.where(nxt_pos < n_blocks, block_e[jnp.minimum(nxt_pos, n_blocks - 1)], -1).astype(I32)
    smem = lambda f_: pl.BlockSpec((1, 1, MOE_ROWS), f_, memory_space=pltpu.SMEM)
    last = n_blocks - 1
    any_spec = pl.BlockSpec(memory_space=pl.ANY)
    grid_spec = pltpu.PrefetchScalarGridSpec(
        num_scalar_prefetch=5,
        grid=(n_blocks,),
        in_specs=[smem(lambda i, *_: (i, 0, 0)),
                  smem(lambda i, *_: (jnp.minimum(i + 1, last), 0, 0)),
                  smem(lambda i, *_: (i, 0, 0)),
                  any_spec, any_spec, any_spec, any_spec],
        out_specs=any_spec,
        scratch_shapes=[pltpu.VMEM((2, MOE_ROWS * nc, LANE), U32),
                        pltpu.VMEM((2, MOE_ROWS * nc, LANE), U32),
                        pltpu.VMEM((2, d // 2, f), F32),
                        pltpu.VMEM((2, f // 2, d), F32),
                        pltpu.VMEM((d, f), BF16), pltpu.VMEM((d, f), BF16), pltpu.VMEM((f, d), BF16),
                        pltpu.SemaphoreType.DMA((2,)), pltpu.SemaphoreType.DMA((2,)),
                        pltpu.SemaphoreType.DMA((2,)), pltpu.SemaphoreType.DMA((2,))],
    )
    return pl.pallas_call(
        functools.partial(_moe_kernel, layer=layer),
        out_shape=jax.ShapeDtypeStruct((n_tokens * TOP_K, nc, LANE), U32),
        grid_spec=grid_spec,
        compiler_params=pltpu.CompilerParams(dimension_semantics=("arbitrary",),
                                             vmem_limit_bytes=MOE_VMEM_LIMIT),
        name="moe_experts",
    )(block_e, nb_used, block_cnt, first, e_next, tok3, tok3, asg3, h_rows, wg, wu, wd)


def moe_dispatch(idx_ts, counts):
    sizes = [ix.shape[1] for ix in idx_ts]
    n_tokens = sum(sizes)
    flat_e = jnp.concatenate([ix.reshape(-1) for ix in idx_ts])
    nk = n_tokens * TOP_K
    order = jnp.argsort(flat_e).astype(I32)
    start = jnp.cumsum(counts) - counts
    padded = (counts + MOE_ROWS - 1) // MOE_ROWS * MOE_ROWS
    pend = jnp.cumsum(padded)
    pad_start = pend - padded
    n_blocks = -(-nk // MOE_ROWS) + N_EXPERTS
    blk_first = jnp.arange(n_blocks, dtype=I32) * MOE_ROWS
    block_e = jnp.minimum(jnp.sum((pend[None, :] <= blk_first[:, None]).astype(I32), axis=1),
                          N_EXPERTS - 1)
    nb_used = (pend[-1] // MOE_ROWS).astype(I32).reshape(1)
    p = jnp.arange(n_blocks * MOE_ROWS, dtype=I32)
    e_p = jnp.repeat(block_e, MOE_ROWS)
    k_p = p - pad_start[e_p]
    valid = (k_p < counts[e_p]) & (p < pend[-1])
    a_p = order[jnp.clip(start[e_p] + k_p, 0, nk - 1)]
    tok = jnp.zeros_like(a_p)
    base_a, base_t = 0, 0
    for m in sizes:
        local = a_p - base_a
        tok = jnp.where((local >= 0) & (local < TOP_K * m), base_t + local % m, tok)
        base_a += TOP_K * m
        base_t += m
    row_tok = jnp.where(valid, tok, 0).astype(I32)
    row_asg = jnp.where(valid, a_p, 0).astype(I32)
    block_cnt = jnp.sum(valid.reshape(n_blocks, MOE_ROWS).astype(I32), axis=1)
    return row_tok, row_asg, block_e.astype(I32), nb_used, block_cnt


def _finish_kernel(*refs):
    h_ref = refs[0]
    y_refs = refs[1:1 + TOP_K]
    wt_ref, x_ref, g2_ref, wg_ref, wu_ref, wd_ref, o_ref = refs[1 + TOP_K:]
    tm, d = x_ref.shape
    nc = d // 2 // LANE
    h = _load_rows(h_ref, tm, nc, BF16)
    g = jnp.dot(h, wg_ref[...], preferred_element_type=F32)
    u = jnp.dot(h, wu_ref[...], preferred_element_type=F32)
    acc = jnp.dot((_silu(g) * u).astype(BF16), wd_ref[...], preferred_element_type=F32)
    wt = wt_ref[...]
    for k in range(TOP_K):
        acc = acc + wt[:, k:k + 1] * _load_rows(y_refs[k], tm, nc, F32)
    o_ref[...] = x_ref[...] + g2_ref[...] * acc


def moe_finish(x1, st, h_rows2, wts, y_rows2, wgs, wus, wds, row_off, asg_off):
    m, d = x1.shape
    f = wgs.shape[1]
    tm = _tile(m, 256)
    nt = m // tm
    assert row_off % tm == 0 and asg_off % tm == 0
    off = row_off // tm
    nc = d // 2 // LANE
    rows_spec = lambda base: pl.BlockSpec((tm * nc, LANE), lambda i: (i + base, 0))
    whole = lambda a, b: pl.BlockSpec((a, b), lambda i: (0, 0), pipeline_mode=pl.Buffered(1))
    return pl.pallas_call(
        _finish_kernel,
        out_shape=jax.ShapeDtypeStruct((m, d), F32),
        grid=(nt,),
        in_specs=[rows_spec(off)] + [rows_spec(asg_off // tm + k * nt) for k in range(TOP_K)]
                 + [pl.BlockSpec((tm, TOPK_PAD), lambda i: (i + off, 0)),
                    pl.BlockSpec((tm, d), lambda i: (i, 0)),
                    st.mod_spec(5, tm, col_of=lambda i: (i, 0)),
                    whole(d, f), whole(d, f), whole(f, d)],
        out_specs=pl.BlockSpec((tm, d), lambda i: (i, 0)),
        compiler_params=_cp("parallel"),
        name="moe_finish",
    )(h_rows2, *([y_rows2] * TOP_K), wts, x1, st.mod, wgs, wus, wds)


def _residual_epilogue(acc, x, gate):
    return x + gate * acc


def decoder_layer(l, xp, xs, c_all, p, caches_k, caches_v, state_gla, dims):
    batch, seq, db, t_new = dims
    d = xp.shape[1]
    mp, ms = xp.shape[0], xs.shape[0]
    n_all = mp + ms
    qa_w = N_GROUPS * A_WIDTH
    dk, dv = d // 2 // GLA_HEADS, d // GLA_HEADS
    gq, gv = GLA_HEADS * dk, GLA_HEADS * dv
    n_main = 3 * qa_w + 2 * gq + 2 * gv
    nc = d // 2 // LANE

    mod = ada_mod(c_all, p['w_ada'], p['b_ada'], l)
    st_p = Stream(mp, _tile(mp, 1024), mod[:batch].reshape(batch, 1, 6 * d), seq,
                  jnp.arange(seq))
    st_s = Stream(ms, ms, jnp.repeat(mod[batch:], t_new, axis=0).reshape(1, ms, 6 * d), 0,
                  jnp.tile(PAST_LEN + jnp.arange(t_new), db))
    w_lr = p['w_lr'][l]
    w_a, w_b = p['w_branch_a'][l].astype(BF16), p['w_branch_b'][l].astype(BF16)
    w_out = p['w_out'][l].astype(BF16)
    w_router_t = p['w_router'][l].T

    new_k, new_v, new_s, x1s, idxs, wtss, hrs, cnts = {}, {}, {}, {}, [], [], [], []
    for name, st, x in (('p', st_p, xp), ('s', st_s, xs)):
        m, tm = st.rows, st.tm
        h1 = norm_mod(x, p['norm_mix'][l], st, 0, 1)
        proj = matmul(h1, p['w_in'], n_main, tm=tm, tn=_tile(n_main, 1024), out_dtype=BF16,
                      name="in_proj", layer=l)
        la = gla_log_decay(h1, w_lr, p['alpha_up'][l], p['alpha_bias'][l])
        tables = rope_tables(st.pos)
        if name == 'p':
            dils = [dil for _, dil in DIL_GROUPS]
            keeps = [min(w_, seq) for w_, _ in DIL_GROUPS]
            qkv, kc, vc = qk_prep(proj, p['q_norm'][l], p['k_norm'][l], tables, dils, batch, seq,
                                  keeps)
            outs, lses = zip(*[attn_prompt(qkv[3 * g], qkv[3 * g + 1], qkv[3 * g + 2], g)
                               for g in range(N_GROUPS)])
            oa = combine_groups(outs, lses)
            ob, s_new = gla(proj.reshape(batch, seq, -1), la.reshape(batch, seq, gq),
                            p['gla_norm'][l], None, l, dk, dv)
            ob = ob.reshape(m, gv)
            shape4 = lambda a, keep: a.reshape(batch, keep, A_HEADS, A_HEAD_DIM)
            new_k[name] = [shape4(a, keep) for a, keep in zip(kc, keeps)]
            new_v[name] = [shape4(a, keep) for a, keep in zip(vc, keeps)]
        else:
            qkv, kc, vc = qk_prep(proj, p['q_norm'][l], p['k_norm'][l], tables, [1] * N_GROUPS,
                                  1, m, [m] * N_GROUPS)
            new5 = lambda which: jnp.stack(
                [qkv[3 * g + which].reshape(db, t_new, A_HEADS, A_HEAD_DIM)
                 for g in range(N_GROUPS)], axis=2).astype(F32)
            o5, l5 = attn_sample(new5(0), new5(1), new5(2), caches_k, caches_v, l)
            outs = [o5[:, :, g].reshape(1, 1, m, A_WIDTH).astype(BF16) for g in range(N_GROUPS)]
            lses = [jnp.pad(l5[:, :, g, :, 0].reshape(m, A_HEADS),
                            ((0, 0), (0, LANE - A_HEADS))).reshape(1, 1, m, LANE)
                    for g in range(N_GROUPS)]
            oa = combine_groups(outs, lses)
            padc = lambda a: jnp.pad(a.reshape(db, t_new, -1),
                                     ((0, 0), (0, GLA_CHUNK - t_new), (0, 0)))
            ob, s_new = gla(padc(proj), padc(la), p['gla_norm'][l], state_gla, l, dk, dv)
            ob = ob[:, :t_new].reshape(m, gv)
            shape4 = lambda a: a.reshape(db, t_new, A_HEADS, A_HEAD_DIM)
            new_k[name] = [shape4(a) for a in kc]
            new_v[name] = [shape4(a) for a in vc]
        new_s[name] = s_new
        merged = merge_branches(h1, oa, ob, p['w_gates'], w_a, w_b, tm, l)
        tn = _tile(d, 1024)
        x1 = matmul(merged, w_out, d, tm=tm, tn=tn, out_dtype=F32,
                    extra=(x, st.mod),
                    extra_specs=(pl.BlockSpec((tm, tn), lambda j, i: (i, j)),
                                 st.mod_spec(2, tm, tn, col_of=lambda j, i: (i, j))),
                    epilogue=_residual_epilogue, name="out_proj")
        h_rows, idx_t, wt_t, tile_cnt = norm_router(x1, p['norm_ffn'][l], st, 3, 4, w_router_t,
                                                    p['router_bias'][l])
        x1s[name] = x1
        hrs.append(h_rows)
        cnts.append(jnp.sum(tile_cnt[:, :, 0], axis=0))
        idxs.append(idx_t[:TOP_K])
        wtss.append(wt_t)

    wts_all = jnp.concatenate(wtss, axis=1).T
    h_rows2 = jnp.concatenate(hrs, axis=0)
    row_tok, row_asg, block_e, nb_used, block_cnt = moe_dispatch(idxs, cnts[0] + cnts[1])
    y_rows = moe_experts(h_rows2.reshape(n_all, nc, LANE), row_tok, row_asg, block_e, nb_used,
                         block_cnt, p['w_gate_e'], p['w_up_e'], p['w_down_e'], l)
    y_rows2 = y_rows.reshape(-1, LANE)
    wgs, wus, wds = (p['w_gate_s'][l].astype(BF16), p['w_up_s'][l].astype(BF16),
                     p['w_down_s'][l].astype(BF16))
    xp2 = moe_finish(x1s['p'], st_p, h_rows2, wts_all, y_rows2, wgs, wus, wds, 0, 0)
    xs2 = moe_finish(x1s['s'], st_s, h_rows2, wts_all, y_rows2, wgs, wus, wds, mp, TOP_K * mp)
    return xp2, xs2, new_k, new_v, new_s


def kernel(x_prompt, x_sample, cache_k_w128, cache_v_w128, cache_k_w512, cache_v_w512, cache_k_w2048, cache_v_w2048, state_gla, c_prompt, c_sample, w_in, q_norm, k_norm, alpha_up, alpha_bias, gla_norm, w_branch_a, w_branch_b, w_out, norm_mix, norm_ffn, w_ada, b_ada, w_router, router_bias, w_gate_e, w_up_e, w_down_e, w_gate_s, w_up_s, w_down_s):
    batch, seq, d = x_prompt.shape
    db, t_new, _ = x_sample.shape
    depth = w_in.shape[0]
    dims = (batch, seq, db, t_new)
    n_main = w_in.shape[-1] - GLA_RANK - 2 * d
    p = dict(w_in=w_in, w_lr=w_in[:, :, n_main:n_main + GLA_RANK],
             w_gates=gate_weights(w_in, n_main + GLA_RANK, 2 * d),
             q_norm=q_norm, k_norm=k_norm, alpha_up=alpha_up, alpha_bias=alpha_bias,
             gla_norm=gla_norm, w_branch_a=w_branch_a, w_branch_b=w_branch_b, w_out=w_out,
             norm_mix=norm_mix, norm_ffn=norm_ffn, w_ada=w_ada, b_ada=b_ada, w_router=w_router,
             router_bias=router_bias, w_gate_s=w_gate_s, w_up_s=w_up_s, w_down_s=w_down_s,
             w_gate_e=w_gate_e, w_up_e=w_up_e, w_down_e=w_down_e)
    caches_k = (cache_k_w128, cache_k_w512, cache_k_w2048)
    caches_v = (cache_v_w128, cache_v_w512, cache_v_w2048)
    c_all = jnp.concatenate([c_prompt, c_sample], axis=0)
    xp = x_prompt.reshape(batch * seq, d)
    xs = x_sample.reshape(db * t_new, d)
    ks, vs, ss = {'p': [], 's': []}, {'p': [], 's': []}, {'p': [], 's': []}
    for l in range(depth):
        xp, xs, nk, nv, ns = decoder_layer(l, xp, xs, c_all, p, caches_k, caches_v, state_gla, dims)
        for name in ('p', 's'):
            ks[name].append(nk[name])
            vs[name].append(nv[name])
            ss[name].append(ns[name])
    out = [xp.reshape(batch, seq, d), xs.reshape(db, t_new, d)]
    for name in ('p', 's'):
        for g in range(N_GROUPS):
            out.append(jnp.stack([k_[g] for k_ in ks[name]]))
            out.append(jnp.stack([v_[g] for v_ in vs[name]]))
        out.append(jnp.stack(ss[name]))
    return tuple(out)
```

```python
import functools

import jax
import jax.numpy as jnp
from jax import lax
from jax.experimental import pallas as pl
from jax.experimental.pallas import tpu as pltpu

F32 = jnp.float32
BF16 = jnp.bfloat16
I32 = jnp.int32
U32 = jnp.uint32

DIL_GROUPS = ((128, 1), (512, 4), (2048, 16))
N_GROUPS = 3
A_HEADS = 8
A_HEAD_DIM = 128
A_WIDTH = A_HEADS * A_HEAD_DIM
SPAN = 128
ROPE_DIM = A_HEAD_DIM // 4
ROPE_THETA = 500000.0
GLA_HEADS = 4
GLA_RANK = 16
GLA_TAU = 16.0
GLA_CHUNK = 64
N_EXPERTS = 64
TOP_K = 6
N_EXPERT_GROUPS = 8
TOPK_GROUPS = 4
ROUTED_SCALE = 2.5
EPS = 1e-6
NEG_INF = -1e30
PAST_LEN = 8192

LANE = 128
TOPK_PAD = 8
MOE_ROWS = 256
DMA_UNROLL = 32
VMEM_LIMIT = 56 * 1024 * 1024
MOE_VMEM_LIMIT = 60 * 1024 * 1024


def _cp(*sem):
    return pltpu.CompilerParams(dimension_semantics=sem, vmem_limit_bytes=VMEM_LIMIT)


def _silu(x):
    return x * jax.nn.sigmoid(x)


def _tile(n, pref):
    if n <= pref:
        return n
    t = pref
    while n % t:
        t //= 2
    return t


def _ada_kernel(c_ref, w_ref, b_ref, o_ref):
    s = _silu(c_ref[...]).astype(BF16)
    o_ref[...] = jnp.dot(s, w_ref[...].astype(BF16), preferred_element_type=F32) + b_ref[...]


def ada_mod(c, w, b, layer):
    g, d = c.shape
    n = w.shape[-1]
    tn = _tile(n, 512)
    return pl.pallas_call(
        _ada_kernel,
        out_shape=jax.ShapeDtypeStruct((g, n), F32),
        grid=(n // tn,),
        in_specs=[pl.BlockSpec((g, d), lambda j: (0, 0)),
                  pl.BlockSpec((None, d, tn), lambda j: (layer, 0, j)),
                  pl.BlockSpec((None, 1, tn), lambda j: (layer, 0, j))],
        out_specs=pl.BlockSpec((g, tn), lambda j: (0, j)),
        compiler_params=_cp("arbitrary"),
        name="ada_mod",
    )(c, w, b.reshape(b.shape[0], 1, n))


class Stream:
    def __init__(self, rows, tm, mod, rows_per_group, pos):
        self.rows, self.tm, self.mod, self.rpg, self.pos = rows, tm, mod, rows_per_group, pos
        self.d = mod.shape[-1] // 6
        self.r = mod.shape[1]

    def mod_spec(self, comp, tm, tn=None, col_of=None):
        tn = self.d if tn is None else tn
        nj = self.d // tn
        rpg, r = self.rpg, self.r

        def idx(*g):
            i, j = col_of(*g)
            grp = (i * tm) // rpg if r == 1 else 0
            return (grp, 0, comp * nj + j)
        return pl.BlockSpec((None, r, tn), idx)


def _norm_mod(x, g, sc, sh):
    y = x * lax.rsqrt(jnp.mean(x * x, axis=-1, keepdims=True) + EPS) * g
    return y * (1.0 + sc) + sh


def _norm_mod_kernel(x_ref, g_ref, sc_ref, sh_ref, o_ref):
    o_ref[...] = _norm_mod(x_ref[...], g_ref[...], sc_ref[...], sh_ref[...]).astype(o_ref.dtype)


def norm_mod(x, gain, st, comp_shift, comp_scale):
    m, d = x.shape
    tm = _tile(m, 512) if st.r == 1 else m
    col = lambda i: (i, 0)
    return pl.pallas_call(
        _norm_mod_kernel,
        out_shape=jax.ShapeDtypeStruct((m, d), BF16),
        grid=(m // tm,),
        in_specs=[pl.BlockSpec((tm, d), lambda i: (i, 0)),
                  pl.BlockSpec((1, d), lambda i: (0, 0)),
                  st.mod_spec(comp_scale, tm, col_of=col),
                  st.mod_spec(comp_shift, tm, col_of=col)],
        out_specs=pl.BlockSpec((tm, d), lambda i: (i, 0)),
        compiler_params=_cp("parallel"),
        name="norm_mod",
    )(x, gain.reshape(1, d), st.mod, st.mod)


def _mm_kernel(*refs, n_extra, epilogue, cast_w):
    x_ref, w_ref = refs[0], refs[1]
    extras = refs[2:2 + n_extra]
    o_ref = refs[2 + n_extra]
    if cast_w:
        wbf = refs[3 + n_extra]

        @pl.when(pl.program_id(1) == 0)
        def _():
            wbf[...] = w_ref[...].astype(BF16)
        w = wbf[...]
    else:
        w = w_ref[...]
    acc = jnp.dot(x_ref[...], w, preferred_element_type=F32)
    if epilogue is not None:
        acc = epilogue(acc, *[e[...] for e in extras])
    o_ref[...] = acc.astype(o_ref.dtype)


def matmul(x, w, n_cols, *, tm, tn, out_dtype, extra=(), extra_specs=(), epilogue=None, name="mm",
           layer=None):
    m, k = x.shape
    cast_w = w.dtype != BF16
    if layer is None:
        w_spec = pl.BlockSpec((k, tn), lambda j, i: (0, j))
    else:
        w_spec = pl.BlockSpec((None, k, tn), lambda j, i: (layer, 0, j))
    scratch = [pltpu.VMEM((k, tn), BF16)] if cast_w else []
    return pl.pallas_call(
        functools.partial(_mm_kernel, n_extra=len(extra), epilogue=epilogue, cast_w=cast_w),
        out_shape=jax.ShapeDtypeStruct((m, n_cols), out_dtype),
        grid=(n_cols // tn, m // tm),
        in_specs=[pl.BlockSpec((tm, k), lambda j, i: (i, 0)), w_spec] + list(extra_specs),
        out_specs=pl.BlockSpec((tm, tn), lambda j, i: (i, j)),
        scratch_shapes=scratch,
        compiler_params=_cp("parallel", "arbitrary"),
        name=name,
    )(x, w, *extra)


def _qk_prep_kernel(*refs, dils, tm, n_tiles, first_tiles, keep_rows):
    (q_ref, k_ref, v_ref, qn_ref, kn_ref, c_ref, sn_ref, sum_ref, exp_ref, swap_ref) = refs[:10]
    qkv_out = refs[10:10 + 3 * N_GROUPS]
    kc_out = refs[10 + 3 * N_GROUPS:10 + 4 * N_GROUPS]
    vc_out = refs[10 + 4 * N_GROUPS:10 + 5 * N_GROUPS]
    scr = refs[10 + 5 * N_GROUPS]
    cos, sin = c_ref[...], sn_ref[...]
    swap = swap_ref[...]
    ti = pl.program_id(0) % n_tiles

    def normed(x_ref, gain_ref):
        x = x_ref[...].astype(F32)
        ss = jnp.dot((x * x).astype(BF16), sum_ref[...], preferred_element_type=F32)
        rs = lax.rsqrt(ss * (1.0 / A_HEAD_DIM) + EPS)
        hi = rs.astype(BF16)
        lo = (rs - hi.astype(F32)).astype(BF16)
        rs_full = (jnp.dot(hi, exp_ref[...], preferred_element_type=F32)
                   + jnp.dot(lo, exp_ref[...], preferred_element_type=F32))
        return x * rs_full * gain_ref[...]

    def rotary(yh):
        partner = jnp.dot(yh.astype(BF16), swap, preferred_element_type=F32)
        return yh * cos + partner * sin

    qy = normed(q_ref, qn_ref)
    ky = normed(k_ref, kn_ref)
    n_slabs = scr.shape[0]
    slab_i = 0
    for g in range(N_GROUPS):
        d = dils[g]
        kb = keep_rows[g]
        k_heads, v_heads = [], []
        for hl in range(A_HEADS):
            h = g * A_HEADS + hl
            sl = slice(h * A_HEAD_DIM, (h + 1) * A_HEAD_DIM)
            ol = slice(hl * A_HEAD_DIM, (hl + 1) * A_HEAD_DIM)
            qh = rotary(qy[:, sl])
            kh = rotary(ky[:, sl])
            vh = v_ref[:, sl].astype(F32)
            k_heads.append(kh)
            v_heads.append(vh)
            for which, val in enumerate((qh, kh, vh)):
                out = qkv_out[3 * g + which]
                if d == 1:
                    out[0, :, ol] = val.astype(BF16)
                else:
                    slab = slab_i % n_slabs
                    slab_i += 1
                    scr[slab] = val
                    for r in range(d):
                        out[r, :, ol] = scr[slab, pl.ds(r, tm // d, stride=d), :].astype(BF16)

        def write_cache(g=g, kb=kb, k_heads=k_heads, v_heads=v_heads):
            for hl in range(A_HEADS):
                rows_of_head = pl.ds(hl, kb, stride=A_HEADS)
                kc_out[g][rows_of_head, :] = k_heads[hl][tm - kb:, :]
                vc_out[g][rows_of_head, :] = v_heads[hl][tm - kb:, :]

        if first_tiles[g] == 0:
            write_cache()
        else:
            pl.when(ti >= first_tiles[g])(write_cache)


def rope_tables(pos):
    half = ROPE_DIM // 2
    inv = ROPE_THETA ** (-jnp.arange(half, dtype=F32) * 2.0 / ROPE_DIM)
    ang = pos.astype(F32)[:, None] * inv[None, :]
    cos, sin = jnp.cos(ang), jnp.sin(ang)
    t = pos.shape[0]
    rest = A_HEAD_DIM - ROPE_DIM
    c = jnp.concatenate([cos, cos, jnp.ones((t, rest), F32)], axis=1)
    s = jnp.concatenate([-sin, sin, jnp.zeros((t, rest), F32)], axis=1)
    return c, s


def _head_matrices(n_heads):
    col = jnp.arange(n_heads * A_HEAD_DIM) // A_HEAD_DIM
    lane = jnp.arange(LANE)
    summing = (col[:, None] == lane[None, :]).astype(BF16)
    half = ROPE_DIM // 2
    src = jnp.arange(A_HEAD_DIM)
    partner = jnp.where(src < half, src + half, jnp.where(src < ROPE_DIM, src - half, -1))
    swap = (src[:, None] == partner[None, :]).astype(BF16)
    return summing, summing.T, swap


def qk_prep(proj, q_norm, k_norm, tables, dils, batch, seq, keeps):
    c, s = tables
    tm = _tile(seq, 256)
    nt = seq // tm
    qa_w = N_GROUPS * A_WIDTH
    n_heads = N_GROUPS * A_HEADS
    summing, expanding, swap = _head_matrices(n_heads)
    first_tiles, keep_rows, n_keep_blocks = [], [], []
    for keep in keeps:
        kb = min(tm, keep)
        assert keep % kb == 0 and (seq - keep) % kb == 0 and tm % kb == 0
        first_tiles.append((seq - keep) // tm)
        keep_rows.append(kb)
        n_keep_blocks.append(keep // kb)
    tab = pl.BlockSpec((tm, A_HEAD_DIM), lambda i: (i % nt, 0))
    whole = lambda a: pl.BlockSpec(a.shape, lambda i: (0, 0))
    blk = lambda j: pl.BlockSpec((tm, qa_w), lambda i: (i, j))
    gains = [jnp.tile(w.reshape(1, -1), (1, n_heads)) for w in (q_norm, k_norm)]
    out_shapes, out_specs = [], []
    for g in range(N_GROUPS):
        d = dils[g]
        for _ in range(3):
            out_shapes.append(jax.ShapeDtypeStruct((batch, d, seq // d, A_WIDTH), BF16))
            out_specs.append(pl.BlockSpec((None, d, tm // d, A_WIDTH),
                                          lambda i: (i // nt, 0, i % nt, 0)))
    for _ in range(2):
        for g in range(N_GROUPS):
            kb, nkb, first = keep_rows[g], n_keep_blocks[g], first_tiles[g]
            out_shapes.append(jax.ShapeDtypeStruct((batch * keeps[g] * A_HEADS, A_HEAD_DIM), F32))
            out_specs.append(pl.BlockSpec(
                (kb * A_HEADS, A_HEAD_DIM),
                lambda i, nkb=nkb, first=first:
                ((i // nt) * nkb + jnp.clip(i % nt - first, 0, nkb - 1), 0)))
    res = pl.pallas_call(
        functools.partial(_qk_prep_kernel, dils=tuple(dils), tm=tm, n_tiles=nt,
                          first_tiles=tuple(first_tiles), keep_rows=tuple(keep_rows)),
        out_shape=tuple(out_shapes),
        grid=(batch * nt,),
        in_specs=[blk(0), blk(1), blk(2), whole(gains[0]), whole(gains[1]), tab, tab,
                  whole(summing), whole(expanding), whole(swap)],
        out_specs=tuple(out_specs),
        scratch_shapes=[pltpu.VMEM((24, tm, A_HEAD_DIM), F32)],
        compiler_params=_cp("arbitrary"),
        name="qk_prep",
    )(proj, proj, proj, gains[0], gains[1], c, s, summing, expanding, swap)
    n3 = 3 * N_GROUPS
    return res[:n3], res[n3:n3 + N_GROUPS], res[n3 + N_GROUPS:]


def _attn_prompt_kernel(q_ref, kp_ref, kc_ref, vp_ref, vc_ref, o_ref, lse_ref, *, n_res, n_qb):
    qi = lax.broadcasted_iota(I32, (SPAN, 2 * SPAN), 0)
    ki = lax.broadcasted_iota(I32, (SPAN, 2 * SPAN), 1)
    band = (ki >= qi) & (ki <= qi + SPAN)
    band_first = band & ((pl.program_id(2) > 0) | (ki >= SPAN))
    lane = lax.broadcasted_iota(I32, (SPAN, LANE), 1)
    scale = A_HEAD_DIM ** -0.5
    for rr in range(n_res):
        for jb in range(n_qb):
            rows = slice(jb * SPAN, (jb + 1) * SPAN)
            prev = slice((jb - 1) * SPAN, jb * SPAN)
            mask = band_first if jb == 0 else band
            lse_all = jnp.zeros((SPAN, LANE), F32)
            for h in range(A_HEADS):
                sl = slice(h * A_HEAD_DIM, (h + 1) * A_HEAD_DIM)
                k_prev = kp_ref[rr, :, sl] if jb == 0 else kc_ref[rr, prev, sl]
                v_prev = vp_ref[rr, :, sl] if jb == 0 else vc_ref[rr, prev, sl]
                kh = jnp.concatenate([k_prev, kc_ref[rr, rows, sl]], axis=0)
                vh = jnp.concatenate([v_prev, vc_ref[rr, rows, sl]], axis=0)
                s = lax.dot_general(q_ref[rr, rows, sl], kh, (((1,), (1,)), ((), ())),
                                    preferred_element_type=F32) * scale
                s = jnp.where(mask, s, NEG_INF)
                m = jnp.max(s, axis=-1, keepdims=True)
                p = jnp.exp(s - m)
                l = jnp.sum(p, axis=-1, keepdims=True)
                o = jnp.dot(p.astype(BF16), vh, preferred_element_type=F32) / l
                o_ref[rr, rows, sl] = o.astype(o_ref.dtype)
                lse_all = jnp.where(lane == h, m + jnp.log(l), lse_all)
            lse_ref[rr, rows, :] = lse_all


def attn_prompt(q_g, k_g, v_g, g):
    batch, dil, l, _ = q_g.shape
    nblk = l // SPAN
    n_qb = min(4, nblk)
    n_res = min(max(4 // n_qb, 1), dil)
    assert nblk % n_qb == 0 and dil % n_res == 0
    cur = lambda w: pl.BlockSpec((None, n_res, n_qb * SPAN, w), lambda b, r, i: (b, r, i, 0))
    prev = pl.BlockSpec((None, n_res, SPAN, A_WIDTH),
                        lambda b, r, i: (b, r, jnp.maximum(i * n_qb - 1, 0), 0))
    return pl.pallas_call(
        functools.partial(_attn_prompt_kernel, n_res=n_res, n_qb=n_qb),
        out_shape=(jax.ShapeDtypeStruct((batch, dil, l, A_WIDTH), BF16),
                   jax.ShapeDtypeStruct((batch, dil, l, LANE), F32)),
        grid=(batch, dil // n_res, nblk // n_qb),
        in_specs=[cur(A_WIDTH), prev, cur(A_WIDTH), prev, cur(A_WIDTH)],
        out_specs=(cur(A_WIDTH), cur(LANE)),
        compiler_params=_cp("parallel", "parallel", "arbitrary"),
        name=f"attn_prompt_g{g}",
    )(q_g, k_g, k_g, v_g, v_g)


def _attn_sample_kernel(q_ref, k_ref, v_ref, ck0, cv0, ck1, cv1, ck2, cv2, o_ref, l_ref, *, t_new):
    caches = ((ck0, cv0), (ck1, cv1), (ck2, cv2))
    scale = A_HEAD_DIM ** -0.5
    jidx = lax.broadcasted_iota(I32, (SPAN, A_HEADS, 1), 0)
    for g, (_, dil) in enumerate(DIL_GROUPS):
        ck, cv = caches[g]
        for t in range(t_new):
            res = 0 if dil == 1 else t
            q = q_ref[t, g]
            s_c = jnp.sum(ck[:, res] * q[None], axis=-1, keepdims=True) * scale
            if dil == 1:
                s_c = jnp.where(jidx >= t, s_c, NEG_INF)
            new = range(t + 1) if dil == 1 else (t,)
            s_n = [jnp.sum(q * k_ref[u, g], axis=-1, keepdims=True) * scale for u in new]
            m = jnp.max(s_c, axis=0)
            for s in s_n:
                m = jnp.maximum(m, s)
            p_c = jnp.exp(s_c - m[None])
            lsum = jnp.sum(p_c, axis=0)
            o = jnp.sum(p_c * cv[:, res], axis=0)
            for u, s in zip(new, s_n):
                p_u = jnp.exp(s - m)
                lsum = lsum + p_u
                o = o + p_u * v_ref[u, g]
            o_ref[t, g] = o / lsum
            l_ref[t, g] = jnp.broadcast_to(m + jnp.log(lsum), (A_HEADS, A_HEAD_DIM))


def attn_sample(q5, k5, v5, caches_k, caches_v, layer):
    db, t_new = q5.shape[:2]
    new_spec = pl.BlockSpec((None, t_new, N_GROUPS, A_HEADS, A_HEAD_DIM), lambda b: (b, 0, 0, 0, 0))
    ins, specs = [q5, k5, v5], [new_spec] * 3
    for g, (win, dil) in enumerate(DIL_GROUPS):
        for c in (caches_k[g], caches_v[g]):
            depth, _, wc, hh, ee = c.shape
            assert wc == win and (hh, ee) == (A_HEADS, A_HEAD_DIM) and (dil == 1 or t_new <= dil)
            nres = 1 if dil == 1 else t_new
            ins.append(c.reshape(depth, db, wc // dil, dil, hh, ee))
            specs.append(pl.BlockSpec((None, None, SPAN, nres, hh, ee),
                                      lambda b: (layer, b, 0, 0, 0, 0)))
    return pl.pallas_call(
        functools.partial(_attn_sample_kernel, t_new=t_new),
        out_shape=(jax.ShapeDtypeStruct(q5.shape, F32), jax.ShapeDtypeStruct(q5.shape, F32)),
        grid=(db,),
        in_specs=specs,
        out_specs=(new_spec, new_spec),
        compiler_params=_cp("parallel"),
        name="attn_sample",
    )(*ins)


def _combine_kernel(*refs, dils, tm):
    o_refs, l_refs, out_ref, scr_o, scr_l = refs[:3], refs[3:6], refs[6], refs[7], refs[8]
    ls = []
    for g, d in enumerate(dils):
        if d == 1:
            ls.append(l_refs[g][0])
        else:
            for r in range(d):
                scr_l[g, pl.ds(r, tm // d, stride=d), :] = l_refs[g][r]
            ls.append(scr_l[g])
    m = jnp.maximum(jnp.maximum(ls[0], ls[1]), ls[2])
    es = [jnp.exp(x - m) for x in ls]
    den = es[0] + es[1] + es[2]
    rr = lax.broadcasted_iota(I32, (LANE, A_WIDTH), 0)
    cc = lax.broadcasted_iota(I32, (LANE, A_WIDTH), 1)
    expand = (cc // A_HEAD_DIM == rr).astype(BF16)
    wxs = []
    for e in es:
        w = e / den
        hi = w.astype(BF16)
        lo = (w - hi.astype(F32)).astype(BF16)
        wxs.append(jnp.dot(hi, expand, preferred_element_type=F32)
                   + jnp.dot(lo, expand, preferred_element_type=F32))
    for h in range(A_HEADS):
        hs = slice(h * A_HEAD_DIM, (h + 1) * A_HEAD_DIM)
        acc = None
        for g, d in enumerate(dils):
            if d == 1:
                o_tok = o_refs[g][0, :, hs].astype(F32)
            else:
                for r in range(d):
                    scr_o[g, h, pl.ds(r, tm // d, stride=d), :] = o_refs[g][r, :, hs].astype(F32)
                o_tok = scr_o[g, h]
            term = wxs[g][:, hs] * o_tok
            acc = term if acc is None else acc + term
        out_ref[:, hs] = acc.astype(out_ref.dtype)


def combine_groups(os_, ls_):
    batch = os_[0].shape[0]
    dils = tuple(o.shape[1] for o in os_)
    seq = os_[0].shape[1] * os_[0].shape[2]
    tm = _tile(seq, 256)
    nt = seq // tm
    specs = []
    for width in (A_WIDTH, LANE):
        for d in dils:
            specs.append(pl.BlockSpec((None, d, tm // d, width), lambda i: (i // nt, 0, i % nt, 0)))
    return pl.pallas_call(
        functools.partial(_combine_kernel, dils=dils, tm=tm),
        out_shape=jax.ShapeDtypeStruct((batch * seq, A_WIDTH), BF16),
        grid=(batch * nt,),
        in_specs=specs,
        out_specs=pl.BlockSpec((tm, A_WIDTH), lambda i: (i, 0)),
        scratch_shapes=[pltpu.VMEM((N_GROUPS, A_HEADS, tm, A_HEAD_DIM), F32),
                        pltpu.VMEM((N_GROUPS, tm, LANE), F32)],
        compiler_params=_cp("parallel"),
        name="combine_groups",
    )(*os_, *ls_)


def _alpha_kernel(h_ref, w1_ref, w2_ref, b_ref, o_ref):
    lr = jnp.dot(h_ref[...], w1_ref[...], preferred_element_type=F32)
    z = jnp.dot(lr.astype(BF16), w2_ref[...], preferred_element_type=F32) + b_ref[...]
    log_sig = jnp.minimum(z, 0.0) - jnp.log(1.0 + jnp.exp(-jnp.abs(z)))
    o_ref[...] = log_sig / GLA_TAU


def gla_log_decay(h, w1, alpha_up, alpha_bias):
    m, d = h.shape
    gq = alpha_up.shape[1]
    w2 = jnp.pad(alpha_up, ((0, LANE - GLA_RANK), (0, 0))).astype(BF16)
    tm = _tile(m, 512)
    return pl.pallas_call(
        _alpha_kernel,
        out_shape=jax.ShapeDtypeStruct((m, gq), F32),
        grid=(m // tm,),
        in_specs=[pl.BlockSpec((tm, d), lambda i: (i, 0)),
                  pl.BlockSpec((d, LANE), lambda i: (0, 0)),
                  pl.BlockSpec((LANE, gq), lambda i: (0, 0)),
                  pl.BlockSpec((1, gq), lambda i: (0, 0))],
        out_specs=pl.BlockSpec((tm, gq), lambda i: (i, 0)),
        compiler_params=_cp("parallel"),
        name="gla_log_decay",
    )(h, w1, w2, alpha_bias.reshape(1, gq))


def _gla_kernel(*refs, chunk, dk, dv, has_s0):
    if has_s0:
        q_ref, k_ref, v0_ref, v1_ref, la_ref, r0_ref, r1_ref, gn_ref, s0_ref, ob_ref, s_ref = refs
    else:
        q_ref, k_ref, v0_ref, v1_ref, la_ref, r0_ref, r1_ref, gn_ref, ob_ref, s_ref = refs

    @pl.when(pl.program_id(1) == 0)
    def _():
        s_ref[...] = s0_ref[...].astype(F32) if has_s0 else jnp.zeros(s_ref.shape, F32)

    ri = lax.broadcasted_iota(I32, (chunk, chunk), 0)
    ci = lax.broadcasted_iota(I32, (chunk, chunk), 1)
    causal = ri >= ci
    tri = causal.astype(F32)
    gn = gn_ref[...]
    per_block = GLA_HEADS // 2
    for h in range(GLA_HEADS):
        ks = slice(h * dk, (h + 1) * dk)
        vs = slice((h % per_block) * dv, (h % per_block + 1) * dv)
        v_ref, r_ref = (v0_ref, r0_ref) if h < per_block else (v1_ref, r1_ref)
        la = la_ref[:, ks]
        b = jnp.dot(tri, la, precision=lax.Precision.HIGHEST,
                    preferred_element_type=F32)
        b_last = b[chunk - 1:chunk, :]
        q = q_ref[:, ks].astype(F32) * (dk ** -0.5)
        k = k_ref[:, ks].astype(F32)
        v = v_ref[:, vs]
        q_dec = (q * jnp.exp(b)).astype(BF16)
        k_inv = (k * jnp.exp(-b)).astype(BF16)
        k_dec = (k * jnp.exp(b_last - b)).astype(BF16)
        att = lax.dot_general(q_dec, k_inv, (((1,), (1,)), ((), ())), preferred_element_type=F32)
        att = jnp.where(causal, att, 0.0).astype(BF16)
        s_old = s_ref[h]
        o = (jnp.dot(att, v, preferred_element_type=F32)
             + jnp.dot(q_dec, s_old.astype(BF16), preferred_element_type=F32))
        kv = lax.dot_general(k_dec, v, (((0,), (0,)), ((), ())), preferred_element_type=F32)
        decay = jnp.sum(jnp.transpose(la), axis=1, keepdims=True)
        s_ref[h] = jnp.exp(decay) * s_old + kv
        y = o * lax.rsqrt(jnp.mean(o * o, axis=-1, keepdims=True) + EPS) * gn
        ob_ref[:, h * dv:(h + 1) * dv] = (y * _silu(r_ref[:, vs].astype(F32))).astype(ob_ref.dtype)


def gla(proj3, la3, gla_norm, s0, layer, dk, dv):
    bsz, t, w = proj3.shape
    chunk = GLA_CHUNK
    qa_w = N_GROUPS * A_WIDTH
    gq, gv = GLA_HEADS * dk, GLA_HEADS * dv
    assert gv == 2 * gq and (3 * qa_w) % gq == 0
    base = 3 * qa_w // gq
    cspec = lambda off: pl.BlockSpec((None, chunk, gq), lambda b, c: (b, c, off))
    ins = [proj3, proj3, proj3, proj3, la3, proj3, proj3, gla_norm.reshape(1, dv)]
    specs = [cspec(base), cspec(base + 1), cspec(base + 2), cspec(base + 3), cspec(0),
             cspec(base + 4), cspec(base + 5), pl.BlockSpec((1, dv), lambda b, c: (0, 0))]
    if s0 is not None:
        ins.append(s0)
        specs.append(pl.BlockSpec((None, None, GLA_HEADS, dk, dv), lambda b, c: (layer, b, 0, 0, 0)))
    return pl.pallas_call(
        functools.partial(_gla_kernel, chunk=chunk, dk=dk, dv=dv, has_s0=s0 is not None),
        out_shape=(jax.ShapeDtypeStruct((bsz, t, gv), BF16),
                   jax.ShapeDtypeStruct((bsz, GLA_HEADS, dk, dv), F32)),
        grid=(bsz, t // chunk),
        in_specs=specs,
        out_specs=(pl.BlockSpec((None, chunk, gv), lambda b, c: (b, c, 0)),
                   pl.BlockSpec((None, GLA_HEADS, dk, dv), lambda b, c: (b, 0, 0, 0))),
        compiler_params=_cp("parallel", "arbitrary"),
        name="gla",
    )(*ins)


def _gate_weights_kernel(w_ref, nxt_ref, o_ref, *, shift, n_keep):
    tn = o_ref.shape[1]
    both = jnp.concatenate([w_ref[...], nxt_ref[...]], axis=1)
    cut = both[:, shift:shift + tn]
    if n_keep < tn:
        lane = lax.broadcasted_iota(I32, cut.shape, 1)
        cut = jnp.where(lane < n_keep, cut, 0.0)
    o_ref[...] = cut.astype(o_ref.dtype)


def gate_weights(w_in, col0, n_cols, n_keep=None):
    depth, k, n = w_in.shape
    tn = _tile(n_cols, 512)
    base = col0 // LANE * LANE
    shift = col0 - base
    n_keep = tn if n_keep is None else n_keep
    assert base % tn == 0 and n_cols % tn == 0 and col0 + min(n_cols, n_keep) <= n
    last_blk = (n - 1) // LANE
    return pl.pallas_call(
        functools.partial(_gate_weights_kernel, shift=shift, n_keep=n_keep),
        out_shape=jax.ShapeDtypeStruct((depth, k, n_cols), BF16),
        grid=(depth, n_cols // tn),
        in_specs=[pl.BlockSpec((None, k, tn), lambda l, j: (l, 0, base // tn + j)),
                  pl.BlockSpec((None, k, LANE),
                               lambda l, j: (l, 0, jnp.minimum((base + (j + 1) * tn) // LANE, last_blk)))],
        out_specs=pl.BlockSpec((None, k, tn), lambda l, j: (l, 0, j)),
        compiler_params=_cp("parallel", "parallel"),
        name="gate_weights",
    )(w_in, w_in)


def _merge_kernel(h_ref, oa_ref, ob_ref, wga_ref, wgb_ref, wa_ref, wb_ref, o_ref):
    h = h_ref[...]
    ga = jnp.dot(h, wga_ref[...], preferred_element_type=F32)
    gb = jnp.dot(h, wgb_ref[...], preferred_element_type=F32)
    a = jnp.dot(oa_ref[...], wa_ref[...], preferred_element_type=F32)
    b = jnp.dot(ob_ref[...], wb_ref[...], preferred_element_type=F32)
    o_ref[...] = (jax.nn.sigmoid(ga) * a + jax.nn.sigmoid(gb) * b).astype(o_ref.dtype)


def merge_branches(h, oa, ob, w_gates, w_a, w_b, tm, layer):
    m, d = h.shape
    tn = _tile(d, 512)
    nj = d // tn
    row = lambda width: pl.BlockSpec((tm, width), lambda j, i: (i, 0))
    wcol = lambda rows, off: pl.BlockSpec((rows, tn), lambda j, i: (0, j + off))
    gcol = lambda off: pl.BlockSpec((None, d, tn), lambda j, i: (layer, 0, j + off))
    return pl.pallas_call(
        _merge_kernel,
        out_shape=jax.ShapeDtypeStruct((m, d), BF16),
        grid=(nj, m // tm),
        in_specs=[row(d), row(oa.shape[1]), row(ob.shape[1]),
                  gcol(0), gcol(nj), wcol(w_a.shape[0], 0), wcol(w_b.shape[0], 0)],
        out_specs=pl.BlockSpec((tm, tn), lambda j, i: (i, j)),
        compiler_params=_cp("parallel", "arbitrary"),
        name="merge_branches",
    )(h, oa, ob, w_gates, w_gates, w_a, w_b)


def _pack_rows(x):
    half = x.shape[1] // 2
    lo = lax.bitcast_convert_type(x[:, :half].astype(BF16).astype(F32), U32)
    hi = lax.bitcast_convert_type(x[:, half:].astype(BF16).astype(F32), U32)
    return (lo >> 16) | (hi & jnp.uint32(0xFFFF0000))


def _unpack_chunk(u):
    return (lax.bitcast_convert_type(u << 16, F32),
            lax.bitcast_convert_type(u & jnp.uint32(0xFFFF0000), F32))


def _store_rows(ref, packed, rows):
    n_chunks = packed.shape[1] // LANE
    for c in range(n_chunks):
        ref[pl.ds(c, rows, stride=n_chunks), :] = packed[:, c * LANE:(c + 1) * LANE]


def _load_rows(ref, rows, n_chunks, dtype, start=0):
    lo, hi = [], []
    for c in range(n_chunks):
        a, b = _unpack_chunk(ref[pl.ds(start + c, rows, stride=n_chunks), :])
        lo.append(a.astype(dtype))
        hi.append(b.astype(dtype))
    return jnp.concatenate(lo + hi, axis=1)


def _norm_router_kernel(x_ref, g_ref, sc_ref, sh_ref, wr_ref, rb_ref, hr_ref, idx_ref, wt_ref,
                        cnt_ref):
    h = _norm_mod(x_ref[...], g_ref[...], sc_ref[...], sh_ref[...])
    tm = h.shape[0]
    _store_rows(hr_ref, _pack_rows(h), tm)
    per = N_EXPERTS // N_EXPERT_GROUPS
    logits = lax.dot_general(wr_ref[...], h, (((1,), (1,)), ((), ())),
                             precision=lax.Precision.HIGHEST, preferred_element_type=F32)
    scores = jax.nn.sigmoid(logits)
    biased = scores + rb_ref[...]
    bg = biased.reshape(N_EXPERT_GROUPS, per, tm)
    ip = lax.broadcasted_iota(I32, bg.shape, 1)
    m1 = jnp.max(bg, axis=1, keepdims=True)
    i1 = jnp.min(jnp.where(bg == m1, ip, per), axis=1, keepdims=True)
    m2 = jnp.max(jnp.where(ip == i1, -jnp.inf, bg), axis=1, keepdims=True)
    grp = (m1 + m2)[:, 0, :]
    gi = lax.broadcasted_iota(I32, grp.shape, 0)
    rank = jnp.zeros(grp.shape, I32)
    for g in range(N_EXPERT_GROUPS):
        other = grp[g:g + 1, :]
        ahead = (other > grp) | ((other == grp) & (g < gi))
        rank = rank + ahead.astype(I32)
    sel = (rank < TOPK_GROUPS)[:, None, :]
    masked = jnp.where(sel, bg, -jnp.inf).reshape(N_EXPERTS, tm)
    ie = lax.broadcasted_iota(I32, masked.shape, 0)
    ids, ws = [], []
    chosen = jnp.zeros(masked.shape, F32)
    for _ in range(TOP_K):
        mx = jnp.max(masked, axis=0, keepdims=True)
        ik = jnp.min(jnp.where(masked == mx, ie, N_EXPERTS), axis=0, keepdims=True)
        hit = ie == ik
        ids.append(ik)
        ws.append(jnp.sum(jnp.where(hit, scores, 0.0), axis=0, keepdims=True))
        masked = jnp.where(hit, -jnp.inf, masked)
        chosen = chosen + hit.astype(F32)
    cnt_ref[...] = jnp.sum(chosen, axis=1, keepdims=True).astype(I32)
    tot = ws[0]
    for wk in ws[1:]:
        tot = tot + wk
    pad = TOPK_PAD - TOP_K
    idx_ref[...] = jnp.concatenate(ids + [jnp.zeros((pad, tm), I32)], axis=0)
    wt_ref[...] = jnp.concatenate([wk / tot * ROUTED_SCALE for wk in ws]
                                  + [jnp.zeros((pad, tm), F32)], axis=0)


def norm_router(x, gain, st, comp_shift, comp_scale, w_router_t, router_bias):
    m, d = x.shape
    tm = _tile(m, 256) if st.r == 1 else m
    nc = d // 2 // LANE
    col = lambda i: (i, 0)
    e = w_router_t.shape[0]
    return pl.pallas_call(
        _norm_router_kernel,
        out_shape=(jax.ShapeDtypeStruct((m * nc, LANE), U32),
                   jax.ShapeDtypeStruct((TOPK_PAD, m), I32),
                   jax.ShapeDtypeStruct((TOPK_PAD, m), F32),
                   jax.ShapeDtypeStruct((m // tm, e, 1), I32)),
        grid=(m // tm,),
        in_specs=[pl.BlockSpec((tm, d), lambda i: (i, 0)),
                  pl.BlockSpec((1, d), lambda i: (0, 0)),
                  st.mod_spec(comp_scale, tm, col_of=col),
                  st.mod_spec(comp_shift, tm, col_of=col),
                  pl.BlockSpec((e, d), lambda i: (0, 0)),
                  pl.BlockSpec((e, 1), lambda i: (0, 0))],
        out_specs=(pl.BlockSpec((tm * nc, LANE), lambda i: (i, 0)),
                   pl.BlockSpec((TOPK_PAD, tm), lambda i: (0, i)),
                   pl.BlockSpec((TOPK_PAD, tm), lambda i: (0, i)),
                   pl.BlockSpec((None, e, 1), lambda i: (i, 0, 0))),
        compiler_params=_cp("parallel"),
        name="norm_router",
    )(x, gain.reshape(1, d), st.mod, st.mod, w_router_t, router_bias.reshape(e, 1))


def _row_dma_loop(n_rows, start_one):
    def body(i, carry):
        for j in range(DMA_UNROLL):
            start_one(i * DMA_UNROLL + j)
        return carry
    lax.fori_loop(0, n_rows // DMA_UNROLL, body, 0)


def _moe_kernel(be_ref, nb_ref, cnt_ref, first_ref, enext_ref, tok_cur, tok_nxt, asg_ref, h_hbm,
                wg_hbm, wu_hbm, wd_hbm, y_hbm, xbuf, ybuf, stage_a, stage_b, wg_bf, wu_bf, wd_bf,
                sem_in, sem_out, sem_a, sem_b, *, layer):
    i = pl.program_id(0)
    n = pl.num_programs(0)
    nb = nb_ref[0]
    slot = i % 2
    nc = h_hbm.shape[1]
    rows = xbuf.shape[1] // nc
    d_half = stage_a.shape[1]
    f_half = stage_b.shape[1]
    row_at = lambda buf, s, r: buf.at[s, pl.ds(pl.multiple_of(r * nc, nc), nc)]

    def gather(tok_ref, s):
        _row_dma_loop(rows, lambda r: pltpu.make_async_copy(
            h_hbm.at[tok_ref[0, 0, r]], row_at(xbuf, s, r), sem_in.at[s]).start())

    def out_copy(s, r, dst_row):
        return pltpu.make_async_copy(row_at(ybuf, s, r), y_hbm.at[dst_row], sem_out.at[s])

    def one_by_one(count, fn):
        def body(r, carry):
            fn(r)
            return carry
        lax.fori_loop(0, count, body, 0)

    def wait_scatter(block, s):
        cnt = cnt_ref[block]

        @pl.when(cnt == rows)
        def _():
            pltpu.make_async_copy(ybuf.at[s], ybuf.at[s], sem_out.at[s]).wait()

        @pl.when(cnt < rows)
        def _():
            one_by_one(cnt, lambda r: out_copy(s, r, 0).wait())

    def up_copy(w_hbm, e, h):
        return pltpu.make_async_copy(w_hbm.at[layer, e, pl.ds(h * d_half, d_half)], stage_a.at[h],
                                     sem_a.at[h])

    def down_copy(e, h):
        return pltpu.make_async_copy(wd_hbm.at[layer, e, pl.ds(h * f_half, f_half)], stage_b.at[h],
                                     sem_b.at[h])

    def prefetch(e):
        for h in range(2):
            up_copy(wg_hbm, e, h).start()
            down_copy(e, h).start()

    @pl.when(i == 0)
    def _():
        gather(tok_cur, 0)
        prefetch(be_ref[0])

    @pl.when(i + 1 < nb)
    def _():
        gather(tok_nxt, 1 - slot)

    @pl.when(i < nb)
    def _():
        e = be_ref[i]

        @pl.when(first_ref[i] == 1)
        def _():
            for h in range(2):
                up_copy(wg_hbm, e, h).wait()
                wg_bf[pl.ds(h * d_half, d_half), :] = stage_a[h].astype(BF16)
                up_copy(wu_hbm, e, h).start()
                down_copy(e, h).wait()
                wd_bf[pl.ds(h * f_half, f_half), :] = stage_b[h].astype(BF16)
            for h in range(2):
                up_copy(wu_hbm, e, h).wait()
                wu_bf[pl.ds(h * d_half, d_half), :] = stage_a[h].astype(BF16)

            @pl.when(enext_ref[i] >= 0)
            def _():
                prefetch(enext_ref[i])

        pltpu.make_async_copy(xbuf.at[slot], xbuf.at[slot], sem_in.at[slot]).wait()

        @pl.when(i >= 2)
        def _():
            wait_scatter(i - 2, slot)
        x = _load_rows(xbuf.at[slot], rows, nc, BF16)
        g = jnp.dot(x, wg_bf[...], preferred_element_type=F32)
        u = jnp.dot(x, wu_bf[...], preferred_element_type=F32)
        hb = (_silu(g) * u).astype(BF16)
        y = jnp.dot(hb, wd_bf[...], preferred_element_type=F32)
        _store_rows(ybuf.at[slot], _pack_rows(y), rows)
        cnt = cnt_ref[i]
        start_row = lambda r: out_copy(slot, r, asg_ref[0, 0, r]).start()

        @pl.when(cnt == rows)
        def _():
            _row_dma_loop(rows, start_row)

        @pl.when(cnt < rows)
        def _():
            one_by_one(cnt, start_row)

    @pl.when(i == n - 1)
    def _():
        @pl.when(nb >= 1)
        def _():
            wait_scatter(nb - 1, (nb - 1) % 2)

        @pl.when(nb >= 2)
        def _():
            wait_scatter(nb - 2, nb % 2)


def moe_experts(h_rows, row_tok, row_asg, block_e, nb_used, block_cnt, wg, wu, wd, layer):
    n_blocks = block_e.shape[0]
    n_tokens, nc = h_rows.shape[:2]
    d = 2 * nc * LANE
    f = wg.shape[-1]
    tok3 = row_tok.reshape(n_blocks, 1, MOE_ROWS)
    asg3 = row_asg.reshape(n_blocks, 1, MOE_ROWS)
    idx = jnp.arange(n_blocks, dtype=I32)
    in_use = idx < nb_used[0]
    prev_e = jnp.concatenate([jnp.full((1,), -1, I32), block_e[:-1]])
    first = (in_use & (block_e != prev_e)).astype(I32)
    later_first = (first[None, :] == 1) & (idx[None, :] > idx[:, None])
    nxt_pos = jnp.min(jnp.where(later_first, idx[None, :], n_blocks), axis=1)
    e_next = jnp.where(nxt_pos < n_blocks, block_e[jnp.minimum(nxt_pos, n_blocks - 1)], -1).astype(I32)
    smem = lambda f_: pl.BlockSpec((1, 1, MOE_ROWS), f_, memory_space=pltpu.SMEM)
    last = n_blocks - 1
    any_spec = pl.BlockSpec(memory_space=pl.ANY)
    grid_spec = pltpu.PrefetchScalarGridSpec(
        num_scalar_prefetch=5,
        grid=(n_blocks,),
        in_specs=[smem(lambda i, *_: (i, 0, 0)),
                  smem(lambda i, *_: (jnp.minimum(i + 1, last), 0, 0)),
                  smem(lambda i, *_: (i, 0, 0)),
                  any_spec, any_spec, any_spec, any_spec],
        out_specs=any_spec,
        scratch_shapes=[pltpu.VMEM((2, MOE_ROWS * nc, LANE), U32),
                        pltpu.VMEM((2, MOE_ROWS * nc, LANE), U32),
                        pltpu.VMEM((2, d // 2, f), F32),
                        pltpu.VMEM((2, f // 2, d), F32),
                        pltpu.VMEM((d, f), BF16), pltpu.VMEM((d, f), BF16), pltpu.VMEM((f, d), BF16),
                        pltpu.SemaphoreType.DMA((2,)), pltpu.SemaphoreType.DMA((2,)),
                        pltpu.SemaphoreType.DMA((2,)), pltpu.SemaphoreType.DMA((2,))],
    )
    return pl.pallas_call(
        functools.partial(_moe_kernel, layer=layer),
        out_shape=jax.ShapeDtypeStruct((n_tokens * TOP_K, nc, LANE), U32),
        grid_spec=grid_spec,
        compiler_params=pltpu.CompilerParams(dimension_semantics=("arbitrary",),
                                             vmem_limit_bytes=MOE_VMEM_LIMIT),
        name="moe_experts",
    )(block_e, nb_used, block_cnt, first, e_next, tok3, tok3, asg3, h_rows, wg, wu, wd)


def moe_dispatch(idx_ts, counts):
    sizes = [ix.shape[1] for ix in idx_ts]
    n_tokens = sum(sizes)
    flat_e = jnp.concatenate([ix.reshape(-1) for ix in idx_ts])
    nk = n_tokens * TOP_K
    order = jnp.argsort(flat_e).astype(I32)
    start = jnp.cumsum(counts) - counts
    padded = (counts + MOE_ROWS - 1) // MOE_ROWS * MOE_ROWS
    pend = jnp.cumsum(padded)
    pad_start = pend - padded
    n_blocks = -(-nk // MOE_ROWS) + N_EXPERTS
    blk_first = jnp.arange(n_blocks, dtype=I32) * MOE_ROWS
    block_e = jnp.minimum(jnp.sum((pend[None, :] <= blk_first[:, None]).astype(I32), axis=1),
                          N_EXPERTS - 1)
    nb_used = (pend[-1] // MOE_ROWS).astype(I32).reshape(1)
    blk_count, blk_start, blk_pad = lax.optimization_barrier(
        (counts[block_e], start[block_e], pad_start[block_e]))
    within = jnp.arange(MOE_ROWS, dtype=I32)[None, :]
    k_p = blk_first[:, None] + within - blk_pad[:, None]
    valid = ((k_p < blk_count[:, None]) & (blk_first[:, None] < pend[-1])).reshape(-1)
    src = jnp.clip(blk_start[:, None] + k_p, 0, nk - 1).reshape(-1)
    a_p = order[src]
    tok = jnp.zeros_like(a_p)
    base_a, base_t = 0, 0
    for m in sizes:
        local = a_p - base_a
        tok = jnp.where((local >= 0) & (local < TOP_K * m), base_t + local % m, tok)
        base_a += TOP_K * m
        base_t += m
    row_tok = jnp.where(valid, tok, 0).astype(I32)
    row_asg = jnp.where(valid, a_p, 0).astype(I32)
    block_cnt = jnp.sum(valid.reshape(n_blocks, MOE_ROWS).astype(I32), axis=1)
    return row_tok, row_asg, block_e.astype(I32), nb_used, block_cnt


def _finish_kernel(*refs):
    h_ref = refs[0]
    y_refs = refs[1:1 + TOP_K]
    wt_ref, x_ref, g2_ref, wg_ref, wu_ref, wd_ref, o_ref = refs[1 + TOP_K:]
    tm, d = x_ref.shape
    nc = d // 2 // LANE
    h = _load_rows(h_ref, tm, nc, BF16)
    g = jnp.dot(h, wg_ref[...], preferred_element_type=F32)
    u = jnp.dot(h, wu_ref[...], preferred_element_type=F32)
    acc = jnp.dot((_silu(g) * u).astype(BF16), wd_ref[...], preferred_element_type=F32)
    wt = wt_ref[...]
    for k in range(TOP_K):
        acc = acc + wt[:, k:k + 1] * _load_rows(y_refs[k], tm, nc, F32)
    o_ref[...] = x_ref[...] + g2_ref[...] * acc


def moe_finish(x1, st, h_rows2, wts, y_rows2, wgs, wus, wds, row_off, asg_off):
    m, d = x1.shape
    f = wgs.shape[1]
    tm = _tile(m, 256)
    nt = m // tm
    assert row_off % tm == 0 and asg_off % tm == 0
    off = row_off // tm
    nc = d // 2 // LANE
    rows_spec = lambda base: pl.BlockSpec((tm * nc, LANE), lambda i: (i + base, 0))
    whole = lambda a, b: pl.BlockSpec((a, b), lambda i: (0, 0), pipeline_mode=pl.Buffered(1))
    return pl.pallas_call(
        _finish_kernel,
        out_shape=jax.ShapeDtypeStruct((m, d), F32),
        grid=(nt,),
        in_specs=[rows_spec(off)] + [rows_spec(asg_off // tm + k * nt) for k in range(TOP_K)]
                 + [pl.BlockSpec((tm, TOPK_PAD), lambda i: (i + off, 0)),
                    pl.BlockSpec((tm, d), lambda i: (i, 0)),
                    st.mod_spec(5, tm, col_of=lambda i: (i, 0)),
                    whole(d, f), whole(d, f), whole(f, d)],
        out_specs=pl.BlockSpec((tm, d), lambda i: (i, 0)),
        compiler_params=_cp("parallel"),
        name="moe_finish",
    )(h_rows2, *([y_rows2] * TOP_K), wts, x1, st.mod, wgs, wus, wds)


def _residual_epilogue(acc, x, gate):
    return x + gate * acc


def decoder_layer(l, xp, xs, c_all, p, caches_k, caches_v, state_gla, dims):
    batch, seq, db, t_new = dims
    d = xp.shape[1]
    mp, ms = xp.shape[0], xs.shape[0]
    n_all = mp + ms
    qa_w = N_GROUPS * A_WIDTH
    dk, dv = d // 2 // GLA_HEADS, d // GLA_HEADS
    gq, gv = GLA_HEADS * dk, GLA_HEADS * dv
    n_main = 3 * qa_w + 2 * gq + 2 * gv
    nc = d // 2 // LANE

    mod = ada_mod(c_all, p['w_ada'], p['b_ada'], l)
    st_p = Stream(mp, _tile(mp, 1024), mod[:batch].reshape(batch, 1, 6 * d), seq,
                  jnp.arange(seq))
    st_s = Stream(ms, ms, jnp.repeat(mod[batch:], t_new, axis=0).reshape(1, ms, 6 * d), 0,
                  jnp.tile(PAST_LEN + jnp.arange(t_new), db))
    w_lr = p['w_lr'][l]
    w_a, w_b = p['w_branch_a'][l].astype(BF16), p['w_branch_b'][l].astype(BF16)
    w_out = p['w_out'][l].astype(BF16)
    w_router_t = p['w_router'][l].T

    new_k, new_v, new_s, x1s, idxs, wtss, hrs, cnts = {}, {}, {}, {}, [], [], [], []
    for name, st, x in (('p', st_p, xp), ('s', st_s, xs)):
        m, tm = st.rows, st.tm
        h1 = norm_mod(x, p['norm_mix'][l], st, 0, 1)
        proj = matmul(h1, p['w_in'], n_main, tm=tm, tn=_tile(n_main, 1024), out_dtype=BF16,
                      name="in_proj", layer=l)
        la = gla_log_decay(h1, w_lr, p['alpha_up'][l], p['alpha_bias'][l])
        tables = rope_tables(st.pos)
        if name == 'p':
            dils = [dil for _, dil in DIL_GROUPS]
            keeps = [min(w_, seq) for w_, _ in DIL_GROUPS]
            qkv, kc, vc = qk_prep(proj, p['q_norm'][l], p['k_norm'][l], tables, dils, batch, seq,
                                  keeps)
            outs, lses = zip(*[attn_prompt(qkv[3 * g], qkv[3 * g + 1], qkv[3 * g + 2], g)
                               for g in range(N_GROUPS)])
            oa = combine_groups(outs, lses)
            ob, s_new = gla(proj.reshape(batch, seq, -1), la.reshape(batch, seq, gq),
                            p['gla_norm'][l], None, l, dk, dv)
            ob = ob.reshape(m, gv)
            shape4 = lambda a, keep: a.reshape(batch, keep, A_HEADS, A_HEAD_DIM)
            new_k[name] = [shape4(a, keep) for a, keep in zip(kc, keeps)]
            new_v[name] = [shape4(a, keep) for a, keep in zip(vc, keeps)]
        else:
            qkv, kc, vc = qk_prep(proj, p['q_norm'][l], p['k_norm'][l], tables, [1] * N_GROUPS,
                                  1, m, [m] * N_GROUPS)
            new5 = lambda which: jnp.stack(
                [qkv[3 * g + which].reshape(db, t_new, A_HEADS, A_HEAD_DIM)
                 for g in range(N_GROUPS)], axis=2).astype(F32)
            o5, l5 = attn_sample(new5(0), new5(1), new5(2), caches_k, caches_v, l)
            outs = [o5[:, :, g].reshape(1, 1, m, A_WIDTH).astype(BF16) for g in range(N_GROUPS)]
            lses = [jnp.pad(l5[:, :, g, :, 0].reshape(m, A_HEADS),
                            ((0, 0), (0, LANE - A_HEADS))).reshape(1, 1, m, LANE)
                    for g in range(N_GROUPS)]
            oa = combine_groups(outs, lses)
            padc = lambda a: jnp.pad(a.reshape(db, t_new, -1),
                                     ((0, 0), (0, GLA_CHUNK - t_new), (0, 0)))
            ob, s_new = gla(padc(proj), padc(la), p['gla_norm'][l], state_gla, l, dk, dv)
            ob = ob[:, :t_new].reshape(m, gv)
            shape4 = lambda a: a.reshape(db, t_new, A_HEADS, A_HEAD_DIM)
            new_k[name] = [shape4(a) for a in kc]
            new_v[name] = [shape4(a) for a in vc]
        new_s[name] = s_new
        merged = merge_branches(h1, oa, ob, p['w_gates'], w_a, w_b, tm, l)
        tn = _tile(d, 1024)
        x1 = matmul(merged, w_out, d, tm=tm, tn=tn, out_dtype=F32,
                    extra=(x, st.mod),
                    extra_specs=(pl.BlockSpec((tm, tn), lambda j, i: (i, j)),
                                 st.mod_spec(2, tm, tn, col_of=lambda j, i: (i, j))),
                    epilogue=_residual_epilogue, name="out_proj")
        h_rows, idx_t, wt_t, tile_cnt = norm_router(x1, p['norm_ffn'][l], st, 3, 4, w_router_t,
                                                    p['router_bias'][l])
        x1s[name] = x1
        hrs.append(h_rows)
        cnts.append(jnp.sum(tile_cnt[:, :, 0], axis=0))
        idxs.append(idx_t[:TOP_K])
        wtss.append(wt_t)

    wts_all = jnp.concatenate(wtss, axis=1).T
    h_rows2 = jnp.concatenate(hrs, axis=0)
    row_tok, row_asg, block_e, nb_used, block_cnt = moe_dispatch(idxs, cnts[0] + cnts[1])
    y_rows = moe_experts(h_rows2.reshape(n_all, nc, LANE), row_tok, row_asg, block_e, nb_used,
                         block_cnt, p['w_gate_e'], p['w_up_e'], p['w_down_e'], l)
    y_rows2 = y_rows.reshape(-1, LANE)
    wgs, wus, wds = (p['w_gate_s'][l].astype(BF16), p['w_up_s'][l].astype(BF16),
                     p['w_down_s'][l].astype(BF16))
    xp2 = moe_finish(x1s['p'], st_p, h_rows2, wts_all, y_rows2, wgs, wus, wds, 0, 0)
    xs2 = moe_finish(x1s['s'], st_s, h_rows2, wts_all, y_rows2, wgs, wus, wds, mp, TOP_K * mp)
    return xp2, xs2, new_k, new_v, new_s


def kernel(x_prompt, x_sample, cache_k_w128, cache_v_w128, cache_k_w512, cache_v_w512, cache_k_w2048, cache_v_w2048, state_gla, c_prompt, c_sample, w_in, q_norm, k_norm, alpha_up, alpha_bias, gla_norm, w_branch_a, w_branch_b, w_out, norm_mix, norm_ffn, w_ada, b_ada, w_router, router_bias, w_gate_e, w_up_e, w_down_e, w_gate_s, w_up_s, w_down_s):
    batch, seq, d = x_prompt.shape
    db, t_new, _ = x_sample.shape
    depth = w_in.shape[0]
    dims = (batch, seq, db, t_new)
    n_main = w_in.shape[-1] - GLA_RANK - 2 * d
    p = dict(w_in=w_in, w_lr=gate_weights(w_in, n_main, LANE, n_keep=GLA_RANK),
             w_gates=gate_weights(w_in, n_main + GLA_RANK, 2 * d),
             q_norm=q_norm, k_norm=k_norm, alpha_up=alpha_up, alpha_bias=alpha_bias,
             gla_norm=gla_norm, w_branch_a=w_branch_a, w_branch_b=w_branch_b, w_out=w_out,
             norm_mix=norm_mix, norm_ffn=norm_ffn, w_ada=w_ada, b_ada=b_ada, w_router=w_router,
             router_bias=router_bias, w_gate_s=w_gate_s, w_up_s=w_up_s, w_down_s=w_down_s,
             w_gate_e=w_gate_e, w_up_e=w_up_e, w_down_e=w_down_e)
    caches_k = (cache_k_w128, cache_k_w512, cache_k_w2048)
    caches_v = (cache_v_w128, cache_v_w512, cache_v_w2048)
    c_all = jnp.concatenate([c_prompt, c_sample], axis=0)
    xp = x_prompt.reshape(batch * seq, d)
    xs = x_sample.reshape(db * t_new, d)
    ks, vs, ss = {'p': [], 's': []}, {'p': [], 's': []}, {'p': [], 's': []}
    for l in range(depth):
        xp, xs, nk, nv, ns = decoder_layer(l, xp, xs, c_all, p, caches_k, caches_v, state_gla, dims)
        for name in ('p', 's'):
            ks[name].append(nk[name])
            vs[name].append(nv[name])
            ss[name].append(ns[name])
    out = [xp.reshape(batch, seq, d), xs.reshape(db, t_new, d)]
    for name in ('p', 's'):
        for g in range(N_GROUPS):
            out.append(jnp.stack([k_[g] for k_ in ks[name]]))
            out.append(jnp.stack([v_[g] for v_ in vs[name]]))
        out.append(jnp.stack(ss[name]))
    return tuple(out)
```

```python
import functools

import jax
import jax.numpy as jnp
from jax import lax
from jax.experimental import pallas as pl
from jax.experimental.pallas import tpu as pltpu

F32 = jnp.float32
BF16 = jnp.bfloat16
I32 = jnp.int32
U32 = jnp.uint32

DIL_GROUPS = ((128, 1), (512, 4), (2048, 16))
N_GROUPS = 3
A_HEADS = 8
A_HEAD_DIM = 128
A_WIDTH = A_HEADS * A_HEAD_DIM
SPAN = 128
ROPE_DIM = A_HEAD_DIM // 4
ROPE_THETA = 500000.0
GLA_HEADS = 4
GLA_RANK = 16
GLA_TAU = 16.0
GLA_CHUNK = 64
N_EXPERTS = 64
TOP_K = 6
N_EXPERT_GROUPS = 8
TOPK_GROUPS = 4
ROUTED_SCALE = 2.5
EPS = 1e-6
NEG_INF = -1e30
PAST_LEN = 8192

LANE = 128
TOPK_PAD = 8
MOE_ROWS = 256
DMA_UNROLL = 32
VMEM_LIMIT = 56 * 1024 * 1024
MOE_VMEM_LIMIT = 60 * 1024 * 1024


def _cp(*sem):
    return pltpu.CompilerParams(dimension_semantics=sem, vmem_limit_bytes=VMEM_LIMIT)


def _silu(x):
    return x * jax.nn.sigmoid(x)


def _tile(n, pref):
    if n <= pref:
        return n
    t = pref
    while n % t:
        t //= 2
    return t


def _ada_kernel(c_ref, w_ref, b_ref, o_ref):
    s = _silu(c_ref[...]).astype(BF16)
    o_ref[...] = jnp.dot(s, w_ref[...].astype(BF16), preferred_element_type=F32) + b_ref[...]


def ada_mod(c, w, b, layer):
    g, d = c.shape
    n = w.shape[-1]
    tn = _tile(n, 512)
    return pl.pallas_call(
        _ada_kernel,
        out_shape=jax.ShapeDtypeStruct((g, n), F32),
        grid=(n // tn,),
        in_specs=[pl.BlockSpec((g, d), lambda j: (0, 0)),
                  pl.BlockSpec((None, d, tn), lambda j: (layer, 0, j)),
                  pl.BlockSpec((None, 1, tn), lambda j: (layer, 0, j))],
        out_specs=pl.BlockSpec((g, tn), lambda j: (0, j)),
        compiler_params=_cp("arbitrary"),
        name="ada_mod",
    )(c, w, b.reshape(b.shape[0], 1, n))


class Stream:
    def __init__(self, rows, tm, mod, rows_per_group, pos):
        self.rows, self.tm, self.mod, self.rpg, self.pos = rows, tm, mod, rows_per_group, pos
        self.d = mod.shape[-1] // 6
        self.r = mod.shape[1]

    def mod_spec(self, comp, tm, tn=None, col_of=None):
        tn = self.d if tn is None else tn
        nj = self.d // tn
        rpg, r = self.rpg, self.r

        def idx(*g):
            i, j = col_of(*g)
            grp = (i * tm) // rpg if r == 1 else 0
            return (grp, 0, comp * nj + j)
        return pl.BlockSpec((None, r, tn), idx)


def _norm_mod(x, g, sc, sh):
    y = x * lax.rsqrt(jnp.mean(x * x, axis=-1, keepdims=True) + EPS) * g
    return y * (1.0 + sc) + sh


def _norm_mod_kernel(x_ref, g_ref, sc_ref, sh_ref, o_ref):
    o_ref[...] = _norm_mod(x_ref[...], g_ref[...], sc_ref[...], sh_ref[...]).astype(o_ref.dtype)


def norm_mod(x, gain, st, comp_shift, comp_scale):
    m, d = x.shape
    tm = _tile(m, 512) if st.r == 1 else m
    col = lambda i: (i, 0)
    return pl.pallas_call(
        _norm_mod_kernel,
        out_shape=jax.ShapeDtypeStruct((m, d), BF16),
        grid=(m // tm,),
        in_specs=[pl.BlockSpec((tm, d), lambda i: (i, 0)),
                  pl.BlockSpec((1, d), lambda i: (0, 0)),
                  st.mod_spec(comp_scale, tm, col_of=col),
                  st.mod_spec(comp_shift, tm, col_of=col)],
        out_specs=pl.BlockSpec((tm, d), lambda i: (i, 0)),
        compiler_params=_cp("parallel"),
        name="norm_mod",
    )(x, gain.reshape(1, d), st.mod, st.mod)


def _mm_kernel(*refs, n_extra, epilogue, cast_w, w_is_nk):
    x_ref, w_ref = refs[0], refs[1]
    extras = refs[2:2 + n_extra]
    o_ref = refs[2 + n_extra]
    if cast_w:
        wbf = refs[3 + n_extra]

        @pl.when(pl.program_id(1) == 0)
        def _():
            w = w_ref[...]
            wbf[...] = (w.T if w_is_nk else w).astype(BF16)
        w = wbf[...]
    else:
        w = w_ref[...]
    acc = jnp.dot(x_ref[...], w, preferred_element_type=F32)
    if epilogue is not None:
        acc = epilogue(acc, *[e[...] for e in extras])
    o_ref[...] = acc.astype(o_ref.dtype)


def matmul(x, w, n_cols, *, tm, tn, out_dtype, extra=(), extra_specs=(), epilogue=None, name="mm",
           layer=None, w_is_nk=False):
    m, k = x.shape
    cast_w = w.dtype != BF16
    assert cast_w or not w_is_nk
    if layer is None:
        w_spec = pl.BlockSpec((k, tn), lambda j, i: (0, j))
    elif w_is_nk:
        w_spec = pl.BlockSpec((None, tn, k), lambda j, i: (layer, j, 0))
    else:
        w_spec = pl.BlockSpec((None, k, tn), lambda j, i: (layer, 0, j))
    scratch = [pltpu.VMEM((k, tn), BF16)] if cast_w else []
    return pl.pallas_call(
        functools.partial(_mm_kernel, n_extra=len(extra), epilogue=epilogue, cast_w=cast_w,
                          w_is_nk=w_is_nk),
        out_shape=jax.ShapeDtypeStruct((m, n_cols), out_dtype),
        grid=(n_cols // tn, m // tm),
        in_specs=[pl.BlockSpec((tm, k), lambda j, i: (i, 0)), w_spec] + list(extra_specs),
        out_specs=pl.BlockSpec((tm, tn), lambda j, i: (i, j)),
        scratch_shapes=scratch,
        compiler_params=_cp("parallel", "arbitrary"),
        name=name,
    )(x, w, *extra)


def _qk_prep_kernel(*refs, dils, tm, n_tiles, first_tiles, keep_rows):
    (q_ref, k_ref, v_ref, qn_ref, kn_ref, c_ref, sn_ref, sum_ref, exp_ref, swap_ref) = refs[:10]
    qkv_out = refs[10:10 + 3 * N_GROUPS]
    kc_out = refs[10 + 3 * N_GROUPS:10 + 4 * N_GROUPS]
    vc_out = refs[10 + 4 * N_GROUPS:10 + 5 * N_GROUPS]
    scr = refs[10 + 5 * N_GROUPS]
    cos, sin = c_ref[...], sn_ref[...]
    swap = swap_ref[...]
    ti = pl.program_id(0) % n_tiles

    def normed(x_ref, gain_ref):
        x = x_ref[...].astype(F32)
        ss = jnp.dot((x * x).astype(BF16), sum_ref[...], preferred_element_type=F32)
        rs = lax.rsqrt(ss * (1.0 / A_HEAD_DIM) + EPS)
        hi = rs.astype(BF16)
        lo = (rs - hi.astype(F32)).astype(BF16)
        rs_full = (jnp.dot(hi, exp_ref[...], preferred_element_type=F32)
                   + jnp.dot(lo, exp_ref[...], preferred_element_type=F32))
        return x * rs_full * gain_ref[...]

    def rotary(yh):
        partner = jnp.dot(yh.astype(BF16), swap, preferred_element_type=F32)
        return yh * cos + partner * sin

    qy = normed(q_ref, qn_ref)
    ky = normed(k_ref, kn_ref)
    n_slabs = scr.shape[0]
    slab_i = 0
    for g in range(N_GROUPS):
        d = dils[g]
        kb = keep_rows[g]
        k_heads, v_heads = [], []
        for hl in range(A_HEADS):
            h = g * A_HEADS + hl
            sl = slice(h * A_HEAD_DIM, (h + 1) * A_HEAD_DIM)
            ol = slice(hl * A_HEAD_DIM, (hl + 1) * A_HEAD_DIM)
            qh = rotary(qy[:, sl])
            kh = rotary(ky[:, sl])
            vh = v_ref[:, sl].astype(F32)
            k_heads.append(kh)
            v_heads.append(vh)
            for which, val in enumerate((qh, kh, vh)):
                out = qkv_out[3 * g + which]
                if d == 1:
                    out[0, :, ol] = val.astype(BF16)
                else:
                    slab = slab_i % n_slabs
                    slab_i += 1
                    scr[slab] = val
                    for r in range(d):
                        out[r, :, ol] = scr[slab, pl.ds(r, tm // d, stride=d), :].astype(BF16)

        def write_cache(g=g, kb=kb, k_heads=k_heads, v_heads=v_heads):
            for hl in range(A_HEADS):
                rows_of_head = pl.ds(hl, kb, stride=A_HEADS)
                kc_out[g][rows_of_head, :] = k_heads[hl][tm - kb:, :]
                vc_out[g][rows_of_head, :] = v_heads[hl][tm - kb:, :]

        if first_tiles[g] == 0:
            write_cache()
        else:
            pl.when(ti >= first_tiles[g])(write_cache)


def rope_tables(pos):
    half = ROPE_DIM // 2
    inv = ROPE_THETA ** (-jnp.arange(half, dtype=F32) * 2.0 / ROPE_DIM)
    ang = pos.astype(F32)[:, None] * inv[None, :]
    cos, sin = jnp.cos(ang), jnp.sin(ang)
    t = pos.shape[0]
    rest = A_HEAD_DIM - ROPE_DIM
    c = jnp.concatenate([cos, cos, jnp.ones((t, rest), F32)], axis=1)
    s = jnp.concatenate([-sin, sin, jnp.zeros((t, rest), F32)], axis=1)
    return c, s


def _head_matrices(n_heads):
    col = jnp.arange(n_heads * A_HEAD_DIM) // A_HEAD_DIM
    lane = jnp.arange(LANE)
    summing = (col[:, None] == lane[None, :]).astype(BF16)
    half = ROPE_DIM // 2
    src = jnp.arange(A_HEAD_DIM)
    partner = jnp.where(src < half, src + half, jnp.where(src < ROPE_DIM, src - half, -1))
    swap = (src[:, None] == partner[None, :]).astype(BF16)
    return summing, summing.T, swap


def qk_prep(proj, q_norm, k_norm, tables, dils, batch, seq, keeps):
    c, s = tables
    tm = _tile(seq, 256)
    nt = seq // tm
    qa_w = N_GROUPS * A_WIDTH
    n_heads = N_GROUPS * A_HEADS
    summing, expanding, swap = _head_matrices(n_heads)
    first_tiles, keep_rows, n_keep_blocks = [], [], []
    for keep in keeps:
        kb = min(tm, keep)
        assert keep % kb == 0 and (seq - keep) % kb == 0 and tm % kb == 0
        first_tiles.append((seq - keep) // tm)
        keep_rows.append(kb)
        n_keep_blocks.append(keep // kb)
    tab = pl.BlockSpec((tm, A_HEAD_DIM), lambda i: (i % nt, 0))
    whole = lambda a: pl.BlockSpec(a.shape, lambda i: (0, 0))
    blk = lambda j: pl.BlockSpec((tm, qa_w), lambda i: (i, j))
    gains = [jnp.tile(w.reshape(1, -1), (1, n_heads)) for w in (q_norm, k_norm)]
    out_shapes, out_specs = [], []
    for g in range(N_GROUPS):
        d = dils[g]
        for _ in range(3):
            out_shapes.append(jax.ShapeDtypeStruct((batch, d, seq // d, A_WIDTH), BF16))
            out_specs.append(pl.BlockSpec((None, d, tm // d, A_WIDTH),
                                          lambda i: (i // nt, 0, i % nt, 0)))
    for _ in range(2):
        for g in range(N_GROUPS):
            kb, nkb, first = keep_rows[g], n_keep_blocks[g], first_tiles[g]
            out_shapes.append(jax.ShapeDtypeStruct((batch * keeps[g] * A_HEADS, A_HEAD_DIM), F32))
            out_specs.append(pl.BlockSpec(
                (kb * A_HEADS, A_HEAD_DIM),
                lambda i, nkb=nkb, first=first:
                ((i // nt) * nkb + jnp.clip(i % nt - first, 0, nkb - 1), 0)))
    res = pl.pallas_call(
        functools.partial(_qk_prep_kernel, dils=tuple(dils), tm=tm, n_tiles=nt,
                          first_tiles=tuple(first_tiles), keep_rows=tuple(keep_rows)),
        out_shape=tuple(out_shapes),
        grid=(batch * nt,),
        in_specs=[blk(0), blk(1), blk(2), whole(gains[0]), whole(gains[1]), tab, tab,
                  whole(summing), whole(expanding), whole(swap)],
        out_specs=tuple(out_specs),
        scratch_shapes=[pltpu.VMEM((24, tm, A_HEAD_DIM), F32)],
        compiler_params=_cp("arbitrary"),
        name="qk_prep",
    )(proj, proj, proj, gains[0], gains[1], c, s, summing, expanding, swap)
    n3 = 3 * N_GROUPS
    return res[:n3], res[n3:n3 + N_GROUPS], res[n3 + N_GROUPS:]


def _attn_prompt_kernel(q_ref, kp_ref, kc_ref, vp_ref, vc_ref, o_ref, lse_ref, *, n_res, n_qb):
    qi = lax.broadcasted_iota(I32, (SPAN, 2 * SPAN), 0)
    ki = lax.broadcasted_iota(I32, (SPAN, 2 * SPAN), 1)
    band = (ki >= qi) & (ki <= qi + SPAN)
    band_first = band & ((pl.program_id(2) > 0) | (ki >= SPAN))
    lane = lax.broadcasted_iota(I32, (SPAN, LANE), 1)
    scale = A_HEAD_DIM ** -0.5
    for rr in range(n_res):
        for jb in range(n_qb):
            rows = slice(jb * SPAN, (jb + 1) * SPAN)
            prev = slice((jb - 1) * SPAN, jb * SPAN)
            mask = band_first if jb == 0 else band
            lse_all = jnp.zeros((SPAN, LANE), F32)
            for h in range(A_HEADS):
                sl = slice(h * A_HEAD_DIM, (h + 1) * A_HEAD_DIM)
                k_prev = kp_ref[rr, :, sl] if jb == 0 else kc_ref[rr, prev, sl]
                v_prev = vp_ref[rr, :, sl] if jb == 0 else vc_ref[rr, prev, sl]
                kh = jnp.concatenate([k_prev, kc_ref[rr, rows, sl]], axis=0)
                vh = jnp.concatenate([v_prev, vc_ref[rr, rows, sl]], axis=0)
                s = lax.dot_general(q_ref[rr, rows, sl], kh, (((1,), (1,)), ((), ())),
                                    preferred_element_type=F32) * scale
                s = jnp.where(mask, s, NEG_INF)
                m = jnp.max(s, axis=-1, keepdims=True)
                p = jnp.exp(s - m)
                l = jnp.sum(p, axis=-1, keepdims=True)
                o = jnp.dot(p.astype(BF16), vh, preferred_element_type=F32) / l
                o_ref[rr, rows, sl] = o.astype(o_ref.dtype)
                lse_all = jnp.where(lane == h, m + jnp.log(l), lse_all)
            lse_ref[rr, rows, :] = lse_all


def attn_prompt(q_g, k_g, v_g, g):
    batch, dil, l, _ = q_g.shape
    nblk = l // SPAN
    n_qb = min(4, nblk)
    n_res = min(max(4 // n_qb, 1), dil)
    assert nblk % n_qb == 0 and dil % n_res == 0
    cur = lambda w: pl.BlockSpec((None, n_res, n_qb * SPAN, w), lambda b, r, i: (b, r, i, 0))
    prev = pl.BlockSpec((None, n_res, SPAN, A_WIDTH),
                        lambda b, r, i: (b, r, jnp.maximum(i * n_qb - 1, 0), 0))
    return pl.pallas_call(
        functools.partial(_attn_prompt_kernel, n_res=n_res, n_qb=n_qb),
        out_shape=(jax.ShapeDtypeStruct((batch, dil, l, A_WIDTH), BF16),
                   jax.ShapeDtypeStruct((batch, dil, l, LANE), F32)),
        grid=(batch, dil // n_res, nblk // n_qb),
        in_specs=[cur(A_WIDTH), prev, cur(A_WIDTH), prev, cur(A_WIDTH)],
        out_specs=(cur(A_WIDTH), cur(LANE)),
        compiler_params=_cp("parallel", "parallel", "arbitrary"),
        name=f"attn_prompt_g{g}",
    )(q_g, k_g, k_g, v_g, v_g)


def _attn_sample_kernel(q_ref, k_ref, v_ref, ck0, cv0, ck1, cv1, ck2, cv2, o_ref, l_ref, *, t_new):
    caches = ((ck0, cv0), (ck1, cv1), (ck2, cv2))
    scale = A_HEAD_DIM ** -0.5
    jidx = lax.broadcasted_iota(I32, (SPAN, A_HEADS, 1), 0)
    for g, (_, dil) in enumerate(DIL_GROUPS):
        ck, cv = caches[g]
        for t in range(t_new):
            res = 0 if dil == 1 else t
            q = q_ref[t, g]
            s_c = jnp.sum(ck[:, res] * q[None], axis=-1, keepdims=True) * scale
            if dil == 1:
                s_c = jnp.where(jidx >= t, s_c, NEG_INF)
            new = range(t + 1) if dil == 1 else (t,)
            s_n = [jnp.sum(q * k_ref[u, g], axis=-1, keepdims=True) * scale for u in new]
            m = jnp.max(s_c, axis=0)
            for s in s_n:
                m = jnp.maximum(m, s)
            p_c = jnp.exp(s_c - m[None])
            lsum = jnp.sum(p_c, axis=0)
            o = jnp.sum(p_c * cv[:, res], axis=0)
            for u, s in zip(new, s_n):
                p_u = jnp.exp(s - m)
                lsum = lsum + p_u
                o = o + p_u * v_ref[u, g]
            o_ref[t, g] = o / lsum
            l_ref[t, g] = jnp.broadcast_to(m + jnp.log(lsum), (A_HEADS, A_HEAD_DIM))


def attn_sample(q5, k5, v5, caches_k, caches_v, layer):
    db, t_new = q5.shape[:2]
    new_spec = pl.BlockSpec((None, t_new, N_GROUPS, A_HEADS, A_HEAD_DIM), lambda b: (b, 0, 0, 0, 0))
    ins, specs = [q5, k5, v5], [new_spec] * 3
    for g, (win, dil) in enumerate(DIL_GROUPS):
        for c in (caches_k[g], caches_v[g]):
            depth, _, wc, hh, ee = c.shape
            assert wc == win and (hh, ee) == (A_HEADS, A_HEAD_DIM) and (dil == 1 or t_new <= dil)
            nres = 1 if dil == 1 else t_new
            ins.append(c.reshape(depth, db, wc // dil, dil, hh, ee))
            specs.append(pl.BlockSpec((None, None, SPAN, nres, hh, ee),
                                      lambda b: (layer, b, 0, 0, 0, 0)))
    return pl.pallas_call(
        functools.partial(_attn_sample_kernel, t_new=t_new),
        out_shape=(jax.ShapeDtypeStruct(q5.shape, F32), jax.ShapeDtypeStruct(q5.shape, F32)),
        grid=(db,),
        in_specs=specs,
        out_specs=(new_spec, new_spec),
        compiler_params=_cp("parallel"),
        name="attn_sample",
    )(*ins)


def _combine_kernel(*refs, dils, tm):
    o_refs, l_refs, out_ref, scr_o, scr_l = refs[:3], refs[3:6], refs[6], refs[7], refs[8]
    ls = []
    for g, d in enumerate(dils):
        if d == 1:
            ls.append(l_refs[g][0])
        else:
            for r in range(d):
                scr_l[g, pl.ds(r, tm // d, stride=d), :] = l_refs[g][r]
            ls.append(scr_l[g])
    m = jnp.maximum(jnp.maximum(ls[0], ls[1]), ls[2])
    es = [jnp.exp(x - m) for x in ls]
    den = es[0] + es[1] + es[2]
    rr = lax.broadcasted_iota(I32, (LANE, A_WIDTH), 0)
    cc = lax.broadcasted_iota(I32, (LANE, A_WIDTH), 1)
    expand = (cc // A_HEAD_DIM == rr).astype(BF16)
    wxs = []
    for e in es:
        w = e / den
        hi = w.astype(BF16)
        lo = (w - hi.astype(F32)).astype(BF16)
        wxs.append(jnp.dot(hi, expand, preferred_element_type=F32)
                   + jnp.dot(lo, expand, preferred_element_type=F32))
    for h in range(A_HEADS):
        hs = slice(h * A_HEAD_DIM, (h + 1) * A_HEAD_DIM)
        acc = None
        for g, d in enumerate(dils):
            if d == 1:
                o_tok = o_refs[g][0, :, hs].astype(F32)
            else:
                for r in range(d):
                    scr_o[g, h, pl.ds(r, tm // d, stride=d), :] = o_refs[g][r, :, hs].astype(F32)
                o_tok = scr_o[g, h]
            term = wxs[g][:, hs] * o_tok
            acc = term if acc is None else acc + term
        out_ref[:, hs] = acc.astype(out_ref.dtype)


def combine_groups(os_, ls_):
    batch = os_[0].shape[0]
    dils = tuple(o.shape[1] for o in os_)
    seq = os_[0].shape[1] * os_[0].shape[2]
    tm = _tile(seq, 256)
    nt = seq // tm
    specs = []
    for width in (A_WIDTH, LANE):
        for d in dils:
            specs.append(pl.BlockSpec((None, d, tm // d, width), lambda i: (i // nt, 0, i % nt, 0)))
    return pl.pallas_call(
        functools.partial(_combine_kernel, dils=dils, tm=tm),
        out_shape=jax.ShapeDtypeStruct((batch * seq, A_WIDTH), BF16),
        grid=(batch * nt,),
        in_specs=specs,
        out_specs=pl.BlockSpec((tm, A_WIDTH), lambda i: (i, 0)),
        scratch_shapes=[pltpu.VMEM((N_GROUPS, A_HEADS, tm, A_HEAD_DIM), F32),
                        pltpu.VMEM((N_GROUPS, tm, LANE), F32)],
        compiler_params=_cp("parallel"),
        name="combine_groups",
    )(*os_, *ls_)


def _alpha_kernel(h_ref, w1_ref, w2_ref, b_ref, o_ref):
    lr = jnp.dot(h_ref[...], w1_ref[...], preferred_element_type=F32)
    z = jnp.dot(lr.astype(BF16), w2_ref[...], preferred_element_type=F32) + b_ref[...]
    log_sig = jnp.minimum(z, 0.0) - jnp.log(1.0 + jnp.exp(-jnp.abs(z)))
    o_ref[...] = log_sig / GLA_TAU


def gla_log_decay(h, w1, alpha_up, alpha_bias):
    m, d = h.shape
    gq = alpha_up.shape[1]
    w2 = jnp.pad(alpha_up, ((0, LANE - GLA_RANK), (0, 0))).astype(BF16)
    tm = _tile(m, 512)
    return pl.pallas_call(
        _alpha_kernel,
        out_shape=jax.ShapeDtypeStruct((m, gq), F32),
        grid=(m // tm,),
        in_specs=[pl.BlockSpec((tm, d), lambda i: (i, 0)),
                  pl.BlockSpec((d, LANE), lambda i: (0, 0)),
                  pl.BlockSpec((LANE, gq), lambda i: (0, 0)),
                  pl.BlockSpec((1, gq), lambda i: (0, 0))],
        out_specs=pl.BlockSpec((tm, gq), lambda i: (i, 0)),
        compiler_params=_cp("parallel"),
        name="gla_log_decay",
    )(h, w1, w2, alpha_bias.reshape(1, gq))


def _gla_kernel(*refs, chunk, dk, dv, has_s0):
    if has_s0:
        q_ref, k_ref, v0_ref, v1_ref, la_ref, r0_ref, r1_ref, gn_ref, s0_ref, ob_ref, s_ref = refs
    else:
        q_ref, k_ref, v0_ref, v1_ref, la_ref, r0_ref, r1_ref, gn_ref, ob_ref, s_ref = refs

    @pl.when(pl.program_id(1) == 0)
    def _():
        s_ref[...] = s0_ref[...].astype(F32) if has_s0 else jnp.zeros(s_ref.shape, F32)

    ri = lax.broadcasted_iota(I32, (chunk, chunk), 0)
    ci = lax.broadcasted_iota(I32, (chunk, chunk), 1)
    causal = ri >= ci
    tri = causal.astype(F32)
    gn = gn_ref[...]
    per_block = GLA_HEADS // 2
    for h in range(GLA_HEADS):
        ks = slice(h * dk, (h + 1) * dk)
        vs = slice((h % per_block) * dv, (h % per_block + 1) * dv)
        v_ref, r_ref = (v0_ref, r0_ref) if h < per_block else (v1_ref, r1_ref)
        la = la_ref[:, ks]
        b = jnp.dot(tri, la, precision=lax.Precision.HIGHEST,
                    preferred_element_type=F32)
        b_last = b[chunk - 1:chunk, :]
        q = q_ref[:, ks].astype(F32) * (dk ** -0.5)
        k = k_ref[:, ks].astype(F32)
        v = v_ref[:, vs]
        q_dec = (q * jnp.exp(b)).astype(BF16)
        k_inv = (k * jnp.exp(-b)).astype(BF16)
        k_dec = (k * jnp.exp(b_last - b)).astype(BF16)
        att = lax.dot_general(q_dec, k_inv, (((1,), (1,)), ((), ())), preferred_element_type=F32)
        att = jnp.where(causal, att, 0.0).astype(BF16)
        s_old = s_ref[h]
        o = (jnp.dot(att, v, preferred_element_type=F32)
             + jnp.dot(q_dec, s_old.astype(BF16), preferred_element_type=F32))
        kv = lax.dot_general(k_dec, v, (((0,), (0,)), ((), ())), preferred_element_type=F32)
        decay = jnp.sum(jnp.transpose(la), axis=1, keepdims=True)
        s_ref[h] = jnp.exp(decay) * s_old + kv
        y = o * lax.rsqrt(jnp.mean(o * o, axis=-1, keepdims=True) + EPS) * gn
        ob_ref[:, h * dv:(h + 1) * dv] = (y * _silu(r_ref[:, vs].astype(F32))).astype(ob_ref.dtype)


def gla(proj3, la3, gla_norm, s0, layer, dk, dv):
    bsz, t, w = proj3.shape
    chunk = GLA_CHUNK
    qa_w = N_GROUPS * A_WIDTH
    gq, gv = GLA_HEADS * dk, GLA_HEADS * dv
    assert gv == 2 * gq and (3 * qa_w) % gq == 0
    base = 3 * qa_w // gq
    cspec = lambda off: pl.BlockSpec((None, chunk, gq), lambda b, c: (b, c, off))
    ins = [proj3, proj3, proj3, proj3, la3, proj3, proj3, gla_norm.reshape(1, dv)]
    specs = [cspec(base), cspec(base + 1), cspec(base + 2), cspec(base + 3), cspec(0),
             cspec(base + 4), cspec(base + 5), pl.BlockSpec((1, dv), lambda b, c: (0, 0))]
    if s0 is not None:
        ins.append(s0)
        specs.append(pl.BlockSpec((None, None, GLA_HEADS, dk, dv), lambda b, c: (layer, b, 0, 0, 0)))
    return pl.pallas_call(
        functools.partial(_gla_kernel, chunk=chunk, dk=dk, dv=dv, has_s0=s0 is not None),
        out_shape=(jax.ShapeDtypeStruct((bsz, t, gv), BF16),
                   jax.ShapeDtypeStruct((bsz, GLA_HEADS, dk, dv), F32)),
        grid=(bsz, t // chunk),
        in_specs=specs,
        out_specs=(pl.BlockSpec((None, chunk, gv), lambda b, c: (b, c, 0)),
                   pl.BlockSpec((None, GLA_HEADS, dk, dv), lambda b, c: (b, 0, 0, 0))),
        compiler_params=_cp("parallel", "arbitrary"),
        name="gla",
    )(*ins)


def _gate_weights_kernel(w_ref, nxt_ref, o_ref, *, shift, n_keep):
    tn = o_ref.shape[1]
    both = jnp.concatenate([w_ref[...], nxt_ref[...]], axis=0)
    cut = both[shift:shift + tn, :].T
    if n_keep < tn:
        lane = lax.broadcasted_iota(I32, cut.shape, 1)
        cut = jnp.where(lane < n_keep, cut, 0.0)
    o_ref[...] = cut.astype(o_ref.dtype)


def gate_weights(w_nk, col0, n_cols, n_keep=None):
    depth, n, k = w_nk.shape
    tn = _tile(n_cols, 512)
    base = col0 // LANE * LANE
    shift = col0 - base
    n_keep = tn if n_keep is None else n_keep
    assert base % tn == 0 and n_cols % tn == 0 and shift % 8 == 0
    assert col0 + min(n_cols, n_keep) <= n
    last_blk = (n - 1) // LANE
    return pl.pallas_call(
        functools.partial(_gate_weights_kernel, shift=shift, n_keep=n_keep),
        out_shape=jax.ShapeDtypeStruct((depth, k, n_cols), BF16),
        grid=(depth, n_cols // tn),
        in_specs=[pl.BlockSpec((None, tn, k), lambda l, j: (l, base // tn + j, 0)),
                  pl.BlockSpec((None, LANE, k),
                               lambda l, j: (l, jnp.minimum((base + (j + 1) * tn) // LANE, last_blk), 0))],
        out_specs=pl.BlockSpec((None, k, tn), lambda l, j: (l, 0, j)),
        compiler_params=_cp("parallel", "parallel"),
        name="gate_weights",
    )(w_nk, w_nk)


def _merge_kernel(h_ref, oa_ref, ob_ref, wga_ref, wgb_ref, wa_ref, wb_ref, o_ref):
    h = h_ref[...]
    ga = jnp.dot(h, wga_ref[...], preferred_element_type=F32)
    gb = jnp.dot(h, wgb_ref[...], preferred_element_type=F32)
    a = jnp.dot(oa_ref[...], wa_ref[...], preferred_element_type=F32)
    b = jnp.dot(ob_ref[...], wb_ref[...], preferred_element_type=F32)
    o_ref[...] = (jax.nn.sigmoid(ga) * a + jax.nn.sigmoid(gb) * b).astype(o_ref.dtype)


def merge_branches(h, oa, ob, w_gates, w_a, w_b, tm, layer):
    m, d = h.shape
    tn = _tile(d, 512)
    nj = d // tn
    row = lambda width: pl.BlockSpec((tm, width), lambda j, i: (i, 0))
    wcol = lambda rows, off: pl.BlockSpec((rows, tn), lambda j, i: (0, j + off))
    gcol = lambda off: pl.BlockSpec((None, d, tn), lambda j, i: (layer, 0, j + off))
    return pl.pallas_call(
        _merge_kernel,
        out_shape=jax.ShapeDtypeStruct((m, d), BF16),
        grid=(nj, m // tm),
        in_specs=[row(d), row(oa.shape[1]), row(ob.shape[1]),
                  gcol(0), gcol(nj), wcol(w_a.shape[0], 0), wcol(w_b.shape[0], 0)],
        out_specs=pl.BlockSpec((tm, tn), lambda j, i: (i, j)),
        compiler_params=_cp("parallel", "arbitrary"),
        name="merge_branches",
    )(h, oa, ob, w_gates, w_gates, w_a, w_b)


def _pack_rows(x):
    half = x.shape[1] // 2
    lo = lax.bitcast_convert_type(x[:, :half].astype(BF16).astype(F32), U32)
    hi = lax.bitcast_convert_type(x[:, half:].astype(BF16).astype(F32), U32)
    return (lo >> 16) | (hi & jnp.uint32(0xFFFF0000))


def _unpack_chunk(u):
    return (lax.bitcast_convert_type(u << 16, F32),
            lax.bitcast_convert_type(u & jnp.uint32(0xFFFF0000), F32))


def _store_rows(ref, packed, rows):
    n_chunks = packed.shape[1] // LANE
    for c in range(n_chunks):
        ref[pl.ds(c, rows, stride=n_chunks), :] = packed[:, c * LANE:(c + 1) * LANE]


def _load_rows(ref, rows, n_chunks, dtype, start=0):
    lo, hi = [], []
    for c in range(n_chunks):
        a, b = _unpack_chunk(ref[pl.ds(start + c, rows, stride=n_chunks), :])
        lo.append(a.astype(dtype))
        hi.append(b.astype(dtype))
    return jnp.concatenate(lo + hi, axis=1)


def _norm_router_kernel(x_ref, g_ref, sc_ref, sh_ref, wr_ref, rb_ref, hr_ref, idx_ref, wt_ref,
                        cnt_ref):
    h = _norm_mod(x_ref[...], g_ref[...], sc_ref[...], sh_ref[...])
    tm = h.shape[0]
    _store_rows(hr_ref, _pack_rows(h), tm)
    per = N_EXPERTS // N_EXPERT_GROUPS
    logits = lax.dot_general(wr_ref[...], h, (((1,), (1,)), ((), ())),
                             precision=lax.Precision.HIGHEST, preferred_element_type=F32)
    scores = jax.nn.sigmoid(logits)
    biased = scores + rb_ref[...]
    bg = biased.reshape(N_EXPERT_GROUPS, per, tm)
    ip = lax.broadcasted_iota(I32, bg.shape, 1)
    m1 = jnp.max(bg, axis=1, keepdims=True)
    i1 = jnp.min(jnp.where(bg == m1, ip, per), axis=1, keepdims=True)
    m2 = jnp.max(jnp.where(ip == i1, -jnp.inf, bg), axis=1, keepdims=True)
    grp = (m1 + m2)[:, 0, :]
    gi = lax.broadcasted_iota(I32, grp.shape, 0)
    rank = jnp.zeros(grp.shape, I32)
    for g in range(N_EXPERT_GROUPS):
        other = grp[g:g + 1, :]
        ahead = (other > grp) | ((other == grp) & (g < gi))
        rank = rank + ahead.astype(I32)
    sel = (rank < TOPK_GROUPS)[:, None, :]
    masked = jnp.where(sel, bg, -jnp.inf).reshape(N_EXPERTS, tm)
    ie = lax.broadcasted_iota(I32, masked.shape, 0)
    ids, ws = [], []
    chosen = jnp.zeros(masked.shape, F32)
    for _ in range(TOP_K):
        mx = jnp.max(masked, axis=0, keepdims=True)
        ik = jnp.min(jnp.where(masked == mx, ie, N_EXPERTS), axis=0, keepdims=True)
        hit = ie == ik
        ids.append(ik)
        ws.append(jnp.sum(jnp.where(hit, scores, 0.0), axis=0, keepdims=True))
        masked = jnp.where(hit, -jnp.inf, masked)
        chosen = chosen + hit.astype(F32)
    cnt_ref[...] = jnp.sum(chosen, axis=1, keepdims=True).astype(I32)
    tot = ws[0]
    for wk in ws[1:]:
        tot = tot + wk
    pad = TOPK_PAD - TOP_K
    idx_ref[...] = jnp.concatenate(ids + [jnp.zeros((pad, tm), I32)], axis=0)
    wt_ref[...] = jnp.concatenate([wk / tot * ROUTED_SCALE for wk in ws]
                                  + [jnp.zeros((pad, tm), F32)], axis=0)


def norm_router(x, gain, st, comp_shift, comp_scale, w_router_t, router_bias):
    m, d = x.shape
    tm = _tile(m, 256) if st.r == 1 else m
    nc = d // 2 // LANE
    col = lambda i: (i, 0)
    e = w_router_t.shape[0]
    return pl.pallas_call(
        _norm_router_kernel,
        out_shape=(jax.ShapeDtypeStruct((m * nc, LANE), U32),
                   jax.ShapeDtypeStruct((TOPK_PAD, m), I32),
                   jax.ShapeDtypeStruct((TOPK_PAD, m), F32),
                   jax.ShapeDtypeStruct((m // tm, e, 1), I32)),
        grid=(m // tm,),
        in_specs=[pl.BlockSpec((tm, d), lambda i: (i, 0)),
                  pl.BlockSpec((1, d), lambda i: (0, 0)),
                  st.mod_spec(comp_scale, tm, col_of=col),
                  st.mod_spec(comp_shift, tm, col_of=col),
                  pl.BlockSpec((e, d), lambda i: (0, 0)),
                  pl.BlockSpec((e, 1), lambda i: (0, 0))],
        out_specs=(pl.BlockSpec((tm * nc, LANE), lambda i: (i, 0)),
                   pl.BlockSpec((TOPK_PAD, tm), lambda i: (0, i)),
                   pl.BlockSpec((TOPK_PAD, tm), lambda i: (0, i)),
                   pl.BlockSpec((None, e, 1), lambda i: (i, 0, 0))),
        compiler_params=_cp("parallel"),
        name="norm_router",
    )(x, gain.reshape(1, d), st.mod, st.mod, w_router_t, router_bias.reshape(e, 1))


def _row_dma_loop(n_rows, start_one):
    def body(i, carry):
        for j in range(DMA_UNROLL):
            start_one(i * DMA_UNROLL + j)
        return carry
    lax.fori_loop(0, n_rows // DMA_UNROLL, body, 0)


def _moe_kernel(be_ref, nb_ref, cnt_ref, first_ref, enext_ref, tok_cur, tok_nxt, asg_ref, h_hbm,
                wg_hbm, wu_hbm, wd_hbm, y_hbm, xbuf, ybuf, stage_a, stage_b, wg_bf, wu_bf, wd_bf,
                sem_in, sem_out, sem_a, sem_b, *, layer):
    i = pl.program_id(0)
    n = pl.num_programs(0)
    nb = nb_ref[0]
    slot = i % 2
    nc = h_hbm.shape[1]
    rows = xbuf.shape[1] // nc
    d_half = stage_a.shape[1]
    f_half = stage_b.shape[1]
    row_at = lambda buf, s, r: buf.at[s, pl.ds(pl.multiple_of(r * nc, nc), nc)]

    def gather(tok_ref, s):
        _row_dma_loop(rows, lambda r: pltpu.make_async_copy(
            h_hbm.at[tok_ref[0, 0, r]], row_at(xbuf, s, r), sem_in.at[s]).start())

    def out_copy(s, r, dst_row):
        return pltpu.make_async_copy(row_at(ybuf, s, r), y_hbm.at[dst_row], sem_out.at[s])

    def one_by_one(count, fn):
        def body(r, carry):
            fn(r)
            return carry
        lax.fori_loop(0, count, body, 0)

    def wait_scatter(block, s):
        cnt = cnt_ref[block]

        @pl.when(cnt == rows)
        def _():
            pltpu.make_async_copy(ybuf.at[s], ybuf.at[s], sem_out.at[s]).wait()

        @pl.when(cnt < rows)
        def _():
            one_by_one(cnt, lambda r: out_copy(s, r, 0).wait())

    def up_copy(w_hbm, e, h):
        return pltpu.make_async_copy(w_hbm.at[layer, e, pl.ds(h * d_half, d_half)], stage_a.at[h],
                                     sem_a.at[h])

    def down_copy(e, h):
        return pltpu.make_async_copy(wd_hbm.at[layer, e, pl.ds(h * f_half, f_half)], stage_b.at[h],
                                     sem_b.at[h])

    def prefetch(e):
        for h in range(2):
            up_copy(wg_hbm, e, h).start()
            down_copy(e, h).start()

    @pl.when(i == 0)
    def _():
        gather(tok_cur, 0)
        prefetch(be_ref[0])

    @pl.when(i + 1 < nb)
    def _():
        gather(tok_nxt, 1 - slot)

    @pl.when(i < nb)
    def _():
        e = be_ref[i]

        @pl.when(first_ref[i] == 1)
        def _():
            for h in range(2):
                up_copy(wg_hbm, e, h).wait()
                wg_bf[pl.ds(h * d_half, d_half), :] = stage_a[h].astype(BF16)
                up_copy(wu_hbm, e, h).start()
                down_copy(e, h).wait()
                wd_bf[pl.ds(h * f_half, f_half), :] = stage_b[h].astype(BF16)
            for h in range(2):
                up_copy(wu_hbm, e, h).wait()
                wu_bf[pl.ds(h * d_half, d_half), :] = stage_a[h].astype(BF16)

            @pl.when(enext_ref[i] >= 0)
            def _():
                prefetch(enext_ref[i])

        pltpu.make_async_copy(xbuf.at[slot], xbuf.at[slot], sem_in.at[slot]).wait()

        @pl.when(i >= 2)
        def _():
            wait_scatter(i - 2, slot)
        x = _load_rows(xbuf.at[slot], rows, nc, BF16)
        g = jnp.dot(x, wg_bf[...], preferred_element_type=F32)
        u = jnp.dot(x, wu_bf[...], preferred_element_type=F32)
        hb = (_silu(g) * u).astype(BF16)
        y = jnp.dot(hb, wd_bf[...], preferred_element_type=F32)
        _store_rows(ybuf.at[slot], _pack_rows(y), rows)
        cnt = cnt_ref[i]
        start_row = lambda r: out_copy(slot, r, asg_ref[0, 0, r]).start()

        @pl.when(cnt == rows)
        def _():
            _row_dma_loop(rows, start_row)

        @pl.when(cnt < rows)
        def _():
            one_by_one(cnt, start_row)

    @pl.when(i == n - 1)
    def _():
        @pl.when(nb >= 1)
        def _():
            wait_scatter(nb - 1, (nb - 1) % 2)

        @pl.when(nb >= 2)
        def _():
            wait_scatter(nb - 2, nb % 2)


def moe_experts(h_rows, row_tok, row_asg, block_e, nb_used, block_cnt, wg, wu, wd, layer):
    n_blocks = block_e.shape[0]
    n_tokens, nc = h_rows.shape[:2]
    d = 2 * nc * LANE
    f = wg.shape[-1]
    tok3 = row_tok.reshape(n_blocks, 1, MOE_ROWS)
    asg3 = row_asg.reshape(n_blocks, 1, MOE_ROWS)
    idx = jnp.arange(n_blocks, dtype=I32)
    in_use = idx < nb_used[0]
    prev_e = jnp.concatenate([jnp.full((1,), -1, I32), block_e[:-1]])
    first = (in_use & (block_e != prev_e)).astype(I32)
    later_first = (first[None, :] == 1) & (idx[None, :] > idx[:, None])
    nxt_pos = jnp.min(jnp.where(later_first, idx[None, :], n_blocks), axis=1)
    e_next = jnp.where(nxt_pos < n_blocks, block_e[jnp.minimum(nxt_pos, n_blocks - 1)], -1).astype(I32)
    smem = lambda f_: pl.BlockSpec((1, 1, MOE_ROWS), f_, memory_space=pltpu.SMEM)
    last = n_blocks - 1
    any_spec = pl.BlockSpec(memory_space=pl.ANY)
    grid_spec = pltpu.PrefetchScalarGridSpec(
        num_scalar_prefetch=5,
        grid=(n_blocks,),
        in_specs=[smem(lambda i, *_: (i, 0, 0)),
                  smem(lambda i, *_: (jnp.minimum(i + 1, last), 0, 0)),
                  smem(lambda i, *_: (i, 0, 0)),
                  any_spec, any_spec, any_spec, any_spec],
        out_specs=any_spec,
        scratch_shapes=[pltpu.VMEM((2, MOE_ROWS * nc, LANE), U32),
                        pltpu.VMEM((2, MOE_ROWS * nc, LANE), U32),
                        pltpu.VMEM((2, d // 2, f), F32),
                        pltpu.VMEM((2, f // 2, d), F32),
                        pltpu.VMEM((d, f), BF16), pltpu.VMEM((d, f), BF16), pltpu.VMEM((f, d), BF16),
                        pltpu.SemaphoreType.DMA((2,)), pltpu.SemaphoreType.DMA((2,)),
                        pltpu.SemaphoreType.DMA((2,)), pltpu.SemaphoreType.DMA((2,))],
    )
    return pl.pallas_call(
        functools.partial(_moe_kernel, layer=layer),
        out_shape=jax.ShapeDtypeStruct((n_tokens * TOP_K, nc, LANE), U32),
        grid_spec=grid_spec,
        compiler_params=pltpu.CompilerParams(dimension_semantics=("arbitrary",),
                                             vmem_limit_bytes=MOE_VMEM_LIMIT),
        name="moe_experts",
    )(block_e, nb_used, block_cnt, first, e_next, tok3, tok3, asg3, h_rows, wg, wu, wd)


def moe_dispatch(idx_ts, counts):
    sizes = [ix.shape[1] for ix in idx_ts]
    n_tokens = sum(sizes)
    flat_e = jnp.concatenate([ix.reshape(-1) for ix in idx_ts])
    nk = n_tokens * TOP_K
    order = jnp.argsort(flat_e).astype(I32)
    start = jnp.cumsum(counts) - counts
    padded = (counts + MOE_ROWS - 1) // MOE_ROWS * MOE_ROWS
    pend = jnp.cumsum(padded)
    pad_start = pend - padded
    n_blocks = -(-nk // MOE_ROWS) + N_EXPERTS
    blk_first = jnp.arange(n_blocks, dtype=I32) * MOE_ROWS
    block_e = jnp.minimum(jnp.sum((pend[None, :] <= blk_first[:, None]).astype(I32), axis=1),
                          N_EXPERTS - 1)
    nb_used = (pend[-1] // MOE_ROWS).astype(I32).reshape(1)
    blk_count, blk_start, blk_pad = lax.optimization_barrier(
        (counts[block_e], start[block_e], pad_start[block_e]))
    within = jnp.arange(MOE_ROWS, dtype=I32)[None, :]
    k_p = blk_first[:, None] + within - blk_pad[:, None]
    valid = ((k_p < blk_count[:, None]) & (blk_first[:, None] < pend[-1])).reshape(-1)
    src = jnp.clip(blk_start[:, None] + k_p, 0, nk - 1).reshape(-1)
    a_p = order[src]
    tok = jnp.zeros_like(a_p)
    base_a, base_t = 0, 0
    for m in sizes:
        local = a_p - base_a
        tok = jnp.where((local >= 0) & (local < TOP_K * m), base_t + local % m, tok)
        base_a += TOP_K * m
        base_t += m
    row_tok = jnp.where(valid, tok, 0).astype(I32)
    row_asg = jnp.where(valid, a_p, 0).astype(I32)
    block_cnt = jnp.sum(valid.reshape(n_blocks, MOE_ROWS).astype(I32), axis=1)
    return row_tok, row_asg, block_e.astype(I32), nb_used, block_cnt


def _finish_kernel(*refs):
    h_ref = refs[0]
    y_refs = refs[1:1 + TOP_K]
    wt_ref, x_ref, g2_ref, wg_ref, wu_ref, wd_ref, o_ref = refs[1 + TOP_K:]
    tm, d = x_ref.shape
    nc = d // 2 // LANE
    h = _load_rows(h_ref, tm, nc, BF16)
    g = jnp.dot(h, wg_ref[...], preferred_element_type=F32)
    u = jnp.dot(h, wu_ref[...], preferred_element_type=F32)
    acc = jnp.dot((_silu(g) * u).astype(BF16), wd_ref[...], preferred_element_type=F32)
    wt = wt_ref[...]
    for k in range(TOP_K):
        acc = acc + wt[:, k:k + 1] * _load_rows(y_refs[k], tm, nc, F32)
    o_ref[...] = x_ref[...] + g2_ref[...] * acc


def moe_finish(x1, st, h_rows2, wts, y_rows2, wgs, wus, wds, row_off, asg_off):
    m, d = x1.shape
    f = wgs.shape[1]
    tm = _tile(m, 256)
    nt = m // tm
    assert row_off % tm == 0 and asg_off % tm == 0
    off = row_off // tm
    nc = d // 2 // LANE
    rows_spec = lambda base: pl.BlockSpec((tm * nc, LANE), lambda i: (i + base, 0))
    whole = lambda a, b: pl.BlockSpec((a, b), lambda i: (0, 0), pipeline_mode=pl.Buffered(1))
    return pl.pallas_call(
        _finish_kernel,
        out_shape=jax.ShapeDtypeStruct((m, d), F32),
        grid=(nt,),
        in_specs=[rows_spec(off)] + [rows_spec(asg_off // tm + k * nt) for k in range(TOP_K)]
                 + [pl.BlockSpec((tm, TOPK_PAD), lambda i: (i + off, 0)),
                    pl.BlockSpec((tm, d), lambda i: (i, 0)),
                    st.mod_spec(5, tm, col_of=lambda i: (i, 0)),
                    whole(d, f), whole(d, f), whole(f, d)],
        out_specs=pl.BlockSpec((tm, d), lambda i: (i, 0)),
        compiler_params=_cp("parallel"),
        name="moe_finish",
    )(h_rows2, *([y_rows2] * TOP_K), wts, x1, st.mod, wgs, wus, wds)


def _residual_epilogue(acc, x, gate):
    return x + gate * acc


def decoder_layer(l, xp, xs, c_all, p, caches_k, caches_v, state_gla, dims):
    batch, seq, db, t_new = dims
    d = xp.shape[1]
    mp, ms = xp.shape[0], xs.shape[0]
    n_all = mp + ms
    qa_w = N_GROUPS * A_WIDTH
    dk, dv = d // 2 // GLA_HEADS, d // GLA_HEADS
    gq, gv = GLA_HEADS * dk, GLA_HEADS * dv
    n_main = 3 * qa_w + 2 * gq + 2 * gv
    nc = d // 2 // LANE

    mod = ada_mod(c_all, p['w_ada'], p['b_ada'], l)
    st_p = Stream(mp, _tile(mp, 1024), mod[:batch].reshape(batch, 1, 6 * d), seq,
                  jnp.arange(seq))
    st_s = Stream(ms, ms, jnp.repeat(mod[batch:], t_new, axis=0).reshape(1, ms, 6 * d), 0,
                  jnp.tile(PAST_LEN + jnp.arange(t_new), db))
    w_lr = p['w_lr'][l]
    w_a, w_b = p['w_branch_a'][l].astype(BF16), p['w_branch_b'][l].astype(BF16)
    w_out = p['w_out'][l].astype(BF16)
    w_router_t = p['w_router'][l].T

    new_k, new_v, new_s, x1s, idxs, wtss, hrs, cnts = {}, {}, {}, {}, [], [], [], []
    for name, st, x in (('p', st_p, xp), ('s', st_s, xs)):
        m, tm = st.rows, st.tm
        h1 = norm_mod(x, p['norm_mix'][l], st, 0, 1)
        proj = matmul(h1, p['w_in'], n_main, tm=tm, tn=_tile(n_main, 1024), out_dtype=BF16,
                      name="in_proj", layer=l, w_is_nk=True)
        la = gla_log_decay(h1, w_lr, p['alpha_up'][l], p['alpha_bias'][l])
        tables = rope_tables(st.pos)
        if name == 'p':
            dils = [dil for _, dil in DIL_GROUPS]
            keeps = [min(w_, seq) for w_, _ in DIL_GROUPS]
            qkv, kc, vc = qk_prep(proj, p['q_norm'][l], p['k_norm'][l], tables, dils, batch, seq,
                                  keeps)
            outs, lses = zip(*[attn_prompt(qkv[3 * g], qkv[3 * g + 1], qkv[3 * g + 2], g)
                               for g in range(N_GROUPS)])
            oa = combine_groups(outs, lses)
            ob, s_new = gla(proj.reshape(batch, seq, -1), la.reshape(batch, seq, gq),
                            p['gla_norm'][l], None, l, dk, dv)
            ob = ob.reshape(m, gv)
            shape4 = lambda a, keep: a.reshape(batch, keep, A_HEADS, A_HEAD_DIM)
            new_k[name] = [shape4(a, keep) for a, keep in zip(kc, keeps)]
            new_v[name] = [shape4(a, keep) for a, keep in zip(vc, keeps)]
        else:
            qkv, kc, vc = qk_prep(proj, p['q_norm'][l], p['k_norm'][l], tables, [1] * N_GROUPS,
                                  1, m, [m] * N_GROUPS)
            new5 = lambda which: jnp.stack(
                [qkv[3 * g + which].reshape(db, t_new, A_HEADS, A_HEAD_DIM)
                 for g in range(N_GROUPS)], axis=2).astype(F32)
            o5, l5 = attn_sample(new5(0), new5(1), new5(2), caches_k, caches_v, l)
            outs = [o5[:, :, g].reshape(1, 1, m, A_WIDTH).astype(BF16) for g in range(N_GROUPS)]
            lses = [jnp.pad(l5[:, :, g, :, 0].reshape(m, A_HEADS),
                            ((0, 0), (0, LANE - A_HEADS))).reshape(1, 1, m, LANE)
                    for g in range(N_GROUPS)]
            oa = combine_groups(outs, lses)
            padc = lambda a: jnp.pad(a.reshape(db, t_new, -1),
                                     ((0, 0), (0, GLA_CHUNK - t_new), (0, 0)))
            ob, s_new = gla(padc(proj), padc(la), p['gla_norm'][l], state_gla, l, dk, dv)
            ob = ob[:, :t_new].reshape(m, gv)
            shape4 = lambda a: a.reshape(db, t_new, A_HEADS, A_HEAD_DIM)
            new_k[name] = [shape4(a) for a in kc]
            new_v[name] = [shape4(a) for a in vc]
        new_s[name] = s_new
        merged = merge_branches(h1, oa, ob, p['w_gates'], w_a, w_b, tm, l)
        tn = _tile(d, 1024)
        x1 = matmul(merged, w_out, d, tm=tm, tn=tn, out_dtype=F32,
                    extra=(x, st.mod),
                    extra_specs=(pl.BlockSpec((tm, tn), lambda j, i: (i, j)),
                                 st.mod_spec(2, tm, tn, col_of=lambda j, i: (i, j))),
                    epilogue=_residual_epilogue, name="out_proj")
        h_rows, idx_t, wt_t, tile_cnt = norm_router(x1, p['norm_ffn'][l], st, 3, 4, w_router_t,
                                                    p['router_bias'][l])
        x1s[name] = x1
        hrs.append(h_rows)
        cnts.append(jnp.sum(tile_cnt[:, :, 0], axis=0))
        idxs.append(idx_t[:TOP_K])
        wtss.append(wt_t)

    wts_all = jnp.concatenate(wtss, axis=1).T
    h_rows2 = jnp.concatenate(hrs, axis=0)
    row_tok, row_asg, block_e, nb_used, block_cnt = moe_dispatch(idxs, cnts[0] + cnts[1])
    y_rows = moe_experts(h_rows2.reshape(n_all, nc, LANE), row_tok, row_asg, block_e, nb_used,
                         block_cnt, p['w_gate_e'], p['w_up_e'], p['w_down_e'], l)
    y_rows2 = y_rows.reshape(-1, LANE)
    wgs, wus, wds = (p['w_gate_s'][l].astype(BF16), p['w_up_s'][l].astype(BF16),
                     p['w_down_s'][l].astype(BF16))
    xp2 = moe_finish(x1s['p'], st_p, h_rows2, wts_all, y_rows2, wgs, wus, wds, 0, 0)
    xs2 = moe_finish(x1s['s'], st_s, h_rows2, wts_all, y_rows2, wgs, wus, wds, mp, TOP_K * mp)
    return xp2, xs2, new_k, new_v, new_s


def kernel(x_prompt, x_sample, cache_k_w128, cache_v_w128, cache_k_w512, cache_v_w512, cache_k_w2048, cache_v_w2048, state_gla, c_prompt, c_sample, w_in, q_norm, k_norm, alpha_up, alpha_bias, gla_norm, w_branch_a, w_branch_b, w_out, norm_mix, norm_ffn, w_ada, b_ada, w_router, router_bias, w_gate_e, w_up_e, w_down_e, w_gate_s, w_up_s, w_down_s):
    batch, seq, d = x_prompt.shape
    db, t_new, _ = x_sample.shape
    depth = w_in.shape[0]
    dims = (batch, seq, db, t_new)
    n_main = w_in.shape[-1] - GLA_RANK - 2 * d
    w_in_nk = jnp.swapaxes(w_in, 1, 2)
    p = dict(w_in=w_in_nk, w_lr=gate_weights(w_in_nk, n_main, LANE, n_keep=GLA_RANK),
             w_gates=gate_weights(w_in_nk, n_main + GLA_RANK, 2 * d),
             q_norm=q_norm, k_norm=k_norm, alpha_up=alpha_up, alpha_bias=alpha_bias,
             gla_norm=gla_norm, w_branch_a=w_branch_a, w_branch_b=w_branch_b, w_out=w_out,
             norm_mix=norm_mix, norm_ffn=norm_ffn, w_ada=w_ada, b_ada=b_ada, w_router=w_router,
             router_bias=router_bias, w_gate_s=w_gate_s, w_up_s=w_up_s, w_down_s=w_down_s,
             w_gate_e=w_gate_e, w_up_e=w_up_e, w_down_e=w_down_e)
    caches_k = (cache_k_w128, cache_k_w512, cache_k_w2048)
    caches_v = (cache_v_w128, cache_v_w512, cache_v_w2048)
    c_all = jnp.concatenate([c_prompt, c_sample], axis=0)
    xp = x_prompt.reshape(batch * seq, d)
    xs = x_sample.reshape(db * t_new, d)
    ks, vs, ss = {'p': [], 's': []}, {'p': [], 's': []}, {'p': [], 's': []}
    for l in range(depth):
        xp, xs, nk, nv, ns = decoder_layer(l, xp, xs, c_all, p, caches_k, caches_v, state_gla, dims)
        for name in ('p', 's'):
            ks[name].append(nk[name])
            vs[name].append(nv[name])
            ss[name].append(ns[name])
    out = [xp.reshape(batch, seq, d), xs.reshape(db, t_new, d)]
    for name in ('p', 's'):
        for g in range(N_GROUPS):
            out.append(jnp.stack([k_[g] for k_ in ks[name]]))
            out.append(jnp.stack([v_[g] for v_ in vs[name]]))
        out.append(jnp.stack(ss[name]))
    return tuple(out)
```

```python
import functools

import jax
import jax.numpy as jnp
from jax import lax
from jax.experimental import pallas as pl
from jax.experimental.pallas import tpu as pltpu

F32 = jnp.float32
BF16 = jnp.bfloat16
I32 = jnp.int32
U32 = jnp.uint32

DIL_GROUPS = ((128, 1), (512, 4), (2048, 16))
N_GROUPS = 3
A_HEADS = 8
A_HEAD_DIM = 128
A_WIDTH = A_HEADS * A_HEAD_DIM
SPAN = 128
ROPE_DIM = A_HEAD_DIM // 4
ROPE_THETA = 500000.0
GLA_HEADS = 4
GLA_RANK = 16
GLA_TAU = 16.0
GLA_CHUNK = 64
N_EXPERTS = 64
TOP_K = 6
N_EXPERT_GROUPS = 8
TOPK_GROUPS = 4
ROUTED_SCALE = 2.5
EPS = 1e-6
NEG_INF = -1e30
PAST_LEN = 8192

LANE = 128
TOPK_PAD = 8
MOE_ROWS = 256
DMA_UNROLL = 32
VMEM_LIMIT = 56 * 1024 * 1024
MOE_VMEM_LIMIT = 60 * 1024 * 1024


def _cp(*sem):
    return pltpu.CompilerParams(dimension_semantics=sem, vmem_limit_bytes=VMEM_LIMIT)


def _silu(x):
    return x * jax.nn.sigmoid(x)


def _tile(n, pref):
    if n <= pref:
        return n
    t = pref
    while n % t:
        t //= 2
    return t


def _ada_kernel(c_ref, w_ref, b_ref, o_ref):
    s = _silu(c_ref[...]).astype(BF16)
    o_ref[...] = jnp.dot(s, w_ref[...].astype(BF16), preferred_element_type=F32) + b_ref[...]


def ada_mod(c, w, b, layer):
    g, d = c.shape
    n = w.shape[-1]
    tn = _tile(n, 512)
    return pl.pallas_call(
        _ada_kernel,
        out_shape=jax.ShapeDtypeStruct((g, n), F32),
        grid=(n // tn,),
        in_specs=[pl.BlockSpec((g, d), lambda j: (0, 0)),
                  pl.BlockSpec((None, d, tn), lambda j: (layer, 0, j)),
                  pl.BlockSpec((None, 1, tn), lambda j: (layer, 0, j))],
        out_specs=pl.BlockSpec((g, tn), lambda j: (0, j)),
        compiler_params=_cp("arbitrary"),
        name="ada_mod",
    )(c, w, b.reshape(b.shape[0], 1, n))


class Stream:
    def __init__(self, rows, tm, mod, rows_per_group, pos):
        self.rows, self.tm, self.mod, self.rpg, self.pos = rows, tm, mod, rows_per_group, pos
        self.d = mod.shape[-1] // 6
        self.r = mod.shape[1]

    def mod_spec(self, comp, tm, tn=None, col_of=None):
        tn = self.d if tn is None else tn
        nj = self.d // tn
        rpg, r = self.rpg, self.r

        def idx(*g):
            i, j = col_of(*g)
            grp = (i * tm) // rpg if r == 1 else 0
            return (grp, 0, comp * nj + j)
        return pl.BlockSpec((None, r, tn), idx)


def _norm_mod(x, g, sc, sh):
    y = x * lax.rsqrt(jnp.mean(x * x, axis=-1, keepdims=True) + EPS) * g
    return y * (1.0 + sc) + sh


def _norm_mod_kernel(x_ref, g_ref, sc_ref, sh_ref, o_ref):
    o_ref[...] = _norm_mod(x_ref[...], g_ref[...], sc_ref[...], sh_ref[...]).astype(o_ref.dtype)


def norm_mod(x, gain, st, comp_shift, comp_scale):
    m, d = x.shape
    tm = _tile(m, 512) if st.r == 1 else m
    col = lambda i: (i, 0)
    return pl.pallas_call(
        _norm_mod_kernel,
        out_shape=jax.ShapeDtypeStruct((m, d), BF16),
        grid=(m // tm,),
        in_specs=[pl.BlockSpec((tm, d), lambda i: (i, 0)),
                  pl.BlockSpec((1, d), lambda i: (0, 0)),
                  st.mod_spec(comp_scale, tm, col_of=col),
                  st.mod_spec(comp_shift, tm, col_of=col)],
        out_specs=pl.BlockSpec((tm, d), lambda i: (i, 0)),
        compiler_params=_cp("parallel"),
        name="norm_mod",
    )(x, gain.reshape(1, d), st.mod, st.mod)


def _mm_kernel(*refs, n_extra, epilogue, cast_w, w_is_nk):
    x_ref, w_ref = refs[0], refs[1]
    extras = refs[2:2 + n_extra]
    o_ref = refs[2 + n_extra]
    if cast_w:
        wbf = refs[3 + n_extra]

        @pl.when(pl.program_id(1) == 0)
        def _():
            w = w_ref[...]
            wbf[...] = (w.T if w_is_nk else w).astype(BF16)
        w = wbf[...]
    else:
        w = w_ref[...]
    acc = jnp.dot(x_ref[...], w, preferred_element_type=F32)
    if epilogue is not None:
        acc = epilogue(acc, *[e[...] for e in extras])
    o_ref[...] = acc.astype(o_ref.dtype)


def matmul(x, w, n_cols, *, tm, tn, out_dtype, extra=(), extra_specs=(), epilogue=None, name="mm",
           layer=None, w_is_nk=False):
    m, k = x.shape
    cast_w = w.dtype != BF16
    assert cast_w or not w_is_nk
    if layer is None:
        w_spec = pl.BlockSpec((k, tn), lambda j, i: (0, j))
    elif w_is_nk:
        w_spec = pl.BlockSpec((None, tn, k), lambda j, i: (layer, j, 0))
    else:
        w_spec = pl.BlockSpec((None, k, tn), lambda j, i: (layer, 0, j))
    scratch = [pltpu.VMEM((k, tn), BF16)] if cast_w else []
    return pl.pallas_call(
        functools.partial(_mm_kernel, n_extra=len(extra), epilogue=epilogue, cast_w=cast_w,
                          w_is_nk=w_is_nk),
        out_shape=jax.ShapeDtypeStruct((m, n_cols), out_dtype),
        grid=(n_cols // tn, m // tm),
        in_specs=[pl.BlockSpec((tm, k), lambda j, i: (i, 0)), w_spec] + list(extra_specs),
        out_specs=pl.BlockSpec((tm, tn), lambda j, i: (i, j)),
        scratch_shapes=scratch,
        compiler_params=_cp("parallel", "arbitrary"),
        name=name,
    )(x, w, *extra)


def _qk_prep_kernel(*refs, dils, tm, n_tiles, first_tiles, keep_rows):
    (q_ref, k_ref, v_ref, qn_ref, kn_ref, c_ref, sn_ref, sum_ref, exp_ref, swap_ref) = refs[:10]
    qkv_out = refs[10:10 + 3 * N_GROUPS]
    kc_out = refs[10 + 3 * N_GROUPS:10 + 4 * N_GROUPS]
    vc_out = refs[10 + 4 * N_GROUPS:10 + 5 * N_GROUPS]
    scr = refs[10 + 5 * N_GROUPS]
    cos, sin = c_ref[...], sn_ref[...]
    swap = swap_ref[...]
    ti = pl.program_id(0) % n_tiles

    def normed(x_ref, gain_ref):
        x = x_ref[...].astype(F32)
        ss = jnp.dot((x * x).astype(BF16), sum_ref[...], preferred_element_type=F32)
        rs = lax.rsqrt(ss * (1.0 / A_HEAD_DIM) + EPS)
        hi = rs.astype(BF16)
        lo = (rs - hi.astype(F32)).astype(BF16)
        rs_full = (jnp.dot(hi, exp_ref[...], preferred_element_type=F32)
                   + jnp.dot(lo, exp_ref[...], preferred_element_type=F32))
        return x * rs_full * gain_ref[...]

    def rotary(yh):
        partner = jnp.dot(yh.astype(BF16), swap, preferred_element_type=F32)
        return yh * cos + partner * sin

    qy = normed(q_ref, qn_ref)
    ky = normed(k_ref, kn_ref)
    n_slabs = scr.shape[0]
    slab_i = 0
    for g in range(N_GROUPS):
        d = dils[g]
        kb = keep_rows[g]
        k_heads, v_heads = [], []
        for hl in range(A_HEADS):
            h = g * A_HEADS + hl
            sl = slice(h * A_HEAD_DIM, (h + 1) * A_HEAD_DIM)
            ol = slice(hl * A_HEAD_DIM, (hl + 1) * A_HEAD_DIM)
            qh = rotary(qy[:, sl])
            kh = rotary(ky[:, sl])
            vh = v_ref[:, sl].astype(F32)
            k_heads.append(kh)
            v_heads.append(vh)
            for which, val in enumerate((qh, kh, vh)):
                out = qkv_out[3 * g + which]
                if d == 1:
                    out[0, :, ol] = val.astype(BF16)
                else:
                    slab = slab_i % n_slabs
                    slab_i += 1
                    scr[slab] = val
                    for r in range(d):
                        out[r, :, ol] = scr[slab, pl.ds(r, tm // d, stride=d), :].astype(BF16)

        def write_cache(g=g, kb=kb, k_heads=k_heads, v_heads=v_heads):
            for hl in range(A_HEADS):
                rows_of_head = pl.ds(hl, kb, stride=A_HEADS)
                kc_out[g][rows_of_head, :] = k_heads[hl][tm - kb:, :]
                vc_out[g][rows_of_head, :] = v_heads[hl][tm - kb:, :]

        if first_tiles[g] == 0:
            write_cache()
        else:
            pl.when(ti >= first_tiles[g])(write_cache)


def rope_tables(pos):
    half = ROPE_DIM // 2
    inv = ROPE_THETA ** (-jnp.arange(half, dtype=F32) * 2.0 / ROPE_DIM)
    ang = pos.astype(F32)[:, None] * inv[None, :]
    cos, sin = jnp.cos(ang), jnp.sin(ang)
    t = pos.shape[0]
    rest = A_HEAD_DIM - ROPE_DIM
    c = jnp.concatenate([cos, cos, jnp.ones((t, rest), F32)], axis=1)
    s = jnp.concatenate([-sin, sin, jnp.zeros((t, rest), F32)], axis=1)
    return c, s


def _head_matrices(n_heads):
    col = jnp.arange(n_heads * A_HEAD_DIM) // A_HEAD_DIM
    lane = jnp.arange(LANE)
    summing = (col[:, None] == lane[None, :]).astype(BF16)
    half = ROPE_DIM // 2
    src = jnp.arange(A_HEAD_DIM)
    partner = jnp.where(src < half, src + half, jnp.where(src < ROPE_DIM, src - half, -1))
    swap = (src[:, None] == partner[None, :]).astype(BF16)
    return summing, summing.T, swap


def qk_prep(proj, q_norm, k_norm, tables, dils, batch, seq, keeps):
    c, s = tables
    tm = _tile(seq, 256)
    nt = seq // tm
    qa_w = N_GROUPS * A_WIDTH
    n_heads = N_GROUPS * A_HEADS
    summing, expanding, swap = _head_matrices(n_heads)
    first_tiles, keep_rows, n_keep_blocks = [], [], []
    for keep in keeps:
        kb = min(tm, keep)
        assert keep % kb == 0 and (seq - keep) % kb == 0 and tm % kb == 0
        first_tiles.append((seq - keep) // tm)
        keep_rows.append(kb)
        n_keep_blocks.append(keep // kb)
    tab = pl.BlockSpec((tm, A_HEAD_DIM), lambda i: (i % nt, 0))
    whole = lambda a: pl.BlockSpec(a.shape, lambda i: (0, 0))
    blk = lambda j: pl.BlockSpec((tm, qa_w), lambda i: (i, j))
    gains = [jnp.tile(w.reshape(1, -1), (1, n_heads)) for w in (q_norm, k_norm)]
    out_shapes, out_specs = [], []
    for g in range(N_GROUPS):
        d = dils[g]
        for _ in range(3):
            out_shapes.append(jax.ShapeDtypeStruct((batch, d, seq // d, A_WIDTH), BF16))
            out_specs.append(pl.BlockSpec((None, d, tm // d, A_WIDTH),
                                          lambda i: (i // nt, 0, i % nt, 0)))
    for _ in range(2):
        for g in range(N_GROUPS):
            kb, nkb, first = keep_rows[g], n_keep_blocks[g], first_tiles[g]
            out_shapes.append(jax.ShapeDtypeStruct((batch * keeps[g] * A_HEADS, A_HEAD_DIM), F32))
            out_specs.append(pl.BlockSpec(
                (kb * A_HEADS, A_HEAD_DIM),
                lambda i, nkb=nkb, first=first:
                ((i // nt) * nkb + jnp.clip(i % nt - first, 0, nkb - 1), 0)))
    res = pl.pallas_call(
        functools.partial(_qk_prep_kernel, dils=tuple(dils), tm=tm, n_tiles=nt,
                          first_tiles=tuple(first_tiles), keep_rows=tuple(keep_rows)),
        out_shape=tuple(out_shapes),
        grid=(batch * nt,),
        in_specs=[blk(0), blk(1), blk(2), whole(gains[0]), whole(gains[1]), tab, tab,
                  whole(summing), whole(expanding), whole(swap)],
        out_specs=tuple(out_specs),
        scratch_shapes=[pltpu.VMEM((24, tm, A_HEAD_DIM), F32)],
        compiler_params=_cp("arbitrary"),
        name="qk_prep",
    )(proj, proj, proj, gains[0], gains[1], c, s, summing, expanding, swap)
    n3 = 3 * N_GROUPS
    return res[:n3], res[n3:n3 + N_GROUPS], res[n3 + N_GROUPS:]


def _attn_prompt_kernel(q_ref, kp_ref, kc_ref, vp_ref, vc_ref, o_ref, lse_ref, *, n_res, n_qb):
    qi = lax.broadcasted_iota(I32, (SPAN, 2 * SPAN), 0)
    ki = lax.broadcasted_iota(I32, (SPAN, 2 * SPAN), 1)
    band = (ki >= qi) & (ki <= qi + SPAN)
    band_first = band & ((pl.program_id(2) > 0) | (ki >= SPAN))
    lane = lax.broadcasted_iota(I32, (SPAN, LANE), 1)
    scale = A_HEAD_DIM ** -0.5
    for rr in range(n_res):
        for jb in range(n_qb):
            rows = slice(jb * SPAN, (jb + 1) * SPAN)
            prev = slice((jb - 1) * SPAN, jb * SPAN)
            mask = band_first if jb == 0 else band
            lse_all = jnp.zeros((SPAN, LANE), F32)
            for h in range(A_HEADS):
                sl = slice(h * A_HEAD_DIM, (h + 1) * A_HEAD_DIM)
                k_prev = kp_ref[rr, :, sl] if jb == 0 else kc_ref[rr, prev, sl]
                v_prev = vp_ref[rr, :, sl] if jb == 0 else vc_ref[rr, prev, sl]
                kh = jnp.concatenate([k_prev, kc_ref[rr, rows, sl]], axis=0)
                vh = jnp.concatenate([v_prev, vc_ref[rr, rows, sl]], axis=0)
                s = lax.dot_general(q_ref[rr, rows, sl], kh, (((1,), (1,)), ((), ())),
                                    preferred_element_type=F32) * scale
                s = jnp.where(mask, s, NEG_INF)
                m = jnp.max(s, axis=-1, keepdims=True)
                p = jnp.exp(s - m)
                l = jnp.sum(p, axis=-1, keepdims=True)
                o = jnp.dot(p.astype(BF16), vh, preferred_element_type=F32) / l
                o_ref[rr, rows, sl] = o.astype(o_ref.dtype)
                lse_all = jnp.where(lane == h, m + jnp.log(l), lse_all)
            lse_ref[rr, rows, :] = lse_all


def attn_prompt(q_g, k_g, v_g, g):
    batch, dil, l, _ = q_g.shape
    nblk = l // SPAN
    n_qb = min(4, nblk)
    n_res = min(max(4 // n_qb, 1), dil)
    assert nblk % n_qb == 0 and dil % n_res == 0
    cur = lambda w: pl.BlockSpec((None, n_res, n_qb * SPAN, w), lambda b, r, i: (b, r, i, 0))
    prev = pl.BlockSpec((None, n_res, SPAN, A_WIDTH),
                        lambda b, r, i: (b, r, jnp.maximum(i * n_qb - 1, 0), 0))
    return pl.pallas_call(
        functools.partial(_attn_prompt_kernel, n_res=n_res, n_qb=n_qb),
        out_shape=(jax.ShapeDtypeStruct((batch, dil, l, A_WIDTH), BF16),
                   jax.ShapeDtypeStruct((batch, dil, l, LANE), F32)),
        grid=(batch, dil // n_res, nblk // n_qb),
        in_specs=[cur(A_WIDTH), prev, cur(A_WIDTH), prev, cur(A_WIDTH)],
        out_specs=(cur(A_WIDTH), cur(LANE)),
        compiler_params=_cp("parallel", "parallel", "arbitrary"),
        name=f"attn_prompt_g{g}",
    )(q_g, k_g, k_g, v_g, v_g)


def _attn_sample_kernel(q_ref, k_ref, v_ref, ck0, cv0, ck1, cv1, ck2, cv2, o_ref, l_ref, *, t_new):
    caches = ((ck0, cv0), (ck1, cv1), (ck2, cv2))
    scale = A_HEAD_DIM ** -0.5
    jidx = lax.broadcasted_iota(I32, (SPAN, A_HEADS, 1), 0)
    for g, (_, dil) in enumerate(DIL_GROUPS):
        ck, cv = caches[g]
        for t in range(t_new):
            res = 0 if dil == 1 else t
            q = q_ref[t, g]
            s_c = jnp.sum(ck[:, res] * q[None], axis=-1, keepdims=True) * scale
            if dil == 1:
                s_c = jnp.where(jidx >= t, s_c, NEG_INF)
            new = range(t + 1) if dil == 1 else (t,)
            s_n = [jnp.sum(q * k_ref[u, g], axis=-1, keepdims=True) * scale for u in new]
            m = jnp.max(s_c, axis=0)
            for s in s_n:
                m = jnp.maximum(m, s)
            p_c = jnp.exp(s_c - m[None])
            lsum = jnp.sum(p_c, axis=0)
            o = jnp.sum(p_c * cv[:, res], axis=0)
            for u, s in zip(new, s_n):
                p_u = jnp.exp(s - m)
                lsum = lsum + p_u
                o = o + p_u * v_ref[u, g]
            o_ref[t, g] = o / lsum
            l_ref[t, g] = jnp.broadcast_to(m + jnp.log(lsum), (A_HEADS, A_HEAD_DIM))


def attn_sample(q5, k5, v5, caches_k, caches_v, layer):
    db, t_new = q5.shape[:2]
    new_spec = pl.BlockSpec((None, t_new, N_GROUPS, A_HEADS, A_HEAD_DIM), lambda b: (b, 0, 0, 0, 0))
    ins, specs = [q5, k5, v5], [new_spec] * 3
    for g, (win, dil) in enumerate(DIL_GROUPS):
        for c in (caches_k[g], caches_v[g]):
            depth, _, wc, hh, ee = c.shape
            assert wc == win and (hh, ee) == (A_HEADS, A_HEAD_DIM) and (dil == 1 or t_new <= dil)
            nres = 1 if dil == 1 else t_new
            ins.append(c.reshape(depth, db, wc // dil, dil, hh, ee))
            specs.append(pl.BlockSpec((None, None, SPAN, nres, hh, ee),
                                      lambda b: (layer, b, 0, 0, 0, 0)))
    return pl.pallas_call(
        functools.partial(_attn_sample_kernel, t_new=t_new),
        out_shape=(jax.ShapeDtypeStruct(q5.shape, F32), jax.ShapeDtypeStruct(q5.shape, F32)),
        grid=(db,),
        in_specs=specs,
        out_specs=(new_spec, new_spec),
        compiler_params=_cp("parallel"),
        name="attn_sample",
    )(*ins)


def _combine_kernel(*refs, dils, tm):
    o_refs, l_refs, out_ref, scr_o, scr_l = refs[:3], refs[3:6], refs[6], refs[7], refs[8]
    ls = []
    for g, d in enumerate(dils):
        if d == 1:
            ls.append(l_refs[g][0])
        else:
            for r in range(d):
                scr_l[g, pl.ds(r, tm // d, stride=d), :] = l_refs[g][r]
            ls.append(scr_l[g])
    m = jnp.maximum(jnp.maximum(ls[0], ls[1]), ls[2])
    es = [jnp.exp(x - m) for x in ls]
    den = es[0] + es[1] + es[2]
    rr = lax.broadcasted_iota(I32, (LANE, A_WIDTH), 0)
    cc = lax.broadcasted_iota(I32, (LANE, A_WIDTH), 1)
    expand = (cc // A_HEAD_DIM == rr).astype(BF16)
    wxs = []
    for e in es:
        w = e / den
        hi = w.astype(BF16)
        lo = (w - hi.astype(F32)).astype(BF16)
        wxs.append(jnp.dot(hi, expand, preferred_element_type=F32)
                   + jnp.dot(lo, expand, preferred_element_type=F32))
    for h in range(A_HEADS):
        hs = slice(h * A_HEAD_DIM, (h + 1) * A_HEAD_DIM)
        acc = None
        for g, d in enumerate(dils):
            if d == 1:
                o_tok = o_refs[g][0, :, hs].astype(F32)
            else:
                for r in range(d):
                    scr_o[g, h, pl.ds(r, tm // d, stride=d), :] = o_refs[g][r, :, hs].astype(F32)
                o_tok = scr_o[g, h]
            term = wxs[g][:, hs] * o_tok
            acc = term if acc is None else acc + term
        out_ref[:, hs] = acc.astype(out_ref.dtype)


def combine_groups(os_, ls_):
    batch = os_[0].shape[0]
    dils = tuple(o.shape[1] for o in os_)
    seq = os_[0].shape[1] * os_[0].shape[2]
    tm = _tile(seq, 256)
    nt = seq // tm
    specs = []
    for width in (A_WIDTH, LANE):
        for d in dils:
            specs.append(pl.BlockSpec((None, d, tm // d, width), lambda i: (i // nt, 0, i % nt, 0)))
    return pl.pallas_call(
        functools.partial(_combine_kernel, dils=dils, tm=tm),
        out_shape=jax.ShapeDtypeStruct((batch * seq, A_WIDTH), BF16),
        grid=(batch * nt,),
        in_specs=specs,
        out_specs=pl.BlockSpec((tm, A_WIDTH), lambda i: (i, 0)),
        scratch_shapes=[pltpu.VMEM((N_GROUPS, A_HEADS, tm, A_HEAD_DIM), F32),
                        pltpu.VMEM((N_GROUPS, tm, LANE), F32)],
        compiler_params=_cp("parallel"),
        name="combine_groups",
    )(*os_, *ls_)


def _alpha_kernel(h_ref, w1_ref, w2_ref, b_ref, o_ref):
    lr = jnp.dot(h_ref[...], w1_ref[...], preferred_element_type=F32)
    z = jnp.dot(lr.astype(BF16), w2_ref[...], preferred_element_type=F32) + b_ref[...]
    log_sig = jnp.minimum(z, 0.0) - jnp.log(1.0 + jnp.exp(-jnp.abs(z)))
    o_ref[...] = log_sig / GLA_TAU


def gla_log_decay(h, w1, alpha_up, alpha_bias):
    m, d = h.shape
    gq = alpha_up.shape[1]
    w2 = jnp.pad(alpha_up, ((0, LANE - GLA_RANK), (0, 0))).astype(BF16)
    tm = _tile(m, 512)
    return pl.pallas_call(
        _alpha_kernel,
        out_shape=jax.ShapeDtypeStruct((m, gq), F32),
        grid=(m // tm,),
        in_specs=[pl.BlockSpec((tm, d), lambda i: (i, 0)),
                  pl.BlockSpec((d, LANE), lambda i: (0, 0)),
                  pl.BlockSpec((LANE, gq), lambda i: (0, 0)),
                  pl.BlockSpec((1, gq), lambda i: (0, 0))],
        out_specs=pl.BlockSpec((tm, gq), lambda i: (i, 0)),
        compiler_params=_cp("parallel"),
        name="gla_log_decay",
    )(h, w1, w2, alpha_bias.reshape(1, gq))


def _gla_kernel(*refs, chunk, dk, dv, has_s0):
    if has_s0:
        q_ref, k_ref, v0_ref, v1_ref, la_ref, r0_ref, r1_ref, gn_ref, s0_ref, ob_ref, s_ref = refs
    else:
        q_ref, k_ref, v0_ref, v1_ref, la_ref, r0_ref, r1_ref, gn_ref, ob_ref, s_ref = refs

    @pl.when(pl.program_id(1) == 0)
    def _():
        s_ref[...] = s0_ref[...].astype(F32) if has_s0 else jnp.zeros(s_ref.shape, F32)

    ri = lax.broadcasted_iota(I32, (chunk, chunk), 0)
    ci = lax.broadcasted_iota(I32, (chunk, chunk), 1)
    causal = ri >= ci
    tri = causal.astype(F32)
    gn = gn_ref[...]
    per_block = GLA_HEADS // 2
    for h in range(GLA_HEADS):
        ks = slice(h * dk, (h + 1) * dk)
        vs = slice((h % per_block) * dv, (h % per_block + 1) * dv)
        v_ref, r_ref = (v0_ref, r0_ref) if h < per_block else (v1_ref, r1_ref)
        la = la_ref[:, ks]
        b = jnp.dot(tri, la, precision=lax.Precision.HIGHEST,
                    preferred_element_type=F32)
        b_last = b[chunk - 1:chunk, :]
        q = q_ref[:, ks].astype(F32) * (dk ** -0.5)
        k = k_ref[:, ks].astype(F32)
        v = v_ref[:, vs]
        q_dec = (q * jnp.exp(b)).astype(BF16)
        k_inv = (k * jnp.exp(-b)).astype(BF16)
        k_dec = (k * jnp.exp(b_last - b)).astype(BF16)
        att = lax.dot_general(q_dec, k_inv, (((1,), (1,)), ((), ())), preferred_element_type=F32)
        att = jnp.where(causal, att, 0.0).astype(BF16)
        s_old = s_ref[h]
        o = (jnp.dot(att, v, preferred_element_type=F32)
             + jnp.dot(q_dec, s_old.astype(BF16), preferred_element_type=F32))
        kv = lax.dot_general(k_dec, v, (((0,), (0,)), ((), ())), preferred_element_type=F32)
        decay = jnp.sum(jnp.transpose(la), axis=1, keepdims=True)
        s_ref[h] = jnp.exp(decay) * s_old + kv
        y = o * lax.rsqrt(jnp.mean(o * o, axis=-1, keepdims=True) + EPS) * gn
        ob_ref[:, h * dv:(h + 1) * dv] = (y * _silu(r_ref[:, vs].astype(F32))).astype(ob_ref.dtype)


def gla(proj3, la3, gla_norm, s0, layer, dk, dv):
    bsz, t, w = proj3.shape
    chunk = GLA_CHUNK
    qa_w = N_GROUPS * A_WIDTH
    gq, gv = GLA_HEADS * dk, GLA_HEADS * dv
    assert gv == 2 * gq and (3 * qa_w) % gq == 0
    base = 3 * qa_w // gq
    cspec = lambda off: pl.BlockSpec((None, chunk, gq), lambda b, c: (b, c, off))
    ins = [proj3, proj3, proj3, proj3, la3, proj3, proj3, gla_norm.reshape(1, dv)]
    specs = [cspec(base), cspec(base + 1), cspec(base + 2), cspec(base + 3), cspec(0),
             cspec(base + 4), cspec(base + 5), pl.BlockSpec((1, dv), lambda b, c: (0, 0))]
    if s0 is not None:
        ins.append(s0)
        specs.append(pl.BlockSpec((None, None, GLA_HEADS, dk, dv), lambda b, c: (layer, b, 0, 0, 0)))
    return pl.pallas_call(
        functools.partial(_gla_kernel, chunk=chunk, dk=dk, dv=dv, has_s0=s0 is not None),
        out_shape=(jax.ShapeDtypeStruct((bsz, t, gv), BF16),
                   jax.ShapeDtypeStruct((bsz, GLA_HEADS, dk, dv), F32)),
        grid=(bsz, t // chunk),
        in_specs=specs,
        out_specs=(pl.BlockSpec((None, chunk, gv), lambda b, c: (b, c, 0)),
                   pl.BlockSpec((None, GLA_HEADS, dk, dv), lambda b, c: (b, 0, 0, 0))),
        compiler_params=_cp("parallel", "arbitrary"),
        name="gla",
    )(*ins)


def _gate_weights_kernel(w_ref, nxt_ref, o_ref, *, shift, n_keep):
    tn = o_ref.shape[1]
    both = jnp.concatenate([w_ref[...], nxt_ref[...]], axis=0)
    cut = both[shift:shift + tn, :].T
    if n_keep < tn:
        lane = lax.broadcasted_iota(I32, cut.shape, 1)
        cut = jnp.where(lane < n_keep, cut, 0.0)
    o_ref[...] = cut.astype(o_ref.dtype)


def gate_weights(w_nk, col0, n_cols, n_keep=None):
    depth, n, k = w_nk.shape
    tn = _tile(n_cols, 512)
    base = col0 // LANE * LANE
    shift = col0 - base
    n_keep = tn if n_keep is None else n_keep
    assert base % tn == 0 and n_cols % tn == 0 and shift % 8 == 0
    assert col0 + min(n_cols, n_keep) <= n
    last_blk = (n - 1) // LANE
    return pl.pallas_call(
        functools.partial(_gate_weights_kernel, shift=shift, n_keep=n_keep),
        out_shape=jax.ShapeDtypeStruct((depth, k, n_cols), BF16),
        grid=(depth, n_cols // tn),
        in_specs=[pl.BlockSpec((None, tn, k), lambda l, j: (l, base // tn + j, 0)),
                  pl.BlockSpec((None, LANE, k),
                               lambda l, j: (l, jnp.minimum((base + (j + 1) * tn) // LANE, last_blk), 0))],
        out_specs=pl.BlockSpec((None, k, tn), lambda l, j: (l, 0, j)),
        compiler_params=_cp("parallel", "parallel"),
        name="gate_weights",
    )(w_nk, w_nk)


def _merge_kernel(h_ref, oa_ref, ob_ref, wga_ref, wgb_ref, wa_ref, wb_ref, o_ref):
    h = h_ref[...]
    ga = jnp.dot(h, wga_ref[...], preferred_element_type=F32)
    gb = jnp.dot(h, wgb_ref[...], preferred_element_type=F32)
    a = jnp.dot(oa_ref[...], wa_ref[...], preferred_element_type=F32)
    b = jnp.dot(ob_ref[...], wb_ref[...], preferred_element_type=F32)
    o_ref[...] = (jax.nn.sigmoid(ga) * a + jax.nn.sigmoid(gb) * b).astype(o_ref.dtype)


def merge_branches(h, oa, ob, w_gates, w_a, w_b, tm, layer):
    m, d = h.shape
    tn = _tile(d, 512)
    nj = d // tn
    row = lambda width: pl.BlockSpec((tm, width), lambda j, i: (i, 0))
    wcol = lambda rows, off: pl.BlockSpec((rows, tn), lambda j, i: (0, j + off))
    gcol = lambda off: pl.BlockSpec((None, d, tn), lambda j, i: (layer, 0, j + off))
    return pl.pallas_call(
        _merge_kernel,
        out_shape=jax.ShapeDtypeStruct((m, d), BF16),
        grid=(nj, m // tm),
        in_specs=[row(d), row(oa.shape[1]), row(ob.shape[1]),
                  gcol(0), gcol(nj), wcol(w_a.shape[0], 0), wcol(w_b.shape[0], 0)],
        out_specs=pl.BlockSpec((tm, tn), lambda j, i: (i, j)),
        compiler_params=_cp("parallel", "arbitrary"),
        name="merge_branches",
    )(h, oa, ob, w_gates, w_gates, w_a, w_b)


def _pack_rows(x):
    half = x.shape[1] // 2
    lo = lax.bitcast_convert_type(x[:, :half].astype(BF16).astype(F32), U32)
    hi = lax.bitcast_convert_type(x[:, half:].astype(BF16).astype(F32), U32)
    return (lo >> 16) | (hi & jnp.uint32(0xFFFF0000))


def _unpack_chunk(u):
    return (lax.bitcast_convert_type(u << 16, F32),
            lax.bitcast_convert_type(u & jnp.uint32(0xFFFF0000), F32))


def _store_rows(ref, packed, rows):
    n_chunks = packed.shape[1] // LANE
    for c in range(n_chunks):
        ref[pl.ds(c, rows, stride=n_chunks), :] = packed[:, c * LANE:(c + 1) * LANE]


def _load_rows(ref, rows, n_chunks, dtype, start=0):
    lo, hi = [], []
    for c in range(n_chunks):
        a, b = _unpack_chunk(ref[pl.ds(start + c, rows, stride=n_chunks), :])
        lo.append(a.astype(dtype))
        hi.append(b.astype(dtype))
    return jnp.concatenate(lo + hi, axis=1)


def _norm_router_kernel(x_ref, g_ref, sc_ref, sh_ref, wr_ref, rb_ref, hr_ref, idx_ref, wt_ref,
                        cnt_ref):
    h = _norm_mod(x_ref[...], g_ref[...], sc_ref[...], sh_ref[...])
    tm = h.shape[0]
    _store_rows(hr_ref, _pack_rows(h), tm)
    per = N_EXPERTS // N_EXPERT_GROUPS
    logits = lax.dot_general(wr_ref[...], h, (((1,), (1,)), ((), ())),
                             precision=lax.Precision.HIGHEST, preferred_element_type=F32)
    scores = jax.nn.sigmoid(logits)
    biased = scores + rb_ref[...]
    bg = biased.reshape(N_EXPERT_GROUPS, per, tm)
    ip = lax.broadcasted_iota(I32, bg.shape, 1)
    m1 = jnp.max(bg, axis=1, keepdims=True)
    i1 = jnp.min(jnp.where(bg == m1, ip, per), axis=1, keepdims=True)
    m2 = jnp.max(jnp.where(ip == i1, -jnp.inf, bg), axis=1, keepdims=True)
    grp = (m1 + m2)[:, 0, :]
    gi = lax.broadcasted_iota(I32, grp.shape, 0)
    rank = jnp.zeros(grp.shape, I32)
    for g in range(N_EXPERT_GROUPS):
        other = grp[g:g + 1, :]
        ahead = (other > grp) | ((other == grp) & (g < gi))
        rank = rank + ahead.astype(I32)
    sel = (rank < TOPK_GROUPS)[:, None, :]
    masked = jnp.where(sel, bg, -jnp.inf).reshape(N_EXPERTS, tm)
    ie = lax.broadcasted_iota(I32, masked.shape, 0)
    ids, ws = [], []
    chosen = jnp.zeros(masked.shape, F32)
    for _ in range(TOP_K):
        mx = jnp.max(masked, axis=0, keepdims=True)
        ik = jnp.min(jnp.where(masked == mx, ie, N_EXPERTS), axis=0, keepdims=True)
        hit = ie == ik
        ids.append(ik)
        ws.append(jnp.sum(jnp.where(hit, scores, 0.0), axis=0, keepdims=True))
        masked = jnp.where(hit, -jnp.inf, masked)
        chosen = chosen + hit.astype(F32)
    cnt_ref[...] = jnp.sum(chosen, axis=1, keepdims=True).astype(I32)
    tot = ws[0]
    for wk in ws[1:]:
        tot = tot + wk
    pad = TOPK_PAD - TOP_K
    idx_ref[...] = jnp.concatenate(ids + [jnp.zeros((pad, tm), I32)], axis=0)
    wt_ref[...] = jnp.concatenate([wk / tot * ROUTED_SCALE for wk in ws]
                                  + [jnp.zeros((pad, tm), F32)], axis=0)


def norm_router(x, gain, st, comp_shift, comp_scale, w_router_t, router_bias):
    m, d = x.shape
    tm = _tile(m, 256) if st.r == 1 else m
    nc = d // 2 // LANE
    col = lambda i: (i, 0)
    e = w_router_t.shape[0]
    return pl.pallas_call(
        _norm_router_kernel,
        out_shape=(jax.ShapeDtypeStruct((m * nc, LANE), U32),
                   jax.ShapeDtypeStruct((TOPK_PAD, m), I32),
                   jax.ShapeDtypeStruct((TOPK_PAD, m), F32),
                   jax.ShapeDtypeStruct((m // tm, e, 1), I32)),
        grid=(m // tm,),
        in_specs=[pl.BlockSpec((tm, d), lambda i: (i, 0)),
                  pl.BlockSpec((1, d), lambda i: (0, 0)),
                  st.mod_spec(comp_scale, tm, col_of=col),
                  st.mod_spec(comp_shift, tm, col_of=col),
                  pl.BlockSpec((e, d), lambda i: (0, 0)),
                  pl.BlockSpec((e, 1), lambda i: (0, 0))],
        out_specs=(pl.BlockSpec((tm * nc, LANE), lambda i: (i, 0)),
                   pl.BlockSpec((TOPK_PAD, tm), lambda i: (0, i)),
                   pl.BlockSpec((TOPK_PAD, tm), lambda i: (0, i)),
                   pl.BlockSpec((None, e, 1), lambda i: (i, 0, 0))),
        compiler_params=_cp("parallel"),
        name="norm_router",
    )(x, gain.reshape(1, d), st.mod, st.mod, w_router_t, router_bias.reshape(e, 1))


def _row_dma_loop(n_rows, start_one):
    def body(i, carry):
        for j in range(DMA_UNROLL):
            start_one(i * DMA_UNROLL + j, j % 2)
        return carry
    lax.fori_loop(0, n_rows // DMA_UNROLL, body, 0)


def _moe_kernel(be_ref, nb_ref, cnt_ref, first_ref, enext_ref, tok_cur, tok_nxt, asg_ref, h_hbm,
                wg_hbm, wu_hbm, wd_hbm, y_hbm, xbuf, ybuf, stage_a, stage_b, wg_bf, wu_bf, wd_bf,
                sem_in, sem_out, sem_a, sem_b, *, layer):
    i = pl.program_id(0)
    n = pl.num_programs(0)
    nb = nb_ref[0]
    slot = i % 2
    nc = h_hbm.shape[1]
    rows = xbuf.shape[1] // nc
    d_half = stage_a.shape[1]
    f_half = stage_b.shape[1]
    row_at = lambda buf, s, r: buf.at[s, pl.ds(pl.multiple_of(r * nc, nc), nc)]

    def gather(tok_ref, s):
        _row_dma_loop(rows, lambda r, _: pltpu.make_async_copy(
            h_hbm.at[tok_ref[0, 0, r]], row_at(xbuf, s, r), sem_in.at[s]).start())

    def out_copy(s, r, dst_row):
        return pltpu.make_async_copy(row_at(ybuf, s, r), y_hbm.at[dst_row], sem_out.at[s])

    def one_by_one(count, fn):
        def body(r, carry):
            fn(r)
            return carry
        lax.fori_loop(0, count, body, 0)

    def wait_scatter(block, s):
        cnt = cnt_ref[block]

        @pl.when(cnt == rows)
        def _():
            pltpu.make_async_copy(ybuf.at[s], ybuf.at[s], sem_out.at[s]).wait()

        @pl.when(cnt < rows)
        def _():
            one_by_one(cnt, lambda r: out_copy(s, r, 0).wait())

    def up_copy(w_hbm, e, h):
        return pltpu.make_async_copy(w_hbm.at[layer, e, pl.ds(h * d_half, d_half)], stage_a.at[h],
                                     sem_a.at[h])

    def down_copy(e, h):
        return pltpu.make_async_copy(wd_hbm.at[layer, e, pl.ds(h * f_half, f_half)], stage_b.at[h],
                                     sem_b.at[h])

    def prefetch(e):
        for h in range(2):
            up_copy(wg_hbm, e, h).start(priority=1)
            down_copy(e, h).start(priority=1)

    @pl.when(i == 0)
    def _():
        gather(tok_cur, 0)
        prefetch(be_ref[0])

    @pl.when(i + 1 < nb)
    def _():
        gather(tok_nxt, 1 - slot)

    @pl.when(i < nb)
    def _():
        e = be_ref[i]

        @pl.when(first_ref[i] == 1)
        def _():
            for h in range(2):
                up_copy(wg_hbm, e, h).wait()
                wg_bf[pl.ds(h * d_half, d_half), :] = stage_a[h].astype(BF16)
                up_copy(wu_hbm, e, h).start(priority=1)
                down_copy(e, h).wait()
                wd_bf[pl.ds(h * f_half, f_half), :] = stage_b[h].astype(BF16)
            for h in range(2):
                up_copy(wu_hbm, e, h).wait()
                wu_bf[pl.ds(h * d_half, d_half), :] = stage_a[h].astype(BF16)

            @pl.when(enext_ref[i] >= 0)
            def _():
                prefetch(enext_ref[i])

        pltpu.make_async_copy(xbuf.at[slot], xbuf.at[slot], sem_in.at[slot]).wait()

        @pl.when(i >= 2)
        def _():
            wait_scatter(i - 2, slot)
        x = _load_rows(xbuf.at[slot], rows, nc, BF16)
        g = jnp.dot(x, wg_bf[...], preferred_element_type=F32)
        u = jnp.dot(x, wu_bf[...], preferred_element_type=F32)
        hb = (_silu(g) * u).astype(BF16)
        y = jnp.dot(hb, wd_bf[...], preferred_element_type=F32)
        _store_rows(ybuf.at[slot], _pack_rows(y), rows)
        cnt = cnt_ref[i]
        start_row = lambda r: out_copy(slot, r, asg_ref[0, 0, r]).start()

        @pl.when(cnt == rows)
        def _():
            _row_dma_loop(rows, lambda r, prio: out_copy(slot, r, asg_ref[0, 0, r]).start(prio))

        @pl.when(cnt < rows)
        def _():
            one_by_one(cnt, start_row)

    @pl.when(i == n - 1)
    def _():
        @pl.when(nb >= 1)
        def _():
            wait_scatter(nb - 1, (nb - 1) % 2)

        @pl.when(nb >= 2)
        def _():
            wait_scatter(nb - 2, nb % 2)


def moe_experts(h_rows, row_tok, row_asg, block_e, nb_used, block_cnt, wg, wu, wd, layer):
    n_blocks = block_e.shape[0]
    n_tokens, nc = h_rows.shape[:2]
    d = 2 * nc * LANE
    f = wg.shape[-1]
    tok3 = row_tok.reshape(n_blocks, 1, MOE_ROWS)
    asg3 = row_asg.reshape(n_blocks, 1, MOE_ROWS)
    idx = jnp.arange(n_blocks, dtype=I32)
    in_use = idx < nb_used[0]
    prev_e = jnp.concatenate([jnp.full((1,), -1, I32), block_e[:-1]])
    first = (in_use & (block_e != prev_e)).astype(I32)
    later_first = (first[None, :] == 1) & (idx[None, :] > idx[:, None])
    nxt_pos = jnp.min(jnp.where(later_first, idx[None, :], n_blocks), axis=1)
    e_next = jnp.where(nxt_pos < n_blocks, block_e[jnp.minimum(nxt_pos, n_blocks - 1)], -1).astype(I32)
    smem = lambda f_: pl.BlockSpec((1, 1, MOE_ROWS), f_, memory_space=pltpu.SMEM)
    last = n_blocks - 1
    any_spec = pl.BlockSpec(memory_space=pl.ANY)
    grid_spec = pltpu.PrefetchScalarGridSpec(
        num_scalar_prefetch=5,
        grid=(n_blocks,),
        in_specs=[smem(lambda i, *_: (i, 0, 0)),
                  smem(lambda i, *_: (jnp.minimum(i + 1, last), 0, 0)),
                  smem(lambda i, *_: (i, 0, 0)),
                  any_spec, any_spec, any_spec, any_spec],
        out_specs=any_spec,
        scratch_shapes=[pltpu.VMEM((2, MOE_ROWS * nc, LANE), U32),
                        pltpu.VMEM((2, MOE_ROWS * nc, LANE), U32),
                        pltpu.VMEM((2, d // 2, f), F32),
                        pltpu.VMEM((2, f // 2, d), F32),
                        pltpu.VMEM((d, f), BF16), pltpu.VMEM((d, f), BF16), pltpu.VMEM((f, d), BF16),
                        pltpu.SemaphoreType.DMA((2,)), pltpu.SemaphoreType.DMA((2,)),
                        pltpu.SemaphoreType.DMA((2,)), pltpu.SemaphoreType.DMA((2,))],
    )
    return pl.pallas_call(
        functools.partial(_moe_kernel, layer=layer),
        out_shape=jax.ShapeDtypeStruct((n_tokens * TOP_K, nc, LANE), U32),
        grid_spec=grid_spec,
        compiler_params=pltpu.CompilerParams(dimension_semantics=("arbitrary",),
                                             vmem_limit_bytes=MOE_VMEM_LIMIT),
        name="moe_experts",
    )(block_e, nb_used, block_cnt, first, e_next, tok3, tok3, asg3, h_rows, wg, wu, wd)


def moe_dispatch(idx_ts, counts):
    sizes = [ix.shape[1] for ix in idx_ts]
    n_tokens = sum(sizes)
    flat_e = jnp.concatenate([ix.reshape(-1) for ix in idx_ts])
    nk = n_tokens * TOP_K
    order = jnp.argsort(flat_e).astype(I32)
    start = jnp.cumsum(counts) - counts
    padded = (counts + MOE_ROWS - 1) // MOE_ROWS * MOE_ROWS
    pend = jnp.cumsum(padded)
    pad_start = pend - padded
    n_blocks = -(-nk // MOE_ROWS) + N_EXPERTS
    blk_first = jnp.arange(n_blocks, dtype=I32) * MOE_ROWS
    block_e = jnp.minimum(jnp.sum((pend[None, :] <= blk_first[:, None]).astype(I32), axis=1),
                          N_EXPERTS - 1)
    nb_used = (pend[-1] // MOE_ROWS).astype(I32).reshape(1)
    blk_count, blk_start, blk_pad = lax.optimization_barrier(
        (counts[block_e], start[block_e], pad_start[block_e]))
    within = jnp.arange(MOE_ROWS, dtype=I32)[None, :]
    k_p = blk_first[:, None] + within - blk_pad[:, None]
    valid = ((k_p < blk_count[:, None]) & (blk_first[:, None] < pend[-1])).reshape(-1)
    src = jnp.clip(blk_start[:, None] + k_p, 0, nk - 1).reshape(-1)
    a_p = order[src]
    tok = jnp.zeros_like(a_p)
    base_a, base_t = 0, 0
    for m in sizes:
        local = a_p - base_a
        tok = jnp.where((local >= 0) & (local < TOP_K * m), base_t + local % m, tok)
        base_a += TOP_K * m
        base_t += m
    row_tok = jnp.where(valid, tok, 0).astype(I32)
    row_asg = jnp.where(valid, a_p, 0).astype(I32)
    block_cnt = jnp.sum(valid.reshape(n_blocks, MOE_ROWS).astype(I32), axis=1)
    return row_tok, row_asg, block_e.astype(I32), nb_used, block_cnt


def _finish_kernel(*refs):
    h_ref = refs[0]
    y_refs = refs[1:1 + TOP_K]
    wt_ref, x_ref, g2_ref, wg_ref, wu_ref, wd_ref, o_ref = refs[1 + TOP_K:]
    tm, d = x_ref.shape
    nc = d // 2 // LANE
    h = _load_rows(h_ref, tm, nc, BF16)
    g = jnp.dot(h, wg_ref[...], preferred_element_type=F32)
    u = jnp.dot(h, wu_ref[...], preferred_element_type=F32)
    acc = jnp.dot((_silu(g) * u).astype(BF16), wd_ref[...], preferred_element_type=F32)
    wt = wt_ref[...]
    for k in range(TOP_K):
        acc = acc + wt[:, k:k + 1] * _load_rows(y_refs[k], tm, nc, F32)
    o_ref[...] = x_ref[...] + g2_ref[...] * acc


def moe_finish(x1, st, h_rows2, wts, y_rows2, wgs, wus, wds, row_off, asg_off):
    m, d = x1.shape
    f = wgs.shape[1]
    tm = _tile(m, 256)
    nt = m // tm
    assert row_off % tm == 0 and asg_off % tm == 0
    off = row_off // tm
    nc = d // 2 // LANE
    rows_spec = lambda base: pl.BlockSpec((tm * nc, LANE), lambda i: (i + base, 0))
    whole = lambda a, b: pl.BlockSpec((a, b), lambda i: (0, 0), pipeline_mode=pl.Buffered(1))
    return pl.pallas_call(
        _finish_kernel,
        out_shape=jax.ShapeDtypeStruct((m, d), F32),
        grid=(nt,),
        in_specs=[rows_spec(off)] + [rows_spec(asg_off // tm + k * nt) for k in range(TOP_K)]
                 + [pl.BlockSpec((tm, TOPK_PAD), lambda i: (i + off, 0)),
                    pl.BlockSpec((tm, d), lambda i: (i, 0)),
                    st.mod_spec(5, tm, col_of=lambda i: (i, 0)),
                    whole(d, f), whole(d, f), whole(f, d)],
        out_specs=pl.BlockSpec((tm, d), lambda i: (i, 0)),
        compiler_params=_cp("parallel"),
        name="moe_finish",
    )(h_rows2, *([y_rows2] * TOP_K), wts, x1, st.mod, wgs, wus, wds)


def _residual_epilogue(acc, x, gate):
    return x + gate * acc


def decoder_layer(l, xp, xs, c_all, p, caches_k, caches_v, state_gla, dims):
    batch, seq, db, t_new = dims
    d = xp.shape[1]
    mp, ms = xp.shape[0], xs.shape[0]
    n_all = mp + ms
    qa_w = N_GROUPS * A_WIDTH
    dk, dv = d // 2 // GLA_HEADS, d // GLA_HEADS
    gq, gv = GLA_HEADS * dk, GLA_HEADS * dv
    n_main = 3 * qa_w + 2 * gq + 2 * gv
    nc = d // 2 // LANE

    mod = ada_mod(c_all, p['w_ada'], p['b_ada'], l)
    st_p = Stream(mp, _tile(mp, 1024), mod[:batch].reshape(batch, 1, 6 * d), seq,
                  jnp.arange(seq))
    st_s = Stream(ms, ms, jnp.repeat(mod[batch:], t_new, axis=0).reshape(1, ms, 6 * d), 0,
                  jnp.tile(PAST_LEN + jnp.arange(t_new), db))
    w_lr = p['w_lr'][l]
    w_a, w_b = p['w_branch_a'][l].astype(BF16), p['w_branch_b'][l].astype(BF16)
    w_out = p['w_out'][l].astype(BF16)
    w_router_t = p['w_router'][l].T

    new_k, new_v, new_s, x1s, idxs, wtss, hrs, cnts = {}, {}, {}, {}, [], [], [], []
    for name, st, x in (('p', st_p, xp), ('s', st_s, xs)):
        m, tm = st.rows, st.tm
        h1 = norm_mod(x, p['norm_mix'][l], st, 0, 1)
        proj = matmul(h1, p['w_in'], n_main, tm=tm, tn=_tile(n_main, 1024), out_dtype=BF16,
                      name="in_proj", layer=l, w_is_nk=True)
        la = gla_log_decay(h1, w_lr, p['alpha_up'][l], p['alpha_bias'][l])
        tables = rope_tables(st.pos)
        if name == 'p':
            dils = [dil for _, dil in DIL_GROUPS]
            keeps = [min(w_, seq) for w_, _ in DIL_GROUPS]
            qkv, kc, vc = qk_prep(proj, p['q_norm'][l], p['k_norm'][l], tables, dils, batch, seq,
                                  keeps)
            outs, lses = zip(*[attn_prompt(qkv[3 * g], qkv[3 * g + 1], qkv[3 * g + 2], g)
                               for g in range(N_GROUPS)])
            oa = combine_groups(outs, lses)
            ob, s_new = gla(proj.reshape(batch, seq, -1), la.reshape(batch, seq, gq),
                            p['gla_norm'][l], None, l, dk, dv)
            ob = ob.reshape(m, gv)
            shape4 = lambda a, keep: a.reshape(batch, keep, A_HEADS, A_HEAD_DIM)
            new_k[name] = [shape4(a, keep) for a, keep in zip(kc, keeps)]
            new_v[name] = [shape4(a, keep) for a, keep in zip(vc, keeps)]
        else:
            qkv, kc, vc = qk_prep(proj, p['q_norm'][l], p['k_norm'][l], tables, [1] * N_GROUPS,
                                  1, m, [m] * N_GROUPS)
            new5 = lambda which: jnp.stack(
                [qkv[3 * g + which].reshape(db, t_new, A_HEADS, A_HEAD_DIM)
                 for g in range(N_GROUPS)], axis=2).astype(F32)
            o5, l5 = attn_sample(new5(0), new5(1), new5(2), caches_k, caches_v, l)
            outs = [o5[:, :, g].reshape(1, 1, m, A_WIDTH).astype(BF16) for g in range(N_GROUPS)]
            lses = [jnp.pad(l5[:, :, g, :, 0].reshape(m, A_HEADS),
                            ((0, 0), (0, LANE - A_HEADS))).reshape(1, 1, m, LANE)
                    for g in range(N_GROUPS)]
            oa = combine_groups(outs, lses)
            padc = lambda a: jnp.pad(a.reshape(db, t_new, -1),
                                     ((0, 0), (0, GLA_CHUNK - t_new), (0, 0)))
            ob, s_new = gla(padc(proj), padc(la), p['gla_norm'][l], state_gla, l, dk, dv)
            ob = ob[:, :t_new].reshape(m, gv)
            shape4 = lambda a: a.reshape(db, t_new, A_HEADS, A_HEAD_DIM)
            new_k[name] = [shape4(a) for a in kc]
            new_v[name] = [shape4(a) for a in vc]
        new_s[name] = s_new
        merged = merge_branches(h1, oa, ob, p['w_gates'], w_a, w_b, tm, l)
        tn = _tile(d, 1024)
        x1 = matmul(merged, w_out, d, tm=tm, tn=tn, out_dtype=F32,
                    extra=(x, st.mod),
                    extra_specs=(pl.BlockSpec((tm, tn), lambda j, i: (i, j)),
                                 st.mod_spec(2, tm, tn, col_of=lambda j, i: (i, j))),
                    epilogue=_residual_epilogue, name="out_proj")
        h_rows, idx_t, wt_t, tile_cnt = norm_router(x1, p['norm_ffn'][l], st, 3, 4, w_router_t,
                                                    p['router_bias'][l])
        x1s[name] = x1
        hrs.append(h_rows)
        cnts.append(jnp.sum(tile_cnt[:, :, 0], axis=0))
        idxs.append(idx_t[:TOP_K])
        wtss.append(wt_t)

    wts_all = jnp.concatenate(wtss, axis=1).T
    h_rows2 = jnp.concatenate(hrs, axis=0)
    row_tok, row_asg, block_e, nb_used, block_cnt = moe_dispatch(idxs, cnts[0] + cnts[1])
    y_rows = moe_experts(h_rows2.reshape(n_all, nc, LANE), row_tok, row_asg, block_e, nb_used,
                         block_cnt, p['w_gate_e'], p['w_up_e'], p['w_down_e'], l)
    y_rows2 = y_rows.reshape(-1, LANE)
    wgs, wus, wds = (p['w_gate_s'][l].astype(BF16), p['w_up_s'][l].astype(BF16),
                     p['w_down_s'][l].astype(BF16))
    xp2 = moe_finish(x1s['p'], st_p, h_rows2, wts_all, y_rows2, wgs, wus, wds, 0, 0)
    xs2 = moe_finish(x1s['s'], st_s, h_rows2, wts_all, y_rows2, wgs, wus, wds, mp, TOP_K * mp)
    return xp2, xs2, new_k, new_v, new_s


def kernel(x_prompt, x_sample, cache_k_w128, cache_v_w128, cache_k_w512, cache_v_w512, cache_k_w2048, cache_v_w2048, state_gla, c_prompt, c_sample, w_in, q_norm, k_norm, alpha_up, alpha_bias, gla_norm, w_branch_a, w_branch_b, w_out, norm_mix, norm_ffn, w_ada, b_ada, w_router, router_bias, w_gate_e, w_up_e, w_down_e, w_gate_s, w_up_s, w_down_s):
    batch, seq, d = x_prompt.shape
    db, t_new, _ = x_sample.shape
    depth = w_in.shape[0]
    dims = (batch, seq, db, t_new)
    n_main = w_in.shape[-1] - GLA_RANK - 2 * d
    w_in_nk = jnp.swapaxes(w_in, 1, 2)
    p = dict(w_in=w_in_nk, w_lr=gate_weights(w_in_nk, n_main, LANE, n_keep=GLA_RANK),
             w_gates=gate_weights(w_in_nk, n_main + GLA_RANK, 2 * d),
             q_norm=q_norm, k_norm=k_norm, alpha_up=alpha_up, alpha_bias=alpha_bias,
             gla_norm=gla_norm, w_branch_a=w_branch_a, w_branch_b=w_branch_b, w_out=w_out,
             norm_mix=norm_mix, norm_ffn=norm_ffn, w_ada=w_ada, b_ada=b_ada, w_router=w_router,
             router_bias=router_bias, w_gate_s=w_gate_s, w_up_s=w_up_s, w_down_s=w_down_s,
             w_gate_e=w_gate_e, w_up_e=w_up_e, w_down_e=w_down_e)
    caches_k = (cache_k_w128, cache_k_w512, cache_k_w2048)
    caches_v = (cache_v_w128, cache_v_w512, cache_v_w2048)
    c_all = jnp.concatenate([c_prompt, c_sample], axis=0)
    xp = x_prompt.reshape(batch * seq, d)
    xs = x_sample.reshape(db * t_new, d)
    ks, vs, ss = {'p': [], 's': []}, {'p': [], 's': []}, {'p': [], 's': []}
    for l in range(depth):
        xp, xs, nk, nv, ns = decoder_layer(l, xp, xs, c_all, p, caches_k, caches_v, state_gla, dims)
        for name in ('p', 's'):
            ks[name].append(nk[name])
            vs[name].append(nv[name])
            ss[name].append(ns[name])
    out = [xp.reshape(batch, seq, d), xs.reshape(db, t_new, d)]
    for name in ('p', 's'):
        for g in range(N_GROUPS):
            out.append(jnp.stack([k_[g] for k_ in ks[name]]))
            out.append(jnp.stack([v_[g] for v_ in vs[name]]))
        out.append(jnp.stack(ss[name]))
    return tuple(out)
```
